```python
import math
import jax
import jax.numpy as jnp
from jax import lax
import numpy as np

D_MODEL = 1024
BATCH = 2
SEQ = 16384
DEPTH = 2
DEC_BATCH = 16
DEC_SEQ = 32
PAST_LEN = 1024

CHUNK = 64
D_FF = 2816
CONV_CH = 512
CONV_WIDTH = 31
SSD_HEADS = 16
SSD_HEAD_DIM = 64
SSD_INNER = SSD_HEADS * SSD_HEAD_DIM
SSD_GROUPS = 2
SSD_STATE = 128
SSD_CONV_WIDTH = 4
SSD_CONV_CH = SSD_INNER + 2 * SSD_GROUPS * SSD_STATE
SSD_CHUNK = CHUNK
ATT_HEADS = 4
ATT_QK_DIM = 64
ATT_V_DIM = 2 * ATT_QK_DIM
ATT_WIDTH = ATT_HEADS * ATT_V_DIM
ATT_Q_BLOCK = 128
ATT_SCALE = ATT_QK_DIM ** -0.5
ALIBI_SLOPES = tuple(2.0 ** (-8.0 * (h + 1) / ATT_HEADS) for h in range(ATT_HEADS))
N_BRANCH = 3
N_SUB = 3
EPS = 1e-6
IN_SPLITS = (2 * CONV_CH, SSD_INNER, SSD_CONV_CH, SSD_HEADS,
             2 * ATT_HEADS * ATT_QK_DIM, 2 * ATT_HEADS * ATT_QK_DIM, ATT_WIDTH,
             N_BRANCH * D_MODEL)
N_IN = sum(IN_SPLITS)

kernel_name = 'hybrid_stream_conv_ssd_diffattn'


def _split_points():
    pts, acc = [], 0
    for s in IN_SPLITS[:-1]:
        acc += s
        pts.append(acc)
    return pts


def rms_norm(x, g, eps=EPS):
    xf = x.astype(jnp.float32)
    y = xf * lax.rsqrt(jnp.mean(xf * xf, axis=-1, keepdims=True) + eps)
    return (y * g.astype(jnp.float32)).astype(x.dtype)


def layer_norm(x, g, b, eps=1e-5):
    xf = x.astype(jnp.float32)
    mu = jnp.mean(xf, axis=-1, keepdims=True)
    var = jnp.mean(jnp.square(xf - mu), axis=-1, keepdims=True)
    y = (xf - mu) * lax.rsqrt(var + eps)
    return (y * g.astype(jnp.float32) + b.astype(jnp.float32)).astype(x.dtype)


def gated_group_rms(y, z, g):
    b, L, dn = y.shape
    yz = (y.astype(jnp.float32) * jax.nn.silu(z.astype(jnp.float32))).reshape(b, L, SSD_GROUPS, dn // SSD_GROUPS)
    yz = yz * lax.rsqrt(jnp.mean(yz * yz, axis=-1, keepdims=True) + EPS)
    return (yz.reshape(b, L, dn) * g.astype(jnp.float32)).astype(y.dtype)


def modulate(x, g, shift, scale):
    return rms_norm(x, g) * (1.0 + scale[:, None, :]) + shift[:, None, :]


def swiglu(h, w_in, w_out):
    up, gt = jnp.split(h @ w_in, 2, axis=-1)
    return (jax.nn.silu(gt) * up) @ w_out


def causal_dwconv(x, buf, w, b):
    width = w.shape[0]
    xp = jnp.concatenate([buf.astype(x.dtype), x], axis=1)
    y = lax.conv_general_dilated(xp, w[:, None, :].astype(x.dtype), (1,), 'VALID',
                                 dimension_numbers=('NWC', 'WIO', 'NWC'),
                                 feature_group_count=x.shape[-1])
    return y + b, xp[:, xp.shape[1] - (width - 1):]


def ssd_scan(x, dt, A, B, C, h0):
    bsz, L, H, P = x.shape
    G, N = B.shape[2], B.shape[3]
    R = H // G
    Q = SSD_CHUNK if L % SSD_CHUNK == 0 else L
    nc = L // Q
    f32 = jnp.float32
    xdt = (x.astype(f32) * dt[..., None]).reshape(bsz, nc, Q, G, R, P)
    a_cum = jnp.cumsum((dt * A).reshape(bsz, nc, Q, G, R), axis=2)
    Bc = B.astype(f32).reshape(bsz, nc, Q, G, N)
    Cc = C.astype(f32).reshape(bsz, nc, Q, G, N)
    causal = jnp.tril(jnp.ones((Q, Q), dtype=bool))[None, None, :, :, None, None]
    seg = a_cum[:, :, :, None] - a_cum[:, :, None, :]
    decay_ls = jnp.exp(jnp.where(causal, seg, -jnp.inf))
    cb = jnp.einsum('bclgn,bcsgn->bclsg', Cc, Bc)
    y_diag = jnp.einsum('bclsg,bclsgr,bcsgrp->bclgrp', cb, decay_ls, xdt)
    decay_to_end = jnp.exp(a_cum[:, :, -1:] - a_cum)
    chunk_states = jnp.einsum('bcsgn,bcsgr,bcsgrp->bcgrpn', Bc, decay_to_end, xdt)
    chunk_decay = jnp.exp(a_cum[:, :, -1])

    def step(h, inp):
        s_c, d_c = inp
        return h * d_c[..., None, None] + s_c, h

    h_last, h_prev = lax.scan(step, h0.astype(f32).reshape(bsz, G, R, P, N),
                              (jnp.moveaxis(chunk_states, 1, 0), jnp.moveaxis(chunk_decay, 1, 0)))
    h_prev = jnp.moveaxis(h_prev, 0, 1)
    y_off = jnp.einsum('bclgn,bcgrpn,bclgr->bclgrp', Cc, h_prev, jnp.exp(a_cum))
    y = (y_diag + y_off).reshape(bsz, L, H, P)
    return y.astype(x.dtype), h_last.reshape(bsz, H, P, N).astype(h0.dtype)


def diff_attend(q, k, v, qpos, kpos, lam):
    s = jnp.einsum('bqhcd,bkhcd->bhcqk', q, k).astype(jnp.float32) * ATT_SCALE
    dist = jnp.abs(qpos[:, None] - kpos[None, :]).astype(jnp.float32)
    slopes = jnp.asarray(ALIBI_SLOPES, dtype=jnp.float32)
    s = s - slopes[None, :, None, None, None] * dist[None, None, None]
    visible = (kpos[None, :] // CHUNK) <= (qpos[:, None] // CHUNK)
    p = jax.nn.softmax(jnp.where(visible, s, -jnp.inf), axis=-1)
    a = p[:, :, 0] - lam * p[:, :, 1]
    return jnp.einsum('bhqk,bkhe->bqhe', a.astype(v.dtype), v)


def prompt_attend(q, k, v, lam):
    b, S = q.shape[0], q.shape[1]
    nb = S // ATT_Q_BLOCK
    qb = jnp.moveaxis(q.reshape(b, nb, ATT_Q_BLOCK, ATT_HEADS, 2, ATT_QK_DIM), 1, 0)
    kpos = jnp.arange(S, dtype=jnp.int32)

    def one(args):
        q_blk, start = args
        qpos = start + jnp.arange(ATT_Q_BLOCK, dtype=jnp.int32)
        return diff_attend(q_blk, k, v, qpos, kpos, lam)

    o = lax.map(one, (qb, jnp.arange(nb, dtype=jnp.int32) * ATT_Q_BLOCK))
    return jnp.moveaxis(o, 0, 1).reshape(b, S, ATT_HEADS, ATT_V_DIM)


def make_sample_attend(cache_k, cache_v):
    def attend(q, k, v, lam):
        b, Ls = q.shape[0], q.shape[1]
        P = cache_k.shape[1]
        kk = jnp.concatenate([cache_k.reshape(b, P, ATT_HEADS, 2, ATT_QK_DIM).astype(k.dtype), k], axis=1)
        vv = jnp.concatenate([cache_v.astype(v.dtype), v], axis=1)
        qpos = P + jnp.arange(Ls, dtype=jnp.int32)
        kpos = jnp.arange(P + Ls, dtype=jnp.int32)
        return diff_attend(q, kk, vv, qpos, kpos, lam)
    return attend


def lambda_init(layer):
    return 0.8 - 0.6 * math.exp(-0.3 * layer)


def token_mix(h, p, layer, conv_buf, ssd_conv_buf, ssd_h0, attend):
    b, L, _ = h.shape
    u = h @ p['w_in']
    glu, z, xbc, dt_raw, q, k, v, gate_logits = jnp.split(u, _split_points(), axis=-1)
    ga, gg = jnp.split(glu, 2, axis=-1)
    a = ga * jax.nn.sigmoid(gg)
    a, conv_new = causal_dwconv(a, conv_buf, p['conv_dw_w'], p['conv_dw_b'])
    a_out = jax.nn.silu(layer_norm(a, p['conv_ln_g'], p['conv_ln_b'])) @ p['w_br_conv']
    xbc, ssd_conv_new = causal_dwconv(xbc, ssd_conv_buf, p['ssd_conv_w'], p['ssd_conv_b'])
    xbc = jax.nn.silu(xbc)
    xs, Bm, Cm = jnp.split(xbc, [SSD_INNER, SSD_INNER + SSD_GROUPS * SSD_STATE], axis=-1)
    dt = jax.nn.softplus((dt_raw + p['ssd_dt_bias']).astype(jnp.float32))
    A = -jnp.exp(p['ssd_A_log'].astype(jnp.float32))
    xs_h = xs.reshape(b, L, SSD_HEADS, SSD_HEAD_DIM)
    y, h_new = ssd_scan(xs_h, dt, A, Bm.reshape(b, L, SSD_GROUPS, SSD_STATE),
                        Cm.reshape(b, L, SSD_GROUPS, SSD_STATE), ssd_h0)
    y = y + p['ssd_D'][:, None] * xs_h
    y = gated_group_rms(y.reshape(b, L, SSD_INNER), z, p['ssd_norm_g'])
    b_out = y @ p['w_br_ssd']
    q = q.reshape(b, L, ATT_HEADS, 2, ATT_QK_DIM)
    k = k.reshape(b, L, ATT_HEADS, 2, ATT_QK_DIM)
    v = v.reshape(b, L, ATT_HEADS, ATT_V_DIM)
    lq = p['lambda_q'].astype(jnp.float32)
    lk = p['lambda_k'].astype(jnp.float32)
    lam0 = lambda_init(layer)
    lam = jnp.exp(jnp.sum(lq[0] * lk[0])) - jnp.exp(jnp.sum(lq[1] * lk[1])) + lam0
    o = attend(q, k, v, lam)
    o = rms_norm(o, p['attn_subln_g'], 1e-5) * (1.0 - lam0)
    c_out = o.reshape(b, L, ATT_WIDTH) @ p['w_br_attn']
    gates = jax.nn.sigmoid(gate_logits.reshape(b, L, N_BRANCH, D_MODEL) + p['b_gate'])
    merged = gates[:, :, 0] * a_out + gates[:, :, 1] * b_out + gates[:, :, 2] * c_out
    out = merged @ p['w_mix_out']
    k_rows = k.reshape(b, L, ATT_HEADS, 2 * ATT_QK_DIM)
    return out, (k_rows, v, conv_new, ssd_conv_new, h_new)


def trunk_layer(x, c, p, layer, conv_buf, ssd_conv_buf, ssd_h0, attend):
    mod = (jax.nn.silu(c) @ p['w_ada'] + p['b_ada']).reshape(c.shape[0], N_SUB, 3, D_MODEL)
    shift, scale, gate = mod[:, :, 0], mod[:, :, 1], mod[:, :, 2]
    h = modulate(x, p['norm_pre'][0], shift[:, 0], scale[:, 0])
    f = swiglu(h, p['w_ffn_in'][0], p['w_ffn_out'][0])
    x = x + 0.5 * gate[:, 0, None, :] * rms_norm(f, p['norm_post'][0])
    h = modulate(x, p['norm_pre'][1], shift[:, 1], scale[:, 1])
    m, new_state = token_mix(h, p, layer, conv_buf, ssd_conv_buf, ssd_h0, attend)
    x = x + gate[:, 1, None, :] * rms_norm(m, p['norm_post'][1])
    h = modulate(x, p['norm_pre'][2], shift[:, 2], scale[:, 2])
    f = swiglu(h, p['w_ffn_in'][1], p['w_ffn_out'][1])
    x = x + 0.5 * gate[:, 2, None, :] * rms_norm(f, p['norm_post'][2])
    return x, new_state


def setup_inputs(seed: int = 0) -> dict:
    key = jax.random.key(seed)
    ks = iter(list(jax.random.split(key, 40)))

    def nrm(shape, scale=1.0):
        return jax.random.normal(next(ks), shape, dtype=jnp.float32) * scale

    dt0 = jnp.exp(jax.random.uniform(next(ks), (DEPTH, SSD_HEADS), minval=math.log(1e-3), maxval=math.log(1e-1)))
    dt_bias = dt0 + jnp.log(-jnp.expm1(-dt0))
    a_log = jnp.log(jax.random.uniform(next(ks), (DEPTH, SSD_HEADS), minval=1.0, maxval=16.0))
    return {
        'x_prompt': nrm((BATCH, SEQ, D_MODEL)),
        'x_sample': nrm((DEC_BATCH, DEC_SEQ, D_MODEL)),
        'cache_attn_k': nrm((DEPTH, DEC_BATCH, PAST_LEN, ATT_HEADS, 2 * ATT_QK_DIM)),
        'cache_attn_v': nrm((DEPTH, DEC_BATCH, PAST_LEN, ATT_HEADS, ATT_V_DIM)),
        'state_conv': nrm((DEPTH, DEC_BATCH, CONV_WIDTH - 1, CONV_CH), 0.5),
        'state_ssd_conv': nrm((DEPTH, DEC_BATCH, SSD_CONV_WIDTH - 1, SSD_CONV_CH)),
        'state_ssd': nrm((DEPTH, DEC_BATCH, SSD_HEADS, SSD_HEAD_DIM, SSD_STATE), 0.1),
        'c_prompt': nrm((BATCH, D_MODEL)),
        'c_sample': nrm((DEC_BATCH, D_MODEL)),
        'w_ada': nrm((DEPTH, D_MODEL, N_SUB * 3 * D_MODEL), D_MODEL ** -0.5),
        'b_ada': nrm((DEPTH, N_SUB * 3 * D_MODEL), 0.01),
        'norm_pre': 1.0 + nrm((DEPTH, N_SUB, D_MODEL), 0.05),
        'norm_post': 1.0 + nrm((DEPTH, N_SUB, D_MODEL), 0.05),
        'w_ffn_in': nrm((DEPTH, 2, D_MODEL, 2 * D_FF), D_MODEL ** -0.5),
        'w_ffn_out': nrm((DEPTH, 2, D_FF, D_MODEL), D_FF ** -0.5),
        'w_in': nrm((DEPTH, D_MODEL, N_IN), D_MODEL ** -0.5),
        'b_gate': nrm((DEPTH, N_BRANCH, D_MODEL), 0.01),
        'conv_dw_w': nrm((DEPTH, CONV_WIDTH, CONV_CH), CONV_WIDTH ** -0.5),
        'conv_dw_b': nrm((DEPTH, CONV_CH), 0.01),
        'conv_ln_g': 1.0 + nrm((DEPTH, CONV_CH), 0.05),
        'conv_ln_b': nrm((DEPTH, CONV_CH), 0.01),
        'w_br_conv': nrm((DEPTH, CONV_CH, D_MODEL), CONV_CH ** -0.5),
        'ssd_conv_w': nrm((DEPTH, SSD_CONV_WIDTH, SSD_CONV_CH), SSD_CONV_WIDTH ** -0.5),
        'ssd_conv_b': nrm((DEPTH, SSD_CONV_CH), 0.01),
        'ssd_dt_bias': dt_bias,
        'ssd_A_log': a_log,
        'ssd_D': 1.0 + nrm((DEPTH, SSD_HEADS), 0.1),
        'ssd_norm_g': 1.0 + nrm((DEPTH, SSD_INNER), 0.05),
        'w_br_ssd': nrm((DEPTH, SSD_INNER, D_MODEL), SSD_INNER ** -0.5),
        'lambda_q': nrm((DEPTH, 2, ATT_QK_DIM), 0.1),
        'lambda_k': nrm((DEPTH, 2, ATT_QK_DIM), 0.1),
        'attn_subln_g': 1.0 + nrm((DEPTH, ATT_V_DIM), 0.05),
        'w_br_attn': nrm((DEPTH, ATT_WIDTH, D_MODEL), ATT_WIDTH ** -0.5),
        'w_mix_out': nrm((DEPTH, D_MODEL, D_MODEL), D_MODEL ** -0.5),
    }


def reference(x_prompt, x_sample, cache_attn_k, cache_attn_v, state_conv, state_ssd_conv, state_ssd,
              c_prompt, c_sample, w_ada, b_ada, norm_pre, norm_post, w_ffn_in, w_ffn_out, w_in, b_gate,
              conv_dw_w, conv_dw_b, conv_ln_g, conv_ln_b, w_br_conv, ssd_conv_w, ssd_conv_b, ssd_dt_bias,
              ssd_A_log, ssd_D, ssd_norm_g, w_br_ssd, lambda_q, lambda_k, attn_subln_g, w_br_attn, w_mix_out):
    bp = x_prompt.shape[0]
    dtp = x_prompt.dtype
    conv0 = jnp.zeros((bp, CONV_WIDTH - 1, CONV_CH), dtp)
    ssd_conv0 = jnp.zeros((bp, SSD_CONV_WIDTH - 1, SSD_CONV_CH), dtp)
    ssd0 = jnp.zeros((bp, SSD_HEADS, SSD_HEAD_DIM, SSD_STATE), dtp)
    xp, xs = x_prompt, x_sample
    st_p, st_s = [], []
    for l in range(DEPTH):
        p = dict(w_ada=w_ada[l], b_ada=b_ada[l], norm_pre=norm_pre[l], norm_post=norm_post[l],
                 w_ffn_in=w_ffn_in[l], w_ffn_out=w_ffn_out[l], w_in=w_in[l], b_gate=b_gate[l],
                 conv_dw_w=conv_dw_w[l], conv_dw_b=conv_dw_b[l], conv_ln_g=conv_ln_g[l], conv_ln_b=conv_ln_b[l],
                 w_br_conv=w_br_conv[l], ssd_conv_w=ssd_conv_w[l], ssd_conv_b=ssd_conv_b[l],
                 ssd_dt_bias=ssd_dt_bias[l], ssd_A_log=ssd_A_log[l], ssd_D=ssd_D[l], ssd_norm_g=ssd_norm_g[l],
                 w_br_ssd=w_br_ssd[l], lambda_q=lambda_q[l], lambda_k=lambda_k[l],
                 attn_subln_g=attn_subln_g[l], w_br_attn=w_br_attn[l], w_mix_out=w_mix_out[l])
        xp, sp = trunk_layer(xp, c_prompt, p, l, conv0, ssd_conv0, ssd0, prompt_attend)
        xs, ss = trunk_layer(xs, c_sample, p, l, state_conv[l], state_ssd_conv[l], state_ssd[l],
                             make_sample_attend(cache_attn_k[l], cache_attn_v[l]))
        st_p.append(sp)
        st_s.append(ss)
    k_prompt = jnp.stack([s[0] for s in st_p])
    v_prompt = jnp.stack([s[1] for s in st_p])
    conv_prompt = jnp.stack([s[2] for s in st_p])
    ssd_conv_prompt = jnp.stack([s[3] for s in st_p])
    ssd_prompt = jnp.stack([s[4] for s in st_p])
    k_sample = jnp.stack([s[0] for s in st_s])
    v_sample = jnp.stack([s[1] for s in st_s])
    conv_sample = jnp.stack([s[2] for s in st_s])
    ssd_conv_sample = jnp.stack([s[3] for s in st_s])
    ssd_sample = jnp.stack([s[4] for s in st_s])
    return (xp, xs, k_prompt, v_prompt, conv_prompt, ssd_conv_prompt, ssd_prompt,
            k_sample, v_sample, conv_sample, ssd_conv_sample, ssd_sample)
```

```python
import functools
import math

import numpy as np
import jax
import jax.numpy as jnp
from jax import lax
from jax.experimental import pallas as pl
from jax.experimental.pallas import tpu as pltpu

F32, BF16 = jnp.float32, jnp.bfloat16

D_MODEL = 1024
D_FF = 2816
CONV_CH = 512
CONV_WIDTH = 31
SSD_HEADS = 16
SSD_HEAD_DIM = 64
SSD_INNER = SSD_HEADS * SSD_HEAD_DIM
SSD_GROUPS = 2
SSD_STATE = 128
SSD_CONV_WIDTH = 4
SSD_CONV_CH = SSD_INNER + 2 * SSD_GROUPS * SSD_STATE
CHUNK = 64
CHUNK_SHIFT = 6
ATT_HEADS = 4
ATT_QK_DIM = 64
ATT_V_DIM = 2 * ATT_QK_DIM
ATT_WIDTH = ATT_HEADS * ATT_V_DIM
ATT_SCALE = ATT_QK_DIM ** -0.5
ALIBI_SLOPES = tuple(2.0 ** (-8.0 * (h + 1) / ATT_HEADS) for h in range(ATT_HEADS))
N_BRANCH = 3
N_SUB = 3
EPS = 1e-6
SUBLN_EPS = 1e-5
CONV_LN_EPS = 1e-5
NEG_BIG = -1e30

VMEM_LIMIT_BYTES = 58 * 1024 * 1024
LANES = 128
SUBLANES = 8

FFN_ROWS = 512
FFN_CHUNKS = 2
MIX_ROWS = 256
ATT_BLOCK = 256
ADA_COLS = 1152
CONV_PAD = 32
SCONV_PAD = 8


def _lambda_init(layer):
    return 0.8 - 0.6 * math.exp(-0.3 * layer)


def _params(sem):
    return pltpu.CompilerParams(dimension_semantics=sem, vmem_limit_bytes=VMEM_LIMIT_BYTES)


def _resident(shape):
    nd = len(shape)
    return pl.BlockSpec(shape, lambda *_: (0,) * nd, pipeline_mode=pl.Buffered(1))


def _dot(a, b):
    return jnp.dot(a, b, preferred_element_type=F32)


def _dot_nt(a, b):
    return lax.dot_general(a, b, (((1,), (1,)), ((), ())), preferred_element_type=F32)


def _dot_tn(a, b):
    return lax.dot_general(a, b, (((0,), (0,)), ((), ())), preferred_element_type=F32)


def _rms(x, g, eps=EPS):
    return x * lax.rsqrt(jnp.mean(x * x, axis=-1, keepdims=True) + eps) * g


def _silu(x):
    return x * jax.nn.sigmoid(x)


def _split3(a):
    hi = a.astype(BF16)
    r = a - hi.astype(F32)
    mid = r.astype(BF16)
    lo = (r - mid.astype(F32)).astype(BF16)
    return hi, mid, lo


def _ada_kernel(c_ref, w_ref, b_ref, o_ref):
    sc = _silu(c_ref[...]).astype(BF16)
    o_ref[...] = _dot(sc, w_ref[...].astype(BF16)) + b_ref[...]


def _ada_call(c_all, w_ada, b_ada):
    depth, d, n = w_ada.shape
    rows = c_all.shape[0]
    return pl.pallas_call(
        _ada_kernel,
        grid=(depth, n // ADA_COLS),
        in_specs=[pl.BlockSpec((rows, d), lambda l, j: (0, 0)),
                  pl.BlockSpec((None, d, ADA_COLS), lambda l, j: (l, 0, j)),
                  pl.BlockSpec((None, 1, ADA_COLS), lambda l, j: (l, 0, j))],
        out_specs=pl.BlockSpec((None, rows, ADA_COLS), lambda l, j: (l, 0, j)),
        out_shape=jax.ShapeDtypeStruct((depth, rows, n), F32),
        compiler_params=_params(("arbitrary", "arbitrary")),
    )(c_all, w_ada, b_ada.reshape(depth, 1, n))


def _ffn_kernel(*refs, with_mix, lam0):
    if with_mix:
        (x_ref, p_ref, g2_ref, o_ref, subg_ref, wbr_ref, wmix_ref, gate1_ref, gpost1_ref, *rest) = refs
    else:
        x_ref, *rest = refs
    shift_ref, scale_ref, gate_ref, gpre_ref, gpost_ref, win_ref, wout_ref, out_ref = rest
    x = x_ref[...]
    if with_mix:
        o = o_ref[...]
        subg = subg_ref[...]
        heads = []
        for h in range(ATT_HEADS):
            oh = o[:, h * ATT_V_DIM:(h + 1) * ATT_V_DIM]
            heads.append(_rms(oh, subg, SUBLN_EPS) * (1.0 - lam0))
        c_out = _dot(jnp.concatenate(heads, axis=1).astype(BF16), wbr_ref[...])
        merged = (p_ref[...] + g2_ref[...] * c_out).astype(BF16)
        x = x + gate1_ref[...] * _rms(_dot(merged, wmix_ref[...]), gpost1_ref[...])
    h = (_rms(x, gpre_ref[...]) * (1.0 + scale_ref[...]) + shift_ref[...]).astype(BF16)
    fc = D_FF // FFN_CHUNKS
    f = None
    for c in range(FFN_CHUNKS):
        up = _dot(h, win_ref[:, c * fc:(c + 1) * fc])
        gt = _dot(h, win_ref[:, D_FF + c * fc:D_FF + (c + 1) * fc])
        part = _dot((_silu(gt) * up).astype(BF16), wout_ref[c * fc:(c + 1) * fc, :])
        f = part if f is None else f + part
    out_ref[...] = x + 0.5 * gate_ref[...] * _rms(f, gpost_ref[...])


def _ffn_call(x, mod, gpre, gpost, w_in, w_out, mix=None, lam0=0.0):
    nseq, L, D = x.shape
    bm = min(FFN_ROWS, L)
    row_spec = pl.BlockSpec((None, bm, D), lambda s, i: (s, i, 0))

    def mod_spec(m):
        if m.shape[1] == 1:
            return pl.BlockSpec((None, 1, D), lambda s, i: (s, 0, 0))
        return row_spec

    args, specs = [x], [row_spec]
    if mix is not None:
        args += [mix["p"], mix["g2"], mix["o"], mix["subg"], mix["w_br_attn"], mix["w_mix_out"], mix["gate"], mix["gpost"]]
        specs += [row_spec, row_spec, pl.BlockSpec((None, bm, ATT_WIDTH), lambda s, i: (s, i, 0)),
                  _resident((1, ATT_V_DIM)), _resident((ATT_WIDTH, D)), _resident((D, D)),
                  mod_spec(mix["gate"]), _resident((1, D))]
    args += [mod[0], mod[1], mod[2], gpre, gpost, w_in, w_out]
    specs += [mod_spec(mod[0]), mod_spec(mod[1]), mod_spec(mod[2]), _resident((1, D)), _resident((1, D)),
              _resident(w_in.shape), _resident(w_out.shape)]
    return pl.pallas_call(
        functools.partial(_ffn_kernel, with_mix=mix is not None, lam0=lam0),
        grid=(nseq, L // bm),
        in_specs=specs,
        out_specs=row_spec,
        out_shape=jax.ShapeDtypeStruct(x.shape, F32),
        compiler_params=_params(("arbitrary", "arbitrary")),
    )(*args)


def _dwconv(src_ref, first, w, b, rows, rc):
    width, ch = w.shape
    outs = []
    for r0 in range(0, rows, rc):
        acc = jnp.broadcast_to(b, (rc, ch))
        for j in range(width):
            acc = acc + w[j:j + 1, :] * src_ref[first + j + r0:first + j + r0 + rc, :]
        outs.append(acc)
    return jnp.concatenate(outs, axis=0)


def _mix_kernel(x_ref, shift_ref, scale_ref, gpre_ref,
                wglu_ref, wz_ref, wxbc_ref, wdtr_ref, wdtt_ref, wqkv_ref, wgate_ref, bgate_ref,
                cw_ref, cb_ref, lng_ref, lnb_ref, wbrc_ref,
                sw_ref, sb_ref, dtbr_ref, dtbc_ref, alr_ref, alc_ref, dr_ref, ng_ref, wbrs_ref,
                conv0_ref, sconv0_ref, ssd0_ref,
                p_ref, g2_ref, k_ref, v_ref, qb_ref, kb_ref, vb_ref, convn_ref, sconvn_ref, ssdn_ref,
                conv_s, sconv_s, state_s, *, bm, q):
    @pl.when(pl.program_id(1) == 0)
    def _():
        conv_s[0:CONV_PAD, :] = conv0_ref[...]
        sconv_s[0:SCONV_PAD, :] = sconv0_ref[...]
        state_s[...] = ssd0_ref[...]

    h = (_rms(x_ref[...], gpre_ref[...]) * (1.0 + scale_ref[...]) + shift_ref[...]).astype(BF16)

    glu = _dot(h, wglu_ref[...])
    conv_s[CONV_PAD:CONV_PAD + bm, :] = glu[:, :CONV_CH] * jax.nn.sigmoid(glu[:, CONV_CH:])
    a = _dwconv(conv_s, CONV_PAD - (CONV_WIDTH - 1), cw_ref[...], cb_ref[...], bm, 32)
    mu = jnp.mean(a, axis=-1, keepdims=True)
    ac = a - mu
    a = ac * lax.rsqrt(jnp.mean(ac * ac, axis=-1, keepdims=True) + CONV_LN_EPS) * lng_ref[...] + lnb_ref[...]
    a_out = _dot(_silu(a).astype(BF16), wbrc_ref[...])
    convn_ref[...] = conv_s[CONV_PAD + bm - (CONV_WIDTH - 1):CONV_PAD + bm, :]
    conv_s[0:CONV_PAD, :] = conv_s[bm:bm + CONV_PAD, :]

    sconv_s[SCONV_PAD:SCONV_PAD + bm, :] = _dot(h, wxbc_ref[...])
    xbc = _silu(_dwconv(sconv_s, SCONV_PAD - (SSD_CONV_WIDTH - 1), sw_ref[...], sb_ref[...], bm, 16))
    sconvn_ref[...] = sconv_s[SCONV_PAD + bm - (SSD_CONV_WIDTH - 1):SCONV_PAD + bm, :]
    sconv_s[0:SCONV_PAD, :] = sconv_s[bm:bm + SCONV_PAD, :]

    dt_e = jax.nn.softplus(_dot(h, wdtr_ref[...]) + dtbr_ref[...])
    dt_t = jax.nn.softplus(_dot_nt(wdtt_ref[...], h) + dtbc_ref[...])
    a_rep = -jnp.exp(alr_ref[...])
    a_col = -jnp.exp(alc_ref[...])
    row = lax.broadcasted_iota(jnp.int32, (q, q), 0)
    col = lax.broadcasted_iota(jnp.int32, (q, q), 1)
    causal = row >= col
    tril = causal.astype(BF16)
    triu = (row <= col).astype(BF16)
    gw = SSD_INNER // SSD_GROUPS
    ys = []
    for c in range(bm // q):
        rows = slice(c * q, (c + 1) * q)
        dt_c = dt_e[rows]
        hi, mid, lo = _split3(dt_c * a_rep)
        acum = _dot(tril, hi) + _dot(tril, mid) + _dot(tril, lo)
        hi, mid, lo = _split3(dt_t[:, rows] * a_col)
        acum_t = _dot(hi, triu) + _dot(mid, triu) + _dot(lo, triu)
        xs_c = xbc[rows, :SSD_INNER]
        xdt = xs_c * dt_c
        a_last = acum[q - 1:q, :]
        x_end = (xdt * jnp.exp(a_last - acum)).astype(BF16)
        e_in = jnp.exp(acum)
        xdt_b = xdt.astype(BF16)
        state = state_s[...]
        state_b = state.astype(BF16)
        y_parts = []
        for g in range(SSD_GROUPS):
            bg = xbc[rows, SSD_INNER + g * SSD_STATE:SSD_INNER + (g + 1) * SSD_STATE]
            cg = xbc[rows, SSD_INNER + (SSD_GROUPS + g) * SSD_STATE:SSD_INNER + (SSD_GROUPS + g + 1) * SSD_STATE]
            bg_b, cg_b = bg.astype(BF16), cg.astype(BF16)
            cb = _dot_nt(cg_b, bg_b)
            y_off = _dot(cg_b, state_b[:, g * gw:(g + 1) * gw]) * e_in[:, g * gw:(g + 1) * gw]
            y_diag = []
            for r in range(SSD_HEADS // SSD_GROUPS):
                hd = g * (SSD_HEADS // SSD_GROUPS) + r
                lo_l = hd * SSD_HEAD_DIM
                seg = acum[:, lo_l:lo_l + q] - acum_t[hd:hd + 1, :]
                m = (cb * jnp.exp(jnp.where(causal, seg, NEG_BIG))).astype(BF16)
                y_diag.append(_dot(m, xdt_b[:, lo_l:lo_l + SSD_HEAD_DIM]))
            y_parts.append(jnp.concatenate(y_diag, axis=1) + y_off)
            state_s[:, g * gw:(g + 1) * gw] = (state[:, g * gw:(g + 1) * gw] * jnp.exp(a_last[:, g * gw:(g + 1) * gw])
                                               + _dot_tn(bg_b, x_end[:, g * gw:(g + 1) * gw]))
        ys.append(jnp.concatenate(y_parts, axis=1) + dr_ref[...] * xs_c)
    ssdn_ref[...] = state_s[...]
    y = jnp.concatenate(ys, axis=0) if len(ys) > 1 else ys[0]
    yz = y * _silu(_dot(h, wz_ref[...]))
    yz = jnp.concatenate([_rms(yz[:, g * gw:(g + 1) * gw], 1.0) for g in range(SSD_GROUPS)], axis=1) * ng_ref[...]
    b_out = _dot(yz.astype(BF16), wbrs_ref[...])

    gates = jax.nn.sigmoid(_dot(h, wgate_ref[...]) + bgate_ref[...])
    p_ref[...] = gates[:, :D_MODEL] * a_out + gates[:, D_MODEL:2 * D_MODEL] * b_out
    g2_ref[...] = gates[:, 2 * D_MODEL:]

    qkv = _dot(h, wqkv_ref[...])
    k = qkv[:, ATT_WIDTH:2 * ATT_WIDTH]
    v = qkv[:, 2 * ATT_WIDTH:]
    k_ref[...] = k
    v_ref[...] = v
    qb_ref[...] = (qkv[:, :ATT_WIDTH] * ATT_SCALE).astype(BF16)
    kb_ref[...] = k.astype(BF16)
    vb_ref[...] = v.astype(BF16)


def _mix_call(x, shift, scale, gpre, w, conv0, sconv0, ssd0):
    nseq, L, D = x.shape
    bm = min(MIX_ROWS, L)
    q = min(CHUNK, bm)
    rows = lambda n: pl.BlockSpec((None, bm, n), lambda s, i: (s, i, 0))
    per_seq = lambda a: pl.BlockSpec((None,) + a.shape[1:], lambda s, i: (s, 0, 0))
    weights = [gpre, w["w_glu"], w["w_z"], w["w_xbc"], w["w_dt_rep"], w["w_dt_t"], w["w_qkv"], w["w_gate"], w["b_gate"],
               w["conv_w"], w["conv_b"], w["ln_g"], w["ln_b"], w["w_br_conv"],
               w["sconv_w"], w["sconv_b"], w["dtb_rep"], w["dtb_col"], w["alog_rep"], w["alog_col"], w["d_rep"],
               w["norm_g"], w["w_br_ssd"]]
    out_shape = [jax.ShapeDtypeStruct((nseq, L, D), F32), jax.ShapeDtypeStruct((nseq, L, D), F32),
                 jax.ShapeDtypeStruct((nseq, L, ATT_WIDTH), F32), jax.ShapeDtypeStruct((nseq, L, ATT_WIDTH), F32),
                 jax.ShapeDtypeStruct((nseq, L, ATT_WIDTH), BF16), jax.ShapeDtypeStruct((nseq, L, ATT_WIDTH), BF16),
                 jax.ShapeDtypeStruct((nseq, L, ATT_WIDTH), BF16),
                 jax.ShapeDtypeStruct((nseq, CONV_WIDTH - 1, CONV_CH), F32),
                 jax.ShapeDtypeStruct((nseq, SSD_CONV_WIDTH - 1, SSD_CONV_CH), F32),
                 jax.ShapeDtypeStruct((nseq, SSD_STATE, SSD_INNER), F32)]
    out_specs = [rows(D), rows(D), rows(ATT_WIDTH), rows(ATT_WIDTH), rows(ATT_WIDTH), rows(ATT_WIDTH), rows(ATT_WIDTH),
                 per_seq(out_shape[7]), per_seq(out_shape[8]), per_seq(out_shape[9])]
    return pl.pallas_call(
        functools.partial(_mix_kernel, bm=bm, q=q),
        grid=(nseq, L // bm),
        in_specs=[rows(D), per_seq(shift), per_seq(scale)] + [_resident(a.shape) for a in weights]
                 + [per_seq(conv0), per_seq(sconv0), per_seq(ssd0)],
        out_specs=out_specs,
        out_shape=out_shape,
        scratch_shapes=[pltpu.VMEM((CONV_PAD + bm, CONV_CH), F32), pltpu.VMEM((SCONV_PAD + bm, SSD_CONV_CH), F32),
                        pltpu.VMEM((SSD_STATE, SSD_INNER), F32)],
        compiler_params=_params(("arbitrary", "arbitrary")),
    )(x, shift, scale, *weights, conv0, sconv0, ssd0)


def _lambda(lq_ref, lk_ref, lam0):
    lq, lk = lq_ref[...], lk_ref[...]
    e0 = jnp.exp(jnp.sum(lq[0:1] * lk[0:1], axis=-1, keepdims=True))
    e1 = jnp.exp(jnp.sum(lq[1:2] * lk[1:2], axis=-1, keepdims=True))
    return e0 - e1 + lam0


def _attn_prompt_kernel(q_ref, k_ref, vt_ref, ext_ref, corr_ref, lq_ref, lk_ref, o_ref, acc_s, *, lam0):
    blk = ATT_BLOCK
    qi = pl.program_id(1)
    lam = _lambda(lq_ref, lk_ref, lam0)
    lane = lax.broadcasted_iota(jnp.int32, (blk, LANES), 1)
    first = lane < ATT_QK_DIM
    one_at = lambda c: jnp.where(lane == c, 1.0, 0.0).astype(BF16)
    for h in range(ATT_HEADS):
        cols = slice(h * ATT_V_DIM, (h + 1) * ATT_V_DIM)
        qh = q_ref[:, cols]
        qa = (jnp.where(first, qh, one_at(ATT_QK_DIM)), jnp.where(first, one_at(0), qh))
        slope = ALIBI_SLOPES[h]

        def tile(kj, carry, diagonal):
            k0 = pl.multiple_of(kj * blk, blk)
            kh = k_ref[pl.ds(k0, blk), cols]
            ka = (jnp.where(first, kh, ext_ref[h, 0]), jnp.where(first, ext_ref[h, 1], kh))
            vt = vt_ref[cols, pl.ds(k0, blk)]
            base = slope * k0.astype(F32)
            out = []
            for c in range(2):
                m_old, l_old = carry[2 * c], carry[2 * c + 1]
                s = _dot_nt(ka[c], qa[c])
                if diagonal:
                    s = s + corr_ref[h]
                m_new = jnp.maximum(m_old, jnp.max(s, axis=0, keepdims=True) + base)
                alpha = jnp.exp(m_old - m_new)
                p = jnp.exp(s - (m_new - base))
                out += [m_new, alpha * l_old + jnp.sum(p, axis=0, keepdims=True)]
                acc_s[c] = alpha * acc_s[c] + _dot(vt, p.astype(BF16))
            return tuple(out)

        acc_s[...] = jnp.zeros_like(acc_s)
        init = (jnp.full((1, blk), NEG_BIG, F32), jnp.zeros((1, blk), F32)) * 2
        carry = lax.fori_loop(0, qi, functools.partial(tile, diagonal=False), init)
        _, l1, _, l2 = tile(qi, carry, True)
        o_t = acc_s[0] / l1 - lam * (acc_s[1] / l2)
        o_ref[:, cols] = o_t.T


def _attn_tables():
    blk = ATT_BLOCK
    pos = np.arange(blk, dtype=np.float64)
    ext = np.zeros((ATT_HEADS, 2, blk, LANES), np.float32)
    corr = np.zeros((ATT_HEADS, blk, blk), np.float32)
    kpos, qpos = pos[:, None], pos[None, :]
    visible = (kpos // CHUNK) <= (qpos // CHUNK)
    for h, slope in enumerate(ALIBI_SLOPES):
        ext[h, 0, :, ATT_QK_DIM] = slope * pos
        ext[h, 1, :, 0] = slope * pos
        corr[h] = np.where(visible, -2.0 * slope * np.maximum(kpos - qpos, 0.0), NEG_BIG)
    return jnp.asarray(ext, BF16), jnp.asarray(corr, F32)


def _attn_prompt_call(qb, kb, vb, lq, lk, lam0):
    b, S, W = qb.shape
    blk = ATT_BLOCK
    vt = jnp.swapaxes(vb, 1, 2)
    ext, corr = _attn_tables()
    whole = lambda shape: pl.BlockSpec((None,) + shape, lambda s, i: (s, 0, 0), pipeline_mode=pl.Buffered(1))
    return pl.pallas_call(
        functools.partial(_attn_prompt_kernel, lam0=lam0),
        grid=(b, S // blk),
        in_specs=[pl.BlockSpec((None, blk, W), lambda s, i: (s, i, 0)), whole((S, W)), whole((W, S)),
                  _resident(ext.shape), _resident(corr.shape), _resident(lq.shape), _resident(lk.shape)],
        out_specs=pl.BlockSpec((None, blk, W), lambda s, i: (s, i, 0)),
        out_shape=jax.ShapeDtypeStruct((b, S, W), F32),
        scratch_shapes=[pltpu.VMEM((2, ATT_V_DIM, blk), F32)],
        compiler_params=_params(("arbitrary", "arbitrary")),
    )(qb, kb, vt, ext, corr, lq, lk)


def _attn_sample_kernel(q_ref, kn_ref, vn_ref, kc_ref, vc_ref, lq_ref, lk_ref, o_ref, *, lam0):
    ls, past = q_ref.shape[0], kc_ref.shape[0]
    lam = _lambda(lq_ref, lk_ref, lam0)
    lane = lax.broadcasted_iota(jnp.int32, (ls, LANES), 1)
    first = lane < ATT_QK_DIM

    def bias_mask(nk, k_first):
        qpos = past + lax.broadcasted_iota(jnp.int32, (ls, nk), 0)
        kpos = k_first + lax.broadcasted_iota(jnp.int32, (ls, nk), 1)
        dist = jnp.abs(qpos - kpos).astype(F32)
        visible = jnp.right_shift(kpos, CHUNK_SHIFT) <= jnp.right_shift(qpos, CHUNK_SHIFT)
        return dist, visible

    dist_c, vis_c = bias_mask(past, 0)
    dist_n, vis_n = bias_mask(ls, past)
    outs = []
    for h in range(ATT_HEADS):
        cols = slice(h * ATT_V_DIM, (h + 1) * ATT_V_DIM)
        qh = q_ref[:, cols]
        kc, kn = kc_ref[:, cols].astype(BF16), kn_ref[:, cols]
        vc, vn = vc_ref[:, cols].astype(BF16), vn_ref[:, cols]
        o = []
        for c in range(2):
            qm = jnp.where(first == (c == 0), qh, jnp.zeros_like(qh))
            sc = jnp.where(vis_c, _dot_nt(qm, kc) - ALIBI_SLOPES[h] * dist_c, NEG_BIG)
            sn = jnp.where(vis_n, _dot_nt(qm, kn) - ALIBI_SLOPES[h] * dist_n, NEG_BIG)
            m = jnp.maximum(jnp.max(sc, axis=-1, keepdims=True), jnp.max(sn, axis=-1, keepdims=True))
            pc, pn = jnp.exp(sc - m), jnp.exp(sn - m)
            l = jnp.sum(pc, axis=-1, keepdims=True) + jnp.sum(pn, axis=-1, keepdims=True)
            o.append((_dot(pc.astype(BF16), vc) + _dot(pn.astype(BF16), vn)) / l)
        outs.append(o[0] - lam * o[1])
    o_ref[...] = jnp.concatenate(outs, axis=1)


def _attn_sample_call(qb, kb, vb, cache_k, cache_v, lq, lk, lam0):
    b, ls, W = qb.shape
    past = cache_k.shape[1]
    new = pl.BlockSpec((None, ls, W), lambda s: (s, 0, 0))
    old = pl.BlockSpec((None, past, W), lambda s: (s, 0, 0))
    return pl.pallas_call(
        functools.partial(_attn_sample_kernel, lam0=lam0),
        grid=(b,),
        in_specs=[new, new, new, old, old, _resident(lq.shape), _resident(lk.shape)],
        out_specs=new,
        out_shape=jax.ShapeDtypeStruct((b, ls, W), F32),
        compiler_params=_params(("arbitrary",)),
    )(qb, kb, vb, cache_k, cache_v, lq, lk)


def _prep_layer(l, w_in, b_gate, conv_dw_w, conv_dw_b, conv_ln_g, conv_ln_b, w_br_conv, ssd_conv_w, ssd_conv_b,
                ssd_dt_bias, ssd_A_log, ssd_D, ssd_norm_g, w_br_ssd):
    splits = (2 * CONV_CH, SSD_INNER, SSD_CONV_CH, SSD_HEADS, 3 * ATT_WIDTH, N_BRANCH * D_MODEL)
    pts = np.cumsum(splits)[:-1].tolist()
    w_glu, w_z, w_xbc, w_dt, w_qkv, w_gate = jnp.split(w_in[l].astype(BF16), pts, axis=1)
    rep = lambda a: jnp.repeat(a, SSD_HEAD_DIM, axis=-1)
    return dict(
        w_glu=w_glu, w_z=w_z, w_xbc=w_xbc, w_dt_rep=rep(w_dt), w_dt_t=w_dt.T, w_qkv=w_qkv, w_gate=w_gate,
        b_gate=b_gate[l].reshape(1, N_BRANCH * D_MODEL),
        conv_w=conv_dw_w[l], conv_b=conv_dw_b[l][None], ln_g=conv_ln_g[l][None], ln_b=conv_ln_b[l][None],
        w_br_conv=w_br_conv[l].astype(BF16),
        sconv_w=ssd_conv_w[l], sconv_b=ssd_conv_b[l][None],
        dtb_rep=rep(ssd_dt_bias[l])[None], dtb_col=ssd_dt_bias[l][:, None],
        alog_rep=rep(ssd_A_log[l])[None], alog_col=ssd_A_log[l][:, None],
        d_rep=rep(ssd_D[l])[None], norm_g=ssd_norm_g[l][None], w_br_ssd=w_br_ssd[l].astype(BF16))


def _state_in(conv, sconv, ssd):
    b = conv.shape[0]
    conv = jnp.pad(conv, ((0, 0), (CONV_PAD - (CONV_WIDTH - 1), 0), (0, 0)))
    sconv = jnp.pad(sconv, ((0, 0), (SCONV_PAD - (SSD_CONV_WIDTH - 1), 0), (0, 0)))
    ssd = jnp.transpose(ssd, (0, 3, 1, 2)).reshape(b, SSD_STATE, SSD_INNER)
    return conv, sconv, ssd


def _ssd_state_out(s):
    b = s.shape[0]
    return jnp.transpose(s.reshape(b, SSD_STATE, SSD_HEADS, SSD_HEAD_DIM), (0, 2, 3, 1))


def kernel(x_prompt, x_sample, cache_attn_k, cache_attn_v, state_conv, state_ssd_conv, state_ssd, c_prompt, c_sample,
           w_ada, b_ada, norm_pre, norm_post, w_ffn_in, w_ffn_out, w_in, b_gate, conv_dw_w, conv_dw_b, conv_ln_g,
           conv_ln_b, w_br_conv, ssd_conv_w, ssd_conv_b, ssd_dt_bias, ssd_A_log, ssd_D, ssd_norm_g, w_br_ssd,
           lambda_q, lambda_k, attn_subln_g, w_br_attn, w_mix_out):
    depth = w_ada.shape[0]
    bp, S, D = x_prompt.shape
    bs, ls, _ = x_sample.shape
    past = cache_attn_k.shape[2]

    mods = _ada_call(jnp.concatenate([c_prompt, c_sample], axis=0), w_ada, b_ada)
    mods = mods.reshape(depth, bp + bs, N_SUB, 3, D)

    zeros = _state_in(jnp.zeros((bp, CONV_WIDTH - 1, CONV_CH), F32), jnp.zeros((bp, SSD_CONV_WIDTH - 1, SSD_CONV_CH), F32),
                      jnp.zeros((bp, SSD_HEADS, SSD_HEAD_DIM, SSD_STATE), F32))
    xp = x_prompt
    xs = x_sample.reshape(1, bs * ls, D)
    st_p, st_s = [], []
    for l in range(depth):
        lam0 = _lambda_init(l)
        w = _prep_layer(l, w_in, b_gate, conv_dw_w, conv_dw_b, conv_ln_g, conv_ln_b, w_br_conv, ssd_conv_w, ssd_conv_b,
                        ssd_dt_bias, ssd_A_log, ssd_D, ssd_norm_g, w_br_ssd)
        wf_in, wf_out = w_ffn_in[l].astype(BF16), w_ffn_out[l].astype(BF16)
        w_bra, w_mix = w_br_attn[l].astype(BF16), w_mix_out[l].astype(BF16)
        gpre = lambda s: norm_pre[l, s][None]
        gpost = lambda s: norm_post[l, s][None]
        subg = attn_subln_g[l][None]
        mp = lambda s, k: mods[l, :bp, s, k][:, None, :]
        ms_seq = lambda s, k: mods[l, bp:, s, k][:, None, :]
        ms_tok = lambda s, k: jnp.repeat(mods[l, bp:, s, k], ls, axis=0)[None]

        xp = _ffn_call(xp, (mp(0, 0), mp(0, 1), mp(0, 2)), gpre(0), gpost(0), wf_in[0], wf_out[0])
        p, g2, k, v, qb, kb, vb, convn, sconvn, ssdn = _mix_call(xp, mp(1, 0), mp(1, 1), gpre(1), w, *zeros)
        o = _attn_prompt_call(qb, kb, vb, lambda_q[l], lambda_k[l], lam0)
        mix = dict(p=p, g2=g2, o=o, subg=subg, w_br_attn=w_bra, w_mix_out=w_mix, gate=mp(1, 2), gpost=gpost(1))
        xp = _ffn_call(xp, (mp(2, 0), mp(2, 1), mp(2, 2)), gpre(2), gpost(2), wf_in[1], wf_out[1], mix=mix, lam0=lam0)
        st_p.append((k.reshape(bp, S, ATT_HEADS, ATT_V_DIM), v.reshape(bp, S, ATT_HEADS, ATT_V_DIM), convn, sconvn,
                     _ssd_state_out(ssdn)))

        xs = _ffn_call(xs, (ms_tok(0, 0), ms_tok(0, 1), ms_tok(0, 2)), gpre(0), gpost(0), wf_in[0], wf_out[0])
        states = _state_in(state_conv[l], state_ssd_conv[l], state_ssd[l])
        p, g2, k, v, qb, kb, vb, convn, sconvn, ssdn = _mix_call(xs.reshape(bs, ls, D), ms_seq(1, 0), ms_seq(1, 1),
                                                                 gpre(1), w, *states)
        o = _attn_sample_call(qb, kb, vb, cache_attn_k[l].reshape(bs, past, ATT_WIDTH),
                              cache_attn_v[l].reshape(bs, past, ATT_WIDTH), lambda_q[l], lambda_k[l], lam0)
        flat = lambda t: t.reshape(1, bs * ls, t.shape[-1])
        mix = dict(p=flat(p), g2=flat(g2), o=flat(o), subg=subg, w_br_attn=w_bra, w_mix_out=w_mix, gate=ms_tok(1, 2),
                   gpost=gpost(1))
        xs = _ffn_call(xs, (ms_tok(2, 0), ms_tok(2, 1), ms_tok(2, 2)), gpre(2), gpost(2), wf_in[1], wf_out[1], mix=mix,
                       lam0=lam0)
        st_s.append((k.reshape(bs, ls, ATT_HEADS, ATT_V_DIM), v.reshape(bs, ls, ATT_HEADS, ATT_V_DIM), convn, sconvn,
                     _ssd_state_out(ssdn)))

    stack = lambda st, i: jnp.stack([s[i] for s in st])
    return (xp, xs.reshape(bs, ls, D),
            stack(st_p, 0), stack(st_p, 1), stack(st_p, 2), stack(st_p, 3), stack(st_p, 4),
            stack(st_s, 0), stack(st_s, 1), stack(st_s, 2), stack(st_s, 3), stack(st_s, 4))
```

```python
import functools
import math

import ml_dtypes
import numpy as np
import jax
import jax.numpy as jnp
from jax import lax
from jax.experimental import pallas as pl
from jax.experimental.pallas import tpu as pltpu

F32, BF16 = jnp.float32, jnp.bfloat16

D_MODEL = 1024
D_FF = 2816
CONV_CH = 512
CONV_WIDTH = 31
SSD_HEADS = 16
SSD_HEAD_DIM = 64
SSD_INNER = SSD_HEADS * SSD_HEAD_DIM
SSD_GROUPS = 2
SSD_STATE = 128
SSD_CONV_WIDTH = 4
SSD_CONV_CH = SSD_INNER + 2 * SSD_GROUPS * SSD_STATE
CHUNK = 64
CHUNK_SHIFT = 6
ATT_HEADS = 4
ATT_QK_DIM = 64
ATT_V_DIM = 2 * ATT_QK_DIM
ATT_WIDTH = ATT_HEADS * ATT_V_DIM
ATT_SCALE = ATT_QK_DIM ** -0.5
ALIBI_SLOPES = tuple(2.0 ** (-8.0 * (h + 1) / ATT_HEADS) for h in range(ATT_HEADS))
N_BRANCH = 3
N_SUB = 3
EPS = 1e-6
SUBLN_EPS = 1e-5
CONV_LN_EPS = 1e-5
NEG_BIG = -1e30
LOG2E = math.log2(math.e)
BIAS_TERMS = 3

VMEM_LIMIT_BYTES = 58 * 1024 * 1024
LANES = 128
SUBLANES = 8

FFN_ROWS = 512
FFN_CHUNKS = 2
MIX_ROWS = 256
ATT_BLOCK = 256
ADA_COLS = 1152
CONV_PAD = 32
SCONV_PAD = 8


def _lambda_init(layer):
    return 0.8 - 0.6 * math.exp(-0.3 * layer)


def _params(sem):
    return pltpu.CompilerParams(dimension_semantics=sem, vmem_limit_bytes=VMEM_LIMIT_BYTES)


def _resident(shape):
    nd = len(shape)
    return pl.BlockSpec(shape, lambda *_: (0,) * nd, pipeline_mode=pl.Buffered(1))


def _dot(a, b):
    return jnp.dot(a, b, preferred_element_type=F32)


def _dot_nt(a, b):
    return lax.dot_general(a, b, (((1,), (1,)), ((), ())), preferred_element_type=F32)


def _dot_tn(a, b):
    return lax.dot_general(a, b, (((0,), (0,)), ((), ())), preferred_element_type=F32)


def _rms(x, g, eps=EPS):
    return x * lax.rsqrt(jnp.mean(x * x, axis=-1, keepdims=True) + eps) * g


def _silu(x):
    return x * jax.nn.sigmoid(x)


def _split3(a):
    hi = a.astype(BF16)
    r = a - hi.astype(F32)
    mid = r.astype(BF16)
    lo = (r - mid.astype(F32)).astype(BF16)
    return hi, mid, lo


def _ada_kernel(c_ref, w_ref, b_ref, o_ref):
    sc = _silu(c_ref[...]).astype(BF16)
    o_ref[...] = _dot(sc, w_ref[...].astype(BF16)) + b_ref[...]


def _ada_call(c_all, w_ada, b_ada):
    depth, d, n = w_ada.shape
    rows = c_all.shape[0]
    return pl.pallas_call(
        _ada_kernel,
        grid=(depth, n // ADA_COLS),
        in_specs=[pl.BlockSpec((rows, d), lambda l, j: (0, 0)),
                  pl.BlockSpec((None, d, ADA_COLS), lambda l, j: (l, 0, j)),
                  pl.BlockSpec((None, 1, ADA_COLS), lambda l, j: (l, 0, j))],
        out_specs=pl.BlockSpec((None, rows, ADA_COLS), lambda l, j: (l, 0, j)),
        out_shape=jax.ShapeDtypeStruct((depth, rows, n), F32),
        compiler_params=_params(("arbitrary", "arbitrary")),
        name="adaln",
    )(c_all, w_ada, b_ada.reshape(depth, 1, n))


def _ffn_kernel(*refs, with_mix, lam0):
    if with_mix:
        (x_ref, p_ref, g2_ref, o_ref, subg_ref, wbr_ref, wmix_ref, gate1_ref, gpost1_ref, *rest) = refs
    else:
        x_ref, *rest = refs
    shift_ref, scale_ref, gate_ref, gpre_ref, gpost_ref, win_ref, wout_ref, out_ref = rest
    x = x_ref[...]
    if with_mix:
        o = o_ref[...]
        subg = subg_ref[...]
        heads = []
        for h in range(ATT_HEADS):
            oh = o[:, h * ATT_V_DIM:(h + 1) * ATT_V_DIM]
            heads.append(_rms(oh, subg, SUBLN_EPS) * (1.0 - lam0))
        c_out = _dot(jnp.concatenate(heads, axis=1).astype(BF16), wbr_ref[...])
        merged = (p_ref[...] + g2_ref[...] * c_out).astype(BF16)
        x = x + gate1_ref[...] * _rms(_dot(merged, wmix_ref[...]), gpost1_ref[...])
    h = (_rms(x, gpre_ref[...]) * (1.0 + scale_ref[...]) + shift_ref[...]).astype(BF16)
    fc = D_FF // FFN_CHUNKS
    f = None
    for c in range(FFN_CHUNKS):
        up = _dot(h, win_ref[:, c * fc:(c + 1) * fc])
        gt = _dot(h, win_ref[:, D_FF + c * fc:D_FF + (c + 1) * fc])
        part = _dot((_silu(gt) * up).astype(BF16), wout_ref[c * fc:(c + 1) * fc, :])
        f = part if f is None else f + part
    out_ref[...] = x + 0.5 * gate_ref[...] * _rms(f, gpost_ref[...])


def _ffn_call(x, mod, gpre, gpost, w_in, w_out, mix=None, lam0=0.0):
    nseq, L, D = x.shape
    bm = min(FFN_ROWS, L)
    row_spec = pl.BlockSpec((None, bm, D), lambda s, i: (s, i, 0))

    def mod_spec(m):
        if m.shape[1] == 1:
            return pl.BlockSpec((None, 1, D), lambda s, i: (s, 0, 0))
        return row_spec

    args, specs = [x], [row_spec]
    if mix is not None:
        args += [mix["p"], mix["g2"], mix["o"], mix["subg"], mix["w_br_attn"], mix["w_mix_out"], mix["gate"], mix["gpost"]]
        specs += [row_spec, row_spec, pl.BlockSpec((None, bm, ATT_WIDTH), lambda s, i: (s, i, 0)),
                  _resident((1, ATT_V_DIM)), _resident((ATT_WIDTH, D)), _resident((D, D)),
                  mod_spec(mix["gate"]), _resident((1, D))]
    args += [mod[0], mod[1], mod[2], gpre, gpost, w_in, w_out]
    specs += [mod_spec(mod[0]), mod_spec(mod[1]), mod_spec(mod[2]), _resident((1, D)), _resident((1, D)),
              _resident(w_in.shape), _resident(w_out.shape)]
    return pl.pallas_call(
        functools.partial(_ffn_kernel, with_mix=mix is not None, lam0=lam0),
        grid=(nseq, L // bm),
        in_specs=specs,
        out_specs=row_spec,
        out_shape=jax.ShapeDtypeStruct(x.shape, F32),
        compiler_params=_params(("arbitrary", "arbitrary")),
        name="merge_ffn" if mix is not None else "ffn",
    )(*args)


def _dwconv(src_ref, first, w, b, rows, rc, shift_ref=None):
    width, ch = w.shape
    if shift_ref is not None:
        span = shift_ref.shape[1]
        for s in sorted({(first + j) % SUBLANES for j in range(width)} - {0}):
            shift_ref[s - 1] = src_ref[s:s + span, :]

    def tap(j, r0):
        a, s = divmod(first + j, SUBLANES)
        if shift_ref is None or s == 0:
            return src_ref[first + j + r0:first + j + r0 + rc, :]
        return shift_ref[s - 1, a * SUBLANES + r0:a * SUBLANES + r0 + rc, :]

    outs = []
    for r0 in range(0, rows, rc):
        acc = jnp.broadcast_to(b, (rc, ch))
        for j in range(width):
            acc = acc + w[j:j + 1, :] * tap(j, r0)
        outs.append(acc)
    return jnp.concatenate(outs, axis=0)


def _mix_kernel(x_ref, shift_ref, scale_ref, gpre_ref,
                wglu_ref, wz_ref, wxbc_ref, wdtr_ref, wdtt_ref, wqkv_ref, wgate_ref, bgate_ref,
                cw_ref, cb_ref, lng_ref, lnb_ref, wbrc_ref,
                sw_ref, sb_ref, dtbr_ref, dtbc_ref, alr_ref, alc_ref, dr_ref, ng_ref, wbrs_ref,
                conv0_ref, sconv0_ref, ssd0_ref,
                p_ref, g2_ref, k_ref, v_ref, qb_ref, kb_ref, vb_ref, convn_ref, sconvn_ref, ssdn_ref,
                conv_s, sconv_s, state_s, shift_s, *, bm, q):
    @pl.when(pl.program_id(1) == 0)
    def _():
        conv_s[0:CONV_PAD, :] = conv0_ref[...]
        sconv_s[0:SCONV_PAD, :] = sconv0_ref[...]
        state_s[...] = ssd0_ref[...]

    h = (_rms(x_ref[...], gpre_ref[...]) * (1.0 + scale_ref[...]) + shift_ref[...]).astype(BF16)

    glu = _dot(h, wglu_ref[...])
    conv_s[CONV_PAD:CONV_PAD + bm, :] = glu[:, :CONV_CH] * jax.nn.sigmoid(glu[:, CONV_CH:])
    a = _dwconv(conv_s, CONV_PAD - (CONV_WIDTH - 1), cw_ref[...], cb_ref[...], bm, 32, shift_s)
    mu = jnp.mean(a, axis=-1, keepdims=True)
    ac = a - mu
    a = ac * lax.rsqrt(jnp.mean(ac * ac, axis=-1, keepdims=True) + CONV_LN_EPS) * lng_ref[...] + lnb_ref[...]
    a_out = _dot(_silu(a).astype(BF16), wbrc_ref[...])
    convn_ref[...] = conv_s[CONV_PAD + bm - (CONV_WIDTH - 1):CONV_PAD + bm, :]
    conv_s[0:CONV_PAD, :] = conv_s[bm:bm + CONV_PAD, :]

    sconv_s[SCONV_PAD:SCONV_PAD + bm, :] = _dot(h, wxbc_ref[...])
    xbc = _silu(_dwconv(sconv_s, SCONV_PAD - (SSD_CONV_WIDTH - 1), sw_ref[...], sb_ref[...], bm, 16))
    sconvn_ref[...] = sconv_s[SCONV_PAD + bm - (SSD_CONV_WIDTH - 1):SCONV_PAD + bm, :]
    sconv_s[0:SCONV_PAD, :] = sconv_s[bm:bm + SCONV_PAD, :]

    dt_e = jax.nn.softplus(_dot(h, wdtr_ref[...]) + dtbr_ref[...])
    dt_t = jax.nn.softplus(_dot_nt(wdtt_ref[...], h) + dtbc_ref[...])
    a_rep = -jnp.exp(alr_ref[...])
    a_col = -jnp.exp(alc_ref[...])
    row = lax.broadcasted_iota(jnp.int32, (q, q), 0)
    col = lax.broadcasted_iota(jnp.int32, (q, q), 1)
    causal = row >= col
    tril = causal.astype(BF16)
    triu = (row <= col).astype(BF16)
    gw = SSD_INNER // SSD_GROUPS
    ys = []
    for c in range(bm // q):
        rows = slice(c * q, (c + 1) * q)
        dt_c = dt_e[rows]
        hi, mid, lo = _split3(dt_c * a_rep)
        acum = _dot(tril, hi) + _dot(tril, mid) + _dot(tril, lo)
        hi, mid, lo = _split3(dt_t[:, rows] * a_col)
        acum_t = _dot(hi, triu) + _dot(mid, triu) + _dot(lo, triu)
        xs_c = xbc[rows, :SSD_INNER]
        xdt = xs_c * dt_c
        a_last = acum[q - 1:q, :]
        x_end = (xdt * jnp.exp(a_last - acum)).astype(BF16)
        e_in = jnp.exp(acum)
        xdt_b = xdt.astype(BF16)
        state = state_s[...]
        state_b = state.astype(BF16)
        y_parts = []
        for g in range(SSD_GROUPS):
            bg = xbc[rows, SSD_INNER + g * SSD_STATE:SSD_INNER + (g + 1) * SSD_STATE]
            cg = xbc[rows, SSD_INNER + (SSD_GROUPS + g) * SSD_STATE:SSD_INNER + (SSD_GROUPS + g + 1) * SSD_STATE]
            bg_b, cg_b = bg.astype(BF16), cg.astype(BF16)
            cb = _dot_nt(cg_b, bg_b)
            y_off = _dot(cg_b, state_b[:, g * gw:(g + 1) * gw]) * e_in[:, g * gw:(g + 1) * gw]
            y_diag = []
            for r in range(SSD_HEADS // SSD_GROUPS):
                hd = g * (SSD_HEADS // SSD_GROUPS) + r
                lo_l = hd * SSD_HEAD_DIM
                seg = acum[:, lo_l:lo_l + q] - acum_t[hd:hd + 1, :]
                m = (cb * jnp.exp(jnp.where(causal, seg, NEG_BIG))).astype(BF16)
                y_diag.append(_dot(m, xdt_b[:, lo_l:lo_l + SSD_HEAD_DIM]))
            y_parts.append(jnp.concatenate(y_diag, axis=1) + y_off)
            state_s[:, g * gw:(g + 1) * gw] = (state[:, g * gw:(g + 1) * gw] * jnp.exp(a_last[:, g * gw:(g + 1) * gw])
                                               + _dot_tn(bg_b, x_end[:, g * gw:(g + 1) * gw]))
        ys.append(jnp.concatenate(y_parts, axis=1) + dr_ref[...] * xs_c)
    ssdn_ref[...] = state_s[...]
    y = jnp.concatenate(ys, axis=0) if len(ys) > 1 else ys[0]
    yz = y * _silu(_dot(h, wz_ref[...]))
    yz = jnp.concatenate([_rms(yz[:, g * gw:(g + 1) * gw], 1.0) for g in range(SSD_GROUPS)], axis=1) * ng_ref[...]
    b_out = _dot(yz.astype(BF16), wbrs_ref[...])

    gates = jax.nn.sigmoid(_dot(h, wgate_ref[...]) + bgate_ref[...])
    p_ref[...] = gates[:, :D_MODEL] * a_out + gates[:, D_MODEL:2 * D_MODEL] * b_out
    g2_ref[...] = gates[:, 2 * D_MODEL:]

    qkv = _dot(h, wqkv_ref[...])
    k = qkv[:, ATT_WIDTH:2 * ATT_WIDTH]
    v = qkv[:, 2 * ATT_WIDTH:]
    k_ref[...] = k
    v_ref[...] = v
    qb_ref[...] = (qkv[:, :ATT_WIDTH] * (ATT_SCALE * LOG2E)).astype(BF16)
    kb_ref[...] = k.astype(BF16)
    vb_ref[...] = v.astype(BF16)


def _mix_call(x, shift, scale, gpre, w, conv0, sconv0, ssd0):
    nseq, L, D = x.shape
    bm = min(MIX_ROWS, L)
    q = min(CHUNK, bm)
    rows = lambda n: pl.BlockSpec((None, bm, n), lambda s, i: (s, i, 0))
    per_seq = lambda a: pl.BlockSpec((None,) + a.shape[1:], lambda s, i: (s, 0, 0))
    weights = [gpre, w["w_glu"], w["w_z"], w["w_xbc"], w["w_dt_rep"], w["w_dt_t"], w["w_qkv"], w["w_gate"], w["b_gate"],
               w["conv_w"], w["conv_b"], w["ln_g"], w["ln_b"], w["w_br_conv"],
               w["sconv_w"], w["sconv_b"], w["dtb_rep"], w["dtb_col"], w["alog_rep"], w["alog_col"], w["d_rep"],
               w["norm_g"], w["w_br_ssd"]]
    out_shape = [jax.ShapeDtypeStruct((nseq, L, D), F32), jax.ShapeDtypeStruct((nseq, L, D), F32),
                 jax.ShapeDtypeStruct((nseq, L, ATT_WIDTH), F32), jax.ShapeDtypeStruct((nseq, L, ATT_WIDTH), F32),
                 jax.ShapeDtypeStruct((nseq, L, ATT_WIDTH), BF16), jax.ShapeDtypeStruct((nseq, L, ATT_WIDTH), BF16),
                 jax.ShapeDtypeStruct((nseq, L, ATT_WIDTH), BF16),
                 jax.ShapeDtypeStruct((nseq, CONV_WIDTH - 1, CONV_CH), F32),
                 jax.ShapeDtypeStruct((nseq, SSD_CONV_WIDTH - 1, SSD_CONV_CH), F32),
                 jax.ShapeDtypeStruct((nseq, SSD_STATE, SSD_INNER), F32)]
    out_specs = [rows(D), rows(D), rows(ATT_WIDTH), rows(ATT_WIDTH), rows(ATT_WIDTH), rows(ATT_WIDTH), rows(ATT_WIDTH),
                 per_seq(out_shape[7]), per_seq(out_shape[8]), per_seq(out_shape[9])]
    return pl.pallas_call(
        functools.partial(_mix_kernel, bm=bm, q=q),
        grid=(nseq, L // bm),
        in_specs=[rows(D), per_seq(shift), per_seq(scale)] + [_resident(a.shape) for a in weights]
                 + [per_seq(conv0), per_seq(sconv0), per_seq(ssd0)],
        out_specs=out_specs,
        out_shape=out_shape,
        scratch_shapes=[pltpu.VMEM((CONV_PAD + bm, CONV_CH), F32), pltpu.VMEM((SCONV_PAD + bm, SSD_CONV_CH), F32),
                        pltpu.VMEM((SSD_STATE, SSD_INNER), F32),
                        pltpu.VMEM((SUBLANES - 1, CONV_PAD + bm - SUBLANES, CONV_CH), F32)],
        compiler_params=_params(("arbitrary", "arbitrary")),
        name="mix_front",
    )(x, shift, scale, *weights, conv0, sconv0, ssd0)


def _lambda(lq_ref, lk_ref, lam0):
    lq, lk = lq_ref[...], lk_ref[...]
    e0 = jnp.exp(jnp.sum(lq[0:1] * lk[0:1], axis=-1, keepdims=True))
    e1 = jnp.exp(jnp.sum(lq[1:2] * lk[1:2], axis=-1, keepdims=True))
    return e0 - e1 + lam0


def _attn_prompt_kernel(q_ref, k_ref, vt_ref, ext_ref, corr_ref, lq_ref, lk_ref, o_ref, qa_s, acc_s, ml_s, *, lam0):
    blk = ATT_BLOCK
    n = 2 * ATT_HEADS
    qi = pl.program_id(1)
    lane = lax.broadcasted_iota(jnp.int32, (blk, LANES), 1)
    first = lane < ATT_QK_DIM
    ones_from = lambda c: jnp.where((lane >= c) & (lane < c + BIAS_TERMS), 1.0, 0.0).astype(BF16)
    head_cols = lambda h: slice(h * ATT_V_DIM, (h + 1) * ATT_V_DIM)
    for h in range(ATT_HEADS):
        qh = q_ref[:, head_cols(h)]
        qa_s[2 * h] = jnp.where(first, qh, ones_from(ATT_QK_DIM))
        qa_s[2 * h + 1] = jnp.where(first, ones_from(0), qh)
    acc_s[...] = jnp.zeros_like(acc_s)
    ml_s[0:n, :] = jnp.full((n, blk), NEG_BIG, F32)
    ml_s[n:2 * n, :] = jnp.zeros((n, blk), F32)

    def tile(kj, diagonal):
        k0 = pl.multiple_of(kj * blk, blk)
        k0f = k0.astype(F32)
        s = []
        for h in range(ATT_HEADS):
            kh = k_ref[pl.ds(k0, blk), head_cols(h)]
            s.append(_dot_nt(jnp.where(first, kh, ext_ref[h, 0]), qa_s[2 * h]))
            s.append(_dot_nt(jnp.where(first, ext_ref[h, 1], kh), qa_s[2 * h + 1]))
        p, alpha = [], []
        for c in range(n):
            h = c // 2
            sc = s[c] + corr_ref[h] if diagonal else s[c]
            base = (ALIBI_SLOPES[h] * LOG2E) * k0f
            m_old = ml_s[c:c + 1, :]
            m_new = jnp.maximum(m_old, jnp.max(sc, axis=0, keepdims=True) + base)
            a = jnp.exp2(m_old - m_new)
            pc = jnp.exp2(sc - (m_new - base))
            ml_s[c:c + 1, :] = m_new
            ml_s[n + c:n + c + 1, :] = a * ml_s[n + c:n + c + 1, :] + jnp.sum(pc, axis=0, keepdims=True)
            p.append(pc.astype(BF16))
            alpha.append(a)
        for c in range(n):
            vt = vt_ref[head_cols(c // 2), pl.ds(k0, blk)]
            acc_s[c] = alpha[c] * acc_s[c] + _dot(vt, p[c])

    def off_diagonal(kj, carry):
        tile(kj, False)
        return carry

    lax.fori_loop(0, qi, off_diagonal, 0)
    tile(qi, True)
    lam = _lambda(lq_ref, lk_ref, lam0)
    for h in range(ATT_HEADS):
        l1, l2 = ml_s[n + 2 * h:n + 2 * h + 1, :], ml_s[n + 2 * h + 1:n + 2 * h + 2, :]
        o_t = acc_s[2 * h] / l1 - lam * (acc_s[2 * h + 1] / l2)
        o_ref[:, head_cols(h)] = o_t.T


def _attn_tables():
    blk = ATT_BLOCK
    pos = np.arange(blk, dtype=np.float64)
    ext = np.zeros((ATT_HEADS, 2, blk, LANES), np.float32)
    corr = np.zeros((ATT_HEADS, blk, blk), np.float32)
    kpos, qpos = pos[:, None], pos[None, :]
    visible = (kpos // CHUNK) <= (qpos // CHUNK)
    to_bf16 = lambda x: x.astype(ml_dtypes.bfloat16).astype(np.float64)
    for h, slope in enumerate(ALIBI_SLOPES):
        rest = slope * LOG2E * pos
        for t in range(BIAS_TERMS):
            term = to_bf16(rest)
            ext[h, 0, :, ATT_QK_DIM + t] = term
            ext[h, 1, :, t] = term
            rest = rest - term
        corr[h] = np.where(visible, -2.0 * slope * LOG2E * np.maximum(kpos - qpos, 0.0), NEG_BIG)
    return jnp.asarray(ext, BF16), jnp.asarray(corr, F32)


def _attn_prompt_call(qb, kb, vb, lq, lk, lam0):
    b, S, W = qb.shape
    blk = ATT_BLOCK
    vt = jnp.swapaxes(vb, 1, 2)
    ext, corr = _attn_tables()
    whole = lambda shape: pl.BlockSpec((None,) + shape, lambda s, i: (s, 0, 0), pipeline_mode=pl.Buffered(1))
    return pl.pallas_call(
        functools.partial(_attn_prompt_kernel, lam0=lam0),
        grid=(b, S // blk),
        in_specs=[pl.BlockSpec((None, blk, W), lambda s, i: (s, i, 0)), whole((S, W)), whole((W, S)),
                  _resident(ext.shape), _resident(corr.shape), _resident(lq.shape), _resident(lk.shape)],
        out_specs=pl.BlockSpec((None, blk, W), lambda s, i: (s, i, 0)),
        out_shape=jax.ShapeDtypeStruct((b, S, W), F32),
        scratch_shapes=[pltpu.VMEM((2 * ATT_HEADS, blk, LANES), BF16), pltpu.VMEM((2 * ATT_HEADS, ATT_V_DIM, blk), F32),
                        pltpu.VMEM((4 * ATT_HEADS, blk), F32)],
        compiler_params=_params(("arbitrary", "arbitrary")),
        name="attn_prompt",
    )(qb, kb, vt, ext, corr, lq, lk)


def _attn_sample_kernel(q_ref, kn_ref, vn_ref, kc_ref, vc_ref, lq_ref, lk_ref, o_ref, *, lam0):
    ls, past = q_ref.shape[0], kc_ref.shape[0]
    lam = _lambda(lq_ref, lk_ref, lam0)
    lane = lax.broadcasted_iota(jnp.int32, (ls, LANES), 1)
    first = lane < ATT_QK_DIM

    def bias_mask(nk, k_first):
        qpos = past + lax.broadcasted_iota(jnp.int32, (ls, nk), 0)
        kpos = k_first + lax.broadcasted_iota(jnp.int32, (ls, nk), 1)
        dist = jnp.abs(qpos - kpos).astype(F32)
        visible = jnp.right_shift(kpos, CHUNK_SHIFT) <= jnp.right_shift(qpos, CHUNK_SHIFT)
        return dist, visible

    dist_c, vis_c = bias_mask(past, 0)
    dist_n, vis_n = bias_mask(ls, past)
    outs = []
    for h in range(ATT_HEADS):
        cols = slice(h * ATT_V_DIM, (h + 1) * ATT_V_DIM)
        qh = q_ref[:, cols]
        kc, kn = kc_ref[:, cols].astype(BF16), kn_ref[:, cols]
        vc, vn = vc_ref[:, cols].astype(BF16), vn_ref[:, cols]
        o = []
        for c in range(2):
            qm = jnp.where(first == (c == 0), qh, jnp.zeros_like(qh))
            sc = jnp.where(vis_c, _dot_nt(qm, kc) - (ALIBI_SLOPES[h] * LOG2E) * dist_c, NEG_BIG)
            sn = jnp.where(vis_n, _dot_nt(qm, kn) - (ALIBI_SLOPES[h] * LOG2E) * dist_n, NEG_BIG)
            m = jnp.maximum(jnp.max(sc, axis=-1, keepdims=True), jnp.max(sn, axis=-1, keepdims=True))
            pc, pn = jnp.exp2(sc - m), jnp.exp2(sn - m)
            l = jnp.sum(pc, axis=-1, keepdims=True) + jnp.sum(pn, axis=-1, keepdims=True)
            o.append((_dot(pc.astype(BF16), vc) + _dot(pn.astype(BF16), vn)) / l)
        outs.append(o[0] - lam * o[1])
    o_ref[...] = jnp.concatenate(outs, axis=1)


def _attn_sample_call(qb, kb, vb, cache_k, cache_v, lq, lk, lam0):
    b, ls, W = qb.shape
    past = cache_k.shape[1]
    new = pl.BlockSpec((None, ls, W), lambda s: (s, 0, 0))
    old = pl.BlockSpec((None, past, W), lambda s: (s, 0, 0))
    return pl.pallas_call(
        functools.partial(_attn_sample_kernel, lam0=lam0),
        grid=(b,),
        in_specs=[new, new, new, old, old, _resident(lq.shape), _resident(lk.shape)],
        out_specs=new,
        out_shape=jax.ShapeDtypeStruct((b, ls, W), F32),
        compiler_params=_params(("arbitrary",)),
        name="attn_sample",
    )(qb, kb, vb, cache_k, cache_v, lq, lk)


def _prep_layer(l, w_in, b_gate, conv_dw_w, conv_dw_b, conv_ln_g, conv_ln_b, w_br_conv, ssd_conv_w, ssd_conv_b,
                ssd_dt_bias, ssd_A_log, ssd_D, ssd_norm_g, w_br_ssd):
    splits = (2 * CONV_CH, SSD_INNER, SSD_CONV_CH, SSD_HEADS, 3 * ATT_WIDTH, N_BRANCH * D_MODEL)
    pts = np.cumsum(splits)[:-1].tolist()
    w_glu, w_z, w_xbc, w_dt, w_qkv, w_gate = jnp.split(w_in[l].astype(BF16), pts, axis=1)
    rep = lambda a: jnp.repeat(a, SSD_HEAD_DIM, axis=-1)
    return dict(
        w_glu=w_glu, w_z=w_z, w_xbc=w_xbc, w_dt_rep=rep(w_dt), w_dt_t=w_dt.T, w_qkv=w_qkv, w_gate=w_gate,
        b_gate=b_gate[l].reshape(1, N_BRANCH * D_MODEL),
        conv_w=conv_dw_w[l], conv_b=conv_dw_b[l][None], ln_g=conv_ln_g[l][None], ln_b=conv_ln_b[l][None],
        w_br_conv=w_br_conv[l].astype(BF16),
        sconv_w=ssd_conv_w[l], sconv_b=ssd_conv_b[l][None],
        dtb_rep=rep(ssd_dt_bias[l])[None], dtb_col=ssd_dt_bias[l][:, None],
        alog_rep=rep(ssd_A_log[l])[None], alog_col=ssd_A_log[l][:, None],
        d_rep=rep(ssd_D[l])[None], norm_g=ssd_norm_g[l][None], w_br_ssd=w_br_ssd[l].astype(BF16))


def _state_in(conv, sconv, ssd):
    b = conv.shape[0]
    conv = jnp.pad(conv, ((0, 0), (CONV_PAD - (CONV_WIDTH - 1), 0), (0, 0)))
    sconv = jnp.pad(sconv, ((0, 0), (SCONV_PAD - (SSD_CONV_WIDTH - 1), 0), (0, 0)))
    ssd = jnp.transpose(ssd, (0, 3, 1, 2)).reshape(b, SSD_STATE, SSD_INNER)
    return conv, sconv, ssd


def _ssd_state_out(s):
    b = s.shape[0]
    return jnp.transpose(s.reshape(b, SSD_STATE, SSD_HEADS, SSD_HEAD_DIM), (0, 2, 3, 1))


def kernel(x_prompt, x_sample, cache_attn_k, cache_attn_v, state_conv, state_ssd_conv, state_ssd, c_prompt, c_sample,
           w_ada, b_ada, norm_pre, norm_post, w_ffn_in, w_ffn_out, w_in, b_gate, conv_dw_w, conv_dw_b, conv_ln_g,
           conv_ln_b, w_br_conv, ssd_conv_w, ssd_conv_b, ssd_dt_bias, ssd_A_log, ssd_D, ssd_norm_g, w_br_ssd,
           lambda_q, lambda_k, attn_subln_g, w_br_attn, w_mix_out):
    depth = w_ada.shape[0]
    bp, S, D = x_prompt.shape
    bs, ls, _ = x_sample.shape
    past = cache_attn_k.shape[2]

    mods = _ada_call(jnp.concatenate([c_prompt, c_sample], axis=0), w_ada, b_ada)
    mods = mods.reshape(depth, bp + bs, N_SUB, 3, D)

    zeros = _state_in(jnp.zeros((bp, CONV_WIDTH - 1, CONV_CH), F32), jnp.zeros((bp, SSD_CONV_WIDTH - 1, SSD_CONV_CH), F32),
                      jnp.zeros((bp, SSD_HEADS, SSD_HEAD_DIM, SSD_STATE), F32))
    xp = x_prompt
    xs = x_sample.reshape(1, bs * ls, D)
    st_p, st_s = [], []
    for l in range(depth):
        lam0 = _lambda_init(l)
        w = _prep_layer(l, w_in, b_gate, conv_dw_w, conv_dw_b, conv_ln_g, conv_ln_b, w_br_conv, ssd_conv_w, ssd_conv_b,
                        ssd_dt_bias, ssd_A_log, ssd_D, ssd_norm_g, w_br_ssd)
        wf_in, wf_out = w_ffn_in[l].astype(BF16), w_ffn_out[l].astype(BF16)
        w_bra, w_mix = w_br_attn[l].astype(BF16), w_mix_out[l].astype(BF16)
        gpre = lambda s: norm_pre[l, s][None]
        gpost = lambda s: norm_post[l, s][None]
        subg = attn_subln_g[l][None]
        mp = lambda s, k: mods[l, :bp, s, k][:, None, :]
        ms_seq = lambda s, k: mods[l, bp:, s, k][:, None, :]
        ms_tok = lambda s, k: jnp.repeat(mods[l, bp:, s, k], ls, axis=0)[None]

        xp = _ffn_call(xp, (mp(0, 0), mp(0, 1), mp(0, 2)), gpre(0), gpost(0), wf_in[0], wf_out[0])
        p, g2, k, v, qb, kb, vb, convn, sconvn, ssdn = _mix_call(xp, mp(1, 0), mp(1, 1), gpre(1), w, *zeros)
        o = _attn_prompt_call(qb, kb, vb, lambda_q[l], lambda_k[l], lam0)
        mix = dict(p=p, g2=g2, o=o, subg=subg, w_br_attn=w_bra, w_mix_out=w_mix, gate=mp(1, 2), gpost=gpost(1))
        xp = _ffn_call(xp, (mp(2, 0), mp(2, 1), mp(2, 2)), gpre(2), gpost(2), wf_in[1], wf_out[1], mix=mix, lam0=lam0)
        st_p.append((k.reshape(bp, S, ATT_HEADS, ATT_V_DIM), v.reshape(bp, S, ATT_HEADS, ATT_V_DIM), convn, sconvn,
                     _ssd_state_out(ssdn)))

        xs = _ffn_call(xs, (ms_tok(0, 0), ms_tok(0, 1), ms_tok(0, 2)), gpre(0), gpost(0), wf_in[0], wf_out[0])
        states = _state_in(state_conv[l], state_ssd_conv[l], state_ssd[l])
        p, g2, k, v, qb, kb, vb, convn, sconvn, ssdn = _mix_call(xs.reshape(bs, ls, D), ms_seq(1, 0), ms_seq(1, 1),
                                                                 gpre(1), w, *states)
        o = _attn_sample_call(qb, kb, vb, cache_attn_k[l].reshape(bs, past, ATT_WIDTH),
                              cache_attn_v[l].reshape(bs, past, ATT_WIDTH), lambda_q[l], lambda_k[l], lam0)
        flat = lambda t: t.reshape(1, bs * ls, t.shape[-1])
        mix = dict(p=flat(p), g2=flat(g2), o=flat(o), subg=subg, w_br_attn=w_bra, w_mix_out=w_mix, gate=ms_tok(1, 2),
                   gpost=gpost(1))
        xs = _ffn_call(xs, (ms_tok(2, 0), ms_tok(2, 1), ms_tok(2, 2)), gpre(2), gpost(2), wf_in[1], wf_out[1], mix=mix,
                       lam0=lam0)
        st_s.append((k.reshape(bs, ls, ATT_HEADS, ATT_V_DIM), v.reshape(bs, ls, ATT_HEADS, ATT_V_DIM), convn, sconvn,
                     _ssd_state_out(ssdn)))

    stack = lambda st, i: jnp.stack([s[i] for s in st])
    return (xp, xs.reshape(bs, ls, D),
            stack(st_p, 0), stack(st_p, 1), stack(st_p, 2), stack(st_p, 3), stack(st_p, 4),
            stack(st_s, 0), stack(st_s, 1), stack(st_s, 2), stack(st_s, 3), stack(st_s, 4))
```

```python
import functools
import math

import ml_dtypes
import numpy as np
import jax
import jax.numpy as jnp
from jax import lax
from jax.experimental import pallas as pl
from jax.experimental.pallas import tpu as pltpu

F32, BF16 = jnp.float32, jnp.bfloat16

D_MODEL = 1024
D_FF = 2816
CONV_CH = 512
CONV_WIDTH = 31
SSD_HEADS = 16
SSD_HEAD_DIM = 64
SSD_INNER = SSD_HEADS * SSD_HEAD_DIM
SSD_GROUPS = 2
SSD_STATE = 128
SSD_CONV_WIDTH = 4
SSD_CONV_CH = SSD_INNER + 2 * SSD_GROUPS * SSD_STATE
CHUNK = 64
CHUNK_SHIFT = 6
ATT_HEADS = 4
ATT_QK_DIM = 64
ATT_V_DIM = 2 * ATT_QK_DIM
ATT_WIDTH = ATT_HEADS * ATT_V_DIM
ATT_SCALE = ATT_QK_DIM ** -0.5
ALIBI_SLOPES = tuple(2.0 ** (-8.0 * (h + 1) / ATT_HEADS) for h in range(ATT_HEADS))
N_BRANCH = 3
N_SUB = 3
EPS = 1e-6
SUBLN_EPS = 1e-5
CONV_LN_EPS = 1e-5
NEG_BIG = -1e30
LOG2E = math.log2(math.e)
BIAS_TERMS = 3
ATT_ONES_ROWS = 16
ATT_SKIP_LOG2 = 160.0
NORM_SLACK = 1.0 + 2.0 ** -6
ATT_PHASES = (((0, 1, 2, 3), 1), ((1, 2, 3), 1), ((2, 3), 2), ((3,), 4))

VMEM_LIMIT_BYTES = 58 * 1024 * 1024
LANES = 128
SUBLANES = 8

FFN_ROWS = 512
FFN_CHUNKS = 2
MIX_ROWS = 256
ATT_BLOCK = 256
ADA_COLS = 1152
CONV_PAD = 32
SCONV_PAD = 8


def _lambda_init(layer):
    return 0.8 - 0.6 * math.exp(-0.3 * layer)


def _params(sem):
    return pltpu.CompilerParams(dimension_semantics=sem, vmem_limit_bytes=VMEM_LIMIT_BYTES)


def _resident(shape):
    nd = len(shape)
    return pl.BlockSpec(shape, lambda *_: (0,) * nd, pipeline_mode=pl.Buffered(1))


def _dot(a, b):
    return jnp.dot(a, b, preferred_element_type=F32)


def _dot_nt(a, b):
    return lax.dot_general(a, b, (((1,), (1,)), ((), ())), preferred_element_type=F32)


def _dot_tn(a, b):
    return lax.dot_general(a, b, (((0,), (0,)), ((), ())), preferred_element_type=F32)


def _rms(x, g, eps=EPS):
    return x * lax.rsqrt(jnp.mean(x * x, axis=-1, keepdims=True) + eps) * g


def _silu(x):
    return x * jax.nn.sigmoid(x)


def _split3(a):
    hi = a.astype(BF16)
    r = a - hi.astype(F32)
    mid = r.astype(BF16)
    lo = (r - mid.astype(F32)).astype(BF16)
    return hi, mid, lo


def _ada_kernel(c_ref, w_ref, b_ref, o_ref):
    sc = _silu(c_ref[...]).astype(BF16)
    o_ref[...] = _dot(sc, w_ref[...].astype(BF16)) + b_ref[...]


def _ada_call(c_all, w_ada, b_ada):
    depth, d, n = w_ada.shape
    rows = c_all.shape[0]
    return pl.pallas_call(
        _ada_kernel,
        grid=(depth, n // ADA_COLS),
        in_specs=[pl.BlockSpec((rows, d), lambda l, j: (0, 0)),
                  pl.BlockSpec((None, d, ADA_COLS), lambda l, j: (l, 0, j)),
                  pl.BlockSpec((None, 1, ADA_COLS), lambda l, j: (l, 0, j))],
        out_specs=pl.BlockSpec((None, rows, ADA_COLS), lambda l, j: (l, 0, j)),
        out_shape=jax.ShapeDtypeStruct((depth, rows, n), F32),
        compiler_params=_params(("arbitrary", "arbitrary")),
        name="adaln",
    )(c_all, w_ada, b_ada.reshape(depth, 1, n))


def _ffn_kernel(*refs, with_mix, lam0):
    if with_mix:
        (x_ref, p_ref, g2_ref, o_ref, subg_ref, wbr_ref, wmix_ref, gate1_ref, gpost1_ref, *rest) = refs
    else:
        x_ref, *rest = refs
    shift_ref, scale_ref, gate_ref, gpre_ref, gpost_ref, win_ref, wout_ref, out_ref = rest
    x = x_ref[...]
    if with_mix:
        o = o_ref[...]
        subg = subg_ref[...]
        heads = []
        for h in range(ATT_HEADS):
            oh = o[:, h * ATT_V_DIM:(h + 1) * ATT_V_DIM]
            heads.append(_rms(oh, subg, SUBLN_EPS) * (1.0 - lam0))
        c_out = _dot(jnp.concatenate(heads, axis=1).astype(BF16), wbr_ref[...])
        merged = (p_ref[...] + g2_ref[...] * c_out).astype(BF16)
        x = x + gate1_ref[...] * _rms(_dot(merged, wmix_ref[...]), gpost1_ref[...])
    h = (_rms(x, gpre_ref[...]) * (1.0 + scale_ref[...]) + shift_ref[...]).astype(BF16)
    fc = D_FF // FFN_CHUNKS
    f = None
    for c in range(FFN_CHUNKS):
        up = _dot(h, win_ref[:, c * fc:(c + 1) * fc])
        gt = _dot(h, win_ref[:, D_FF + c * fc:D_FF + (c + 1) * fc])
        part = _dot((_silu(gt) * up).astype(BF16), wout_ref[c * fc:(c + 1) * fc, :])
        f = part if f is None else f + part
    out_ref[...] = x + 0.5 * gate_ref[...] * _rms(f, gpost_ref[...])


def _ffn_call(x, mod, gpre, gpost, w_in, w_out, mix=None, lam0=0.0):
    nseq, L, D = x.shape
    bm = min(FFN_ROWS, L)
    row_spec = pl.BlockSpec((None, bm, D), lambda s, i: (s, i, 0))

    def mod_spec(m):
        if m.shape[1] == 1:
            return pl.BlockSpec((None, 1, D), lambda s, i: (s, 0, 0))
        return row_spec

    args, specs = [x], [row_spec]
    if mix is not None:
        args += [mix["p"], mix["g2"], mix["o"], mix["subg"], mix["w_br_attn"], mix["w_mix_out"], mix["gate"], mix["gpost"]]
        specs += [row_spec, row_spec, pl.BlockSpec((None, bm, ATT_WIDTH), lambda s, i: (s, i, 0)),
                  _resident((1, ATT_V_DIM)), _resident((ATT_WIDTH, D)), _resident((D, D)),
                  mod_spec(mix["gate"]), _resident((1, D))]
    args += [mod[0], mod[1], mod[2], gpre, gpost, w_in, w_out]
    specs += [mod_spec(mod[0]), mod_spec(mod[1]), mod_spec(mod[2]), _resident((1, D)), _resident((1, D)),
              _resident(w_in.shape), _resident(w_out.shape)]
    return pl.pallas_call(
        functools.partial(_ffn_kernel, with_mix=mix is not None, lam0=lam0),
        grid=(nseq, L // bm),
        in_specs=specs,
        out_specs=row_spec,
        out_shape=jax.ShapeDtypeStruct(x.shape, F32),
        compiler_params=_params(("arbitrary", "arbitrary")),
        name="merge_ffn" if mix is not None else "ffn",
    )(*args)


def _dwconv(src_ref, first, w, b, rows, rc, shift_ref=None):
    width, ch = w.shape
    if shift_ref is not None:
        span = shift_ref.shape[1]
        for s in sorted({(first + j) % SUBLANES for j in range(width)} - {0}):
            shift_ref[s - 1] = src_ref[s:s + span, :]

    def tap(j, r0):
        a, s = divmod(first + j, SUBLANES)
        if shift_ref is None or s == 0:
            return src_ref[first + j + r0:first + j + r0 + rc, :]
        return shift_ref[s - 1, a * SUBLANES + r0:a * SUBLANES + r0 + rc, :]

    outs = []
    for r0 in range(0, rows, rc):
        acc = jnp.broadcast_to(b, (rc, ch))
        for j in range(width):
            acc = acc + w[j:j + 1, :] * tap(j, r0)
        outs.append(acc)
    return jnp.concatenate(outs, axis=0)


def _mix_kernel(x_ref, shift_ref, scale_ref, gpre_ref,
                wglu_ref, wz_ref, wxbc_ref, wdtr_ref, wdtt_ref, wqkv_ref, wgate_ref, bgate_ref,
                cw_ref, cb_ref, lng_ref, lnb_ref, wbrc_ref,
                sw_ref, sb_ref, dtbr_ref, dtbc_ref, alr_ref, alc_ref, dr_ref, ng_ref, wbrs_ref,
                conv0_ref, sconv0_ref, ssd0_ref,
                p_ref, g2_ref, k_ref, v_ref, qb_ref, kb_ref, vb_ref, convn_ref, sconvn_ref, ssdn_ref,
                conv_s, sconv_s, state_s, shift_s, *, bm, q):
    @pl.when(pl.program_id(1) == 0)
    def _():
        conv_s[0:CONV_PAD, :] = conv0_ref[...]
        sconv_s[0:SCONV_PAD, :] = sconv0_ref[...]
        state_s[...] = ssd0_ref[...]

    h = (_rms(x_ref[...], gpre_ref[...]) * (1.0 + scale_ref[...]) + shift_ref[...]).astype(BF16)

    glu = _dot(h, wglu_ref[...])
    conv_s[CONV_PAD:CONV_PAD + bm, :] = glu[:, :CONV_CH] * jax.nn.sigmoid(glu[:, CONV_CH:])
    a = _dwconv(conv_s, CONV_PAD - (CONV_WIDTH - 1), cw_ref[...], cb_ref[...], bm, 32, shift_s)
    mu = jnp.mean(a, axis=-1, keepdims=True)
    ac = a - mu
    a = ac * lax.rsqrt(jnp.mean(ac * ac, axis=-1, keepdims=True) + CONV_LN_EPS) * lng_ref[...] + lnb_ref[...]
    a_out = _dot(_silu(a).astype(BF16), wbrc_ref[...])
    convn_ref[...] = conv_s[CONV_PAD + bm - (CONV_WIDTH - 1):CONV_PAD + bm, :]
    conv_s[0:CONV_PAD, :] = conv_s[bm:bm + CONV_PAD, :]

    sconv_s[SCONV_PAD:SCONV_PAD + bm, :] = _dot(h, wxbc_ref[...])
    xbc = _silu(_dwconv(sconv_s, SCONV_PAD - (SSD_CONV_WIDTH - 1), sw_ref[...], sb_ref[...], bm, 16))
    sconvn_ref[...] = sconv_s[SCONV_PAD + bm - (SSD_CONV_WIDTH - 1):SCONV_PAD + bm, :]
    sconv_s[0:SCONV_PAD, :] = sconv_s[bm:bm + SCONV_PAD, :]

    dt_e = jax.nn.softplus(_dot(h, wdtr_ref[...]) + dtbr_ref[...])
    dt_t = jax.nn.softplus(_dot_nt(wdtt_ref[...], h) + dtbc_ref[...])
    a_rep = -jnp.exp(alr_ref[...])
    a_col = -jnp.exp(alc_ref[...])
    row = lax.broadcasted_iota(jnp.int32, (q, q), 0)
    col = lax.broadcasted_iota(jnp.int32, (q, q), 1)
    causal = row >= col
    tril = causal.astype(BF16)
    triu = (row <= col).astype(BF16)
    gw = SSD_INNER // SSD_GROUPS
    ys = []
    for c in range(bm // q):
        rows = slice(c * q, (c + 1) * q)
        dt_c = dt_e[rows]
        hi, mid, lo = _split3(dt_c * a_rep)
        acum = _dot(tril, hi) + _dot(tril, mid) + _dot(tril, lo)
        hi, mid, lo = _split3(dt_t[:, rows] * a_col)
        acum_t = _dot(hi, triu) + _dot(mid, triu) + _dot(lo, triu)
        xs_c = xbc[rows, :SSD_INNER]
        xdt = xs_c * dt_c
        a_last = acum[q - 1:q, :]
        x_end = (xdt * jnp.exp(a_last - acum)).astype(BF16)
        e_in = jnp.exp(acum)
        xdt_b = xdt.astype(BF16)
        state = state_s[...]
        state_b = state.astype(BF16)
        y_parts = []
        for g in range(SSD_GROUPS):
            bg = xbc[rows, SSD_INNER + g * SSD_STATE:SSD_INNER + (g + 1) * SSD_STATE]
            cg = xbc[rows, SSD_INNER + (SSD_GROUPS + g) * SSD_STATE:SSD_INNER + (SSD_GROUPS + g + 1) * SSD_STATE]
            bg_b, cg_b = bg.astype(BF16), cg.astype(BF16)
            cb = _dot_nt(cg_b, bg_b)
            y_off = _dot(cg_b, state_b[:, g * gw:(g + 1) * gw]) * e_in[:, g * gw:(g + 1) * gw]
            y_diag = []
            for r in range(SSD_HEADS // SSD_GROUPS):
                hd = g * (SSD_HEADS // SSD_GROUPS) + r
                lo_l = hd * SSD_HEAD_DIM
                seg = acum[:, lo_l:lo_l + q] - acum_t[hd:hd + 1, :]
                m = (cb * jnp.exp(jnp.where(causal, seg, NEG_BIG))).astype(BF16)
                y_diag.append(_dot(m, xdt_b[:, lo_l:lo_l + SSD_HEAD_DIM]))
            y_parts.append(jnp.concatenate(y_diag, axis=1) + y_off)
            state_s[:, g * gw:(g + 1) * gw] = (state[:, g * gw:(g + 1) * gw] * jnp.exp(a_last[:, g * gw:(g + 1) * gw])
                                               + _dot_tn(bg_b, x_end[:, g * gw:(g + 1) * gw]))
        ys.append(jnp.concatenate(y_parts, axis=1) + dr_ref[...] * xs_c)
    ssdn_ref[...] = state_s[...]
    y = jnp.concatenate(ys, axis=0) if len(ys) > 1 else ys[0]
    yz = y * _silu(_dot(h, wz_ref[...]))
    yz = jnp.concatenate([_rms(yz[:, g * gw:(g + 1) * gw], 1.0) for g in range(SSD_GROUPS)], axis=1) * ng_ref[...]
    b_out = _dot(yz.astype(BF16), wbrs_ref[...])

    gates = jax.nn.sigmoid(_dot(h, wgate_ref[...]) + bgate_ref[...])
    p_ref[...] = gates[:, :D_MODEL] * a_out + gates[:, D_MODEL:2 * D_MODEL] * b_out
    g2_ref[...] = gates[:, 2 * D_MODEL:]

    qkv = _dot(h, wqkv_ref[...])
    k = qkv[:, ATT_WIDTH:2 * ATT_WIDTH]
    v = qkv[:, 2 * ATT_WIDTH:]
    k_ref[...] = k
    v_ref[...] = v
    qb_ref[...] = (qkv[:, :ATT_WIDTH] * (ATT_SCALE * LOG2E)).astype(BF16)
    kb_ref[...] = k.astype(BF16)
    vb_ref[...] = v.astype(BF16)


def _mix_call(x, shift, scale, gpre, w, conv0, sconv0, ssd0):
    nseq, L, D = x.shape
    bm = min(MIX_ROWS, L)
    q = min(CHUNK, bm)
    rows = lambda n: pl.BlockSpec((None, bm, n), lambda s, i: (s, i, 0))
    per_seq = lambda a: pl.BlockSpec((None,) + a.shape[1:], lambda s, i: (s, 0, 0))
    weights = [gpre, w["w_glu"], w["w_z"], w["w_xbc"], w["w_dt_rep"], w["w_dt_t"], w["w_qkv"], w["w_gate"], w["b_gate"],
               w["conv_w"], w["conv_b"], w["ln_g"], w["ln_b"], w["w_br_conv"],
               w["sconv_w"], w["sconv_b"], w["dtb_rep"], w["dtb_col"], w["alog_rep"], w["alog_col"], w["d_rep"],
               w["norm_g"], w["w_br_ssd"]]
    out_shape = [jax.ShapeDtypeStruct((nseq, L, D), F32), jax.ShapeDtypeStruct((nseq, L, D), F32),
                 jax.ShapeDtypeStruct((nseq, L, ATT_WIDTH), F32), jax.ShapeDtypeStruct((nseq, L, ATT_WIDTH), F32),
                 jax.ShapeDtypeStruct((nseq, L, ATT_WIDTH), BF16), jax.ShapeDtypeStruct((nseq, L, ATT_WIDTH), BF16),
                 jax.ShapeDtypeStruct((nseq, L, ATT_WIDTH), BF16),
                 jax.ShapeDtypeStruct((nseq, CONV_WIDTH - 1, CONV_CH), F32),
                 jax.ShapeDtypeStruct((nseq, SSD_CONV_WIDTH - 1, SSD_CONV_CH), F32),
                 jax.ShapeDtypeStruct((nseq, SSD_STATE, SSD_INNER), F32)]
    out_specs = [rows(D), rows(D), rows(ATT_WIDTH), rows(ATT_WIDTH), rows(ATT_WIDTH), rows(ATT_WIDTH), rows(ATT_WIDTH),
                 per_seq(out_shape[7]), per_seq(out_shape[8]), per_seq(out_shape[9])]
    return pl.pallas_call(
        functools.partial(_mix_kernel, bm=bm, q=q),
        grid=(nseq, L // bm),
        in_specs=[rows(D), per_seq(shift), per_seq(scale)] + [_resident(a.shape) for a in weights]
                 + [per_seq(conv0), per_seq(sconv0), per_seq(ssd0)],
        out_specs=out_specs,
        out_shape=out_shape,
        scratch_shapes=[pltpu.VMEM((CONV_PAD + bm, CONV_CH), F32), pltpu.VMEM((SCONV_PAD + bm, SSD_CONV_CH), F32),
                        pltpu.VMEM((SSD_STATE, SSD_INNER), F32),
                        pltpu.VMEM((SUBLANES - 1, CONV_PAD + bm - SUBLANES, CONV_CH), F32)],
        compiler_params=_params(("arbitrary", "arbitrary")),
        name="mix_front",
    )(x, shift, scale, *weights, conv0, sconv0, ssd0)


def _lambda(lq_ref, lk_ref, lam0):
    lq, lk = lq_ref[...], lk_ref[...]
    e0 = jnp.exp(jnp.sum(lq[0:1] * lk[0:1], axis=-1, keepdims=True))
    e1 = jnp.exp(jnp.sum(lq[1:2] * lk[1:2], axis=-1, keepdims=True))
    return e0 - e1 + lam0


def _attn_prompt_kernel(q_ref, k_ref, vt_ref, ext_ref, corr_ref, lq_ref, lk_ref, o_ref, qa_s, acc_s, m_s, kn_s, *, lam0):
    blk = ATT_BLOCK
    qi = pl.program_id(1)
    lane = lax.broadcasted_iota(jnp.int32, (blk, LANES), 1)
    first = lane < ATT_QK_DIM
    ones_from = lambda c: jnp.where((lane >= c) & (lane < c + BIAS_TERMS), 1.0, 0.0).astype(BF16)
    head_cols = lambda h: slice(h * ATT_V_DIM, (h + 1) * ATT_V_DIM)
    qk_shift = ATT_QK_DIM.bit_length() - 1
    sel = (jnp.right_shift(lax.broadcasted_iota(jnp.int32, (ATT_WIDTH, LANES), 0), qk_shift)
           == lax.broadcasted_iota(jnp.int32, (ATT_WIDTH, LANES), 1)).astype(BF16)

    def max_sq_norm(x):
        xf = x.astype(F32)
        return jnp.max(_dot((xf * xf).astype(BF16), sel), axis=0, keepdims=True)

    @pl.when(qi == 0)
    def _():
        def body(t, mx):
            return jnp.maximum(mx, max_sq_norm(k_ref[pl.ds(pl.multiple_of(t * blk, blk), blk), :]))
        kn_s[0:1, :] = lax.fori_loop(0, k_ref.shape[0] // blk, body, jnp.zeros((1, LANES), F32))

    q = q_ref[...]
    for h in range(ATT_HEADS):
        qa_s[2 * h] = jnp.where(first, q[:, head_cols(h)], ones_from(ATT_QK_DIM))
        qa_s[2 * h + 1] = jnp.where(first, ones_from(0), q[:, head_cols(h)])
    acc_s[...] = jnp.zeros_like(acc_s)
    m_s[...] = jnp.full(m_s.shape, NEG_BIG, F32)
    ones_rows = jnp.ones((ATT_ONES_ROWS, blk), BF16)

    def step(kjs, heads, diagonal=False):
        k0s = [pl.multiple_of(kj * blk, blk) for kj in kjs]
        s = {}
        for h in heads:
            for t, k0 in enumerate(k0s):
                kh = k_ref[pl.ds(k0, blk), head_cols(h)]
                s[h, 0, t] = _dot_nt(jnp.where(first, kh, ext_ref[h, 0]), qa_s[2 * h])
                s[h, 1, t] = _dot_nt(jnp.where(first, ext_ref[h, 1], kh), qa_s[2 * h + 1])
        upd = {}
        for h in heads:
            bases = [(ALIBI_SLOPES[h] * LOG2E) * k0.astype(F32) for k0 in k0s]
            for c in range(2):
                sc = [s[h, c, t] + corr_ref[h] if diagonal else s[h, c, t] for t in range(len(k0s))]
                m_old = m_s[2 * h + c:2 * h + c + 1, :]
                m_new = m_old
                for t in range(len(k0s)):
                    m_new = jnp.maximum(m_new, jnp.max(sc[t], axis=0, keepdims=True) + bases[t])
                m_s[2 * h + c:2 * h + c + 1, :] = m_new
                p = [jnp.exp2(sc[t] - (m_new - bases[t])).astype(BF16) for t in range(len(k0s))]
                upd[h, c] = (jnp.exp2(m_old - m_new), p)
        for h in heads:
            vt = jnp.concatenate([vt_ref[head_cols(h), pl.ds(k0, blk)] for k0 in k0s], axis=1)
            vt = jnp.concatenate([vt, jnp.concatenate([ones_rows] * len(k0s), axis=1)], axis=0)
            for c in range(2):
                alpha, p = upd[h, c]
                acc_s[2 * h + c] = alpha * acc_s[2 * h + c] + _dot(vt, jnp.concatenate(p, axis=0))

    step([qi], range(ATT_HEADS), diagonal=True)

    bound = jnp.sqrt(max_sq_norm(q)) * jnp.sqrt(kn_s[0:1, :]) * NORM_SLACK
    lo, limit = [], qi
    for h in range(ATT_HEADS):
        gap = None
        for c in range(2):
            g = bound[:, 2 * h + c:2 * h + c + 1] - jnp.min(m_s[2 * h + c:2 * h + c + 1, :], axis=1, keepdims=True)
            gap = g if gap is None else jnp.maximum(gap, g)
        x = (-ATT_SKIP_LOG2 - gap) / (ALIBI_SLOPES[h] * LOG2E)
        need = jnp.ceil((x - (blk - 1)) / blk)
        need = jnp.clip(jnp.where(need == need, need, 0.0), 0.0, qi.astype(F32))
        limit = jnp.minimum(limit, need.astype(jnp.int32)[0, 0])
        lo.append(limit)

    hi = qi
    widths = [w for _, w in ATT_PHASES]
    for i, (heads, width) in enumerate(ATT_PHASES):
        align = max(width, widths[min(i + 1, len(widths) - 1)]).bit_length() - 1
        lo_i = jnp.left_shift(jnp.right_shift(lo[heads[0]], align), align)

        def body(j, carry, hi=hi, heads=heads, width=width):
            step([hi - (j + 1) * width + t for t in range(width)], heads)
            return carry

        lax.fori_loop(0, jnp.right_shift(hi - lo_i, width.bit_length() - 1), body, 0)
        hi = lo_i

    lam = _lambda(lq_ref, lk_ref, lam0)
    for h in range(ATT_HEADS):
        a1, a2 = acc_s[2 * h], acc_s[2 * h + 1]
        o_t = (a1[:ATT_V_DIM] / a1[ATT_V_DIM:ATT_V_DIM + 1]
               - lam * (a2[:ATT_V_DIM] / a2[ATT_V_DIM:ATT_V_DIM + 1]))
        o_ref[:, head_cols(h)] = o_t.T


def _attn_tables():
    blk = ATT_BLOCK
    pos = np.arange(blk, dtype=np.float64)
    ext = np.zeros((ATT_HEADS, 2, blk, LANES), np.float32)
    corr = np.zeros((ATT_HEADS, blk, blk), np.float32)
    kpos, qpos = pos[:, None], pos[None, :]
    visible = (kpos // CHUNK) <= (qpos // CHUNK)
    to_bf16 = lambda x: x.astype(ml_dtypes.bfloat16).astype(np.float64)
    for h, slope in enumerate(ALIBI_SLOPES):
        rest = slope * LOG2E * pos
        for t in range(BIAS_TERMS):
            term = to_bf16(rest)
            ext[h, 0, :, ATT_QK_DIM + t] = term
            ext[h, 1, :, t] = term
            rest = rest - term
        corr[h] = np.where(visible, -2.0 * slope * LOG2E * np.maximum(kpos - qpos, 0.0), NEG_BIG)
    return jnp.asarray(ext, BF16), jnp.asarray(corr, F32)


def _attn_prompt_call(qb, kb, vb, lq, lk, lam0):
    b, S, W = qb.shape
    blk = ATT_BLOCK
    vt = jnp.swapaxes(vb, 1, 2)
    ext, corr = _attn_tables()
    whole = lambda shape: pl.BlockSpec((None,) + shape, lambda s, i: (s, 0, 0), pipeline_mode=pl.Buffered(1))
    return pl.pallas_call(
        functools.partial(_attn_prompt_kernel, lam0=lam0),
        grid=(b, S // blk),
        in_specs=[pl.BlockSpec((None, blk, W), lambda s, i: (s, i, 0)), whole((S, W)), whole((W, S)),
                  _resident(ext.shape), _resident(corr.shape), _resident(lq.shape), _resident(lk.shape)],
        out_specs=pl.BlockSpec((None, blk, W), lambda s, i: (s, i, 0)),
        out_shape=jax.ShapeDtypeStruct((b, S, W), F32),
        scratch_shapes=[pltpu.VMEM((2 * ATT_HEADS, blk, LANES), BF16),
                        pltpu.VMEM((2 * ATT_HEADS, ATT_V_DIM + ATT_ONES_ROWS, blk), F32),
                        pltpu.VMEM((2 * ATT_HEADS, blk), F32), pltpu.VMEM((SUBLANES, LANES), F32)],
        compiler_params=_params(("arbitrary", "arbitrary")),
        name="attn_prompt",
    )(qb, kb, vt, ext, corr, lq, lk)


def _attn_sample_kernel(q_ref, kn_ref, vn_ref, kc_ref, vc_ref, lq_ref, lk_ref, o_ref, *, lam0):
    ls, past = q_ref.shape[0], kc_ref.shape[0]
    lam = _lambda(lq_ref, lk_ref, lam0)
    lane = lax.broadcasted_iota(jnp.int32, (ls, LANES), 1)
    first = lane < ATT_QK_DIM

    def bias_mask(nk, k_first):
        qpos = past + lax.broadcasted_iota(jnp.int32, (ls, nk), 0)
        kpos = k_first + lax.broadcasted_iota(jnp.int32, (ls, nk), 1)
        dist = jnp.abs(qpos - kpos).astype(F32)
        visible = jnp.right_shift(kpos, CHUNK_SHIFT) <= jnp.right_shift(qpos, CHUNK_SHIFT)
        return dist, visible

    dist_c, vis_c = bias_mask(past, 0)
    dist_n, vis_n = bias_mask(ls, past)
    outs = []
    for h in range(ATT_HEADS):
        cols = slice(h * ATT_V_DIM, (h + 1) * ATT_V_DIM)
        qh = q_ref[:, cols]
        kc, kn = kc_ref[:, cols].astype(BF16), kn_ref[:, cols]
        vc, vn = vc_ref[:, cols].astype(BF16), vn_ref[:, cols]
        o = []
        for c in range(2):
            qm = jnp.where(first == (c == 0), qh, jnp.zeros_like(qh))
            sc = jnp.where(vis_c, _dot_nt(qm, kc) - (ALIBI_SLOPES[h] * LOG2E) * dist_c, NEG_BIG)
            sn = jnp.where(vis_n, _dot_nt(qm, kn) - (ALIBI_SLOPES[h] * LOG2E) * dist_n, NEG_BIG)
            m = jnp.maximum(jnp.max(sc, axis=-1, keepdims=True), jnp.max(sn, axis=-1, keepdims=True))
            pc, pn = jnp.exp2(sc - m), jnp.exp2(sn - m)
            l = jnp.sum(pc, axis=-1, keepdims=True) + jnp.sum(pn, axis=-1, keepdims=True)
            o.append((_dot(pc.astype(BF16), vc) + _dot(pn.astype(BF16), vn)) / l)
        outs.append(o[0] - lam * o[1])
    o_ref[...] = jnp.concatenate(outs, axis=1)


def _attn_sample_call(qb, kb, vb, cache_k, cache_v, lq, lk, lam0):
    b, ls, W = qb.shape
    past = cache_k.shape[1]
    new = pl.BlockSpec((None, ls, W), lambda s: (s, 0, 0))
    old = pl.BlockSpec((None, past, W), lambda s: (s, 0, 0))
    return pl.pallas_call(
        functools.partial(_attn_sample_kernel, lam0=lam0),
        grid=(b,),
        in_specs=[new, new, new, old, old, _resident(lq.shape), _resident(lk.shape)],
        out_specs=new,
        out_shape=jax.ShapeDtypeStruct((b, ls, W), F32),
        compiler_params=_params(("arbitrary",)),
        name="attn_sample",
    )(qb, kb, vb, cache_k, cache_v, lq, lk)


def _prep_layer(l, w_in, b_gate, conv_dw_w, conv_dw_b, conv_ln_g, conv_ln_b, w_br_conv, ssd_conv_w, ssd_conv_b,
                ssd_dt_bias, ssd_A_log, ssd_D, ssd_norm_g, w_br_ssd):
    splits = (2 * CONV_CH, SSD_INNER, SSD_CONV_CH, SSD_HEADS, 3 * ATT_WIDTH, N_BRANCH * D_MODEL)
    pts = np.cumsum(splits)[:-1].tolist()
    w_glu, w_z, w_xbc, w_dt, w_qkv, w_gate = jnp.split(w_in[l].astype(BF16), pts, axis=1)
    rep = lambda a: jnp.repeat(a, SSD_HEAD_DIM, axis=-1)
    return dict(
        w_glu=w_glu, w_z=w_z, w_xbc=w_xbc, w_dt_rep=rep(w_dt), w_dt_t=w_dt.T, w_qkv=w_qkv, w_gate=w_gate,
        b_gate=b_gate[l].reshape(1, N_BRANCH * D_MODEL),
        conv_w=conv_dw_w[l], conv_b=conv_dw_b[l][None], ln_g=conv_ln_g[l][None], ln_b=conv_ln_b[l][None],
        w_br_conv=w_br_conv[l].astype(BF16),
        sconv_w=ssd_conv_w[l], sconv_b=ssd_conv_b[l][None],
        dtb_rep=rep(ssd_dt_bias[l])[None], dtb_col=ssd_dt_bias[l][:, None],
        alog_rep=rep(ssd_A_log[l])[None], alog_col=ssd_A_log[l][:, None],
        d_rep=rep(ssd_D[l])[None], norm_g=ssd_norm_g[l][None], w_br_ssd=w_br_ssd[l].astype(BF16))


def _state_in(conv, sconv, ssd):
    b = conv.shape[0]
    conv = jnp.pad(conv, ((0, 0), (CONV_PAD - (CONV_WIDTH - 1), 0), (0, 0)))
    sconv = jnp.pad(sconv, ((0, 0), (SCONV_PAD - (SSD_CONV_WIDTH - 1), 0), (0, 0)))
    ssd = jnp.transpose(ssd, (0, 3, 1, 2)).reshape(b, SSD_STATE, SSD_INNER)
    return conv, sconv, ssd


def _ssd_state_out(s):
    b = s.shape[0]
    return jnp.transpose(s.reshape(b, SSD_STATE, SSD_HEADS, SSD_HEAD_DIM), (0, 2, 3, 1))


def kernel(x_prompt, x_sample, cache_attn_k, cache_attn_v, state_conv, state_ssd_conv, state_ssd, c_prompt, c_sample,
           w_ada, b_ada, norm_pre, norm_post, w_ffn_in, w_ffn_out, w_in, b_gate, conv_dw_w, conv_dw_b, conv_ln_g,
           conv_ln_b, w_br_conv, ssd_conv_w, ssd_conv_b, ssd_dt_bias, ssd_A_log, ssd_D, ssd_norm_g, w_br_ssd,
           lambda_q, lambda_k, attn_subln_g, w_br_attn, w_mix_out):
    depth = w_ada.shape[0]
    bp, S, D = x_prompt.shape
    bs, ls, _ = x_sample.shape
    past = cache_attn_k.shape[2]

    mods = _ada_call(jnp.concatenate([c_prompt, c_sample], axis=0), w_ada, b_ada)
    mods = mods.reshape(depth, bp + bs, N_SUB, 3, D)

    zeros = _state_in(jnp.zeros((bp, CONV_WIDTH - 1, CONV_CH), F32), jnp.zeros((bp, SSD_CONV_WIDTH - 1, SSD_CONV_CH), F32),
                      jnp.zeros((bp, SSD_HEADS, SSD_HEAD_DIM, SSD_STATE), F32))
    xp = x_prompt
    xs = x_sample.reshape(1, bs * ls, D)
    st_p, st_s = [], []
    for l in range(depth):
        lam0 = _lambda_init(l)
        w = _prep_layer(l, w_in, b_gate, conv_dw_w, conv_dw_b, conv_ln_g, conv_ln_b, w_br_conv, ssd_conv_w, ssd_conv_b,
                        ssd_dt_bias, ssd_A_log, ssd_D, ssd_norm_g, w_br_ssd)
        wf_in, wf_out = w_ffn_in[l].astype(BF16), w_ffn_out[l].astype(BF16)
        w_bra, w_mix = w_br_attn[l].astype(BF16), w_mix_out[l].astype(BF16)
        gpre = lambda s: norm_pre[l, s][None]
        gpost = lambda s: norm_post[l, s][None]
        subg = attn_subln_g[l][None]
        mp = lambda s, k: mods[l, :bp, s, k][:, None, :]
        ms_seq = lambda s, k: mods[l, bp:, s, k][:, None, :]
        ms_tok = lambda s, k: jnp.repeat(mods[l, bp:, s, k], ls, axis=0)[None]

        xp = _ffn_call(xp, (mp(0, 0), mp(0, 1), mp(0, 2)), gpre(0), gpost(0), wf_in[0], wf_out[0])
        p, g2, k, v, qb, kb, vb, convn, sconvn, ssdn = _mix_call(xp, mp(1, 0), mp(1, 1), gpre(1), w, *zeros)
        o = _attn_prompt_call(qb, kb, vb, lambda_q[l], lambda_k[l], lam0)
        mix = dict(p=p, g2=g2, o=o, subg=subg, w_br_attn=w_bra, w_mix_out=w_mix, gate=mp(1, 2), gpost=gpost(1))
        xp = _ffn_call(xp, (mp(2, 0), mp(2, 1), mp(2, 2)), gpre(2), gpost(2), wf_in[1], wf_out[1], mix=mix, lam0=lam0)
        st_p.append((k.reshape(bp, S, ATT_HEADS, ATT_V_DIM), v.reshape(bp, S, ATT_HEADS, ATT_V_DIM), convn, sconvn,
                     _ssd_state_out(ssdn)))

        xs = _ffn_call(xs, (ms_tok(0, 0), ms_tok(0, 1), ms_tok(0, 2)), gpre(0), gpost(0), wf_in[0], wf_out[0])
        states = _state_in(state_conv[l], state_ssd_conv[l], state_ssd[l])
        p, g2, k, v, qb, kb, vb, convn, sconvn, ssdn = _mix_call(xs.reshape(bs, ls, D), ms_seq(1, 0), ms_seq(1, 1),
                                                                 gpre(1), w, *states)
        o = _attn_sample_call(qb, kb, vb, cache_attn_k[l].reshape(bs, past, ATT_WIDTH),
                              cache_attn_v[l].reshape(bs, past, ATT_WIDTH), lambda_q[l], lambda_k[l], lam0)
        flat = lambda t: t.reshape(1, bs * ls, t.shape[-1])
        mix = dict(p=flat(p), g2=flat(g2), o=flat(o), subg=subg, w_br_attn=w_bra, w_mix_out=w_mix, gate=ms_tok(1, 2),
                   gpost=gpost(1))
        xs = _ffn_call(xs, (ms_tok(2, 0), ms_tok(2, 1), ms_tok(2, 2)), gpre(2), gpost(2), wf_in[1], wf_out[1], mix=mix,
                       lam0=lam0)
        st_s.append((k.reshape(bs, ls, ATT_HEADS, ATT_V_DIM), v.reshape(bs, ls, ATT_HEADS, ATT_V_DIM), convn, sconvn,
                     _ssd_state_out(ssdn)))

    stack = lambda st, i: jnp.stack([s[i] for s in st])
    return (xp, xs.reshape(bs, ls, D),
            stack(st_p, 0), stack(st_p, 1), stack(st_p, 2), stack(st_p, 3), stack(st_p, 4),
            stack(st_s, 0), stack(st_s, 1), stack(st_s, 2), stack(st_s, 3), stack(st_s, 4))
```

```python
import functools
import math

import ml_dtypes
import numpy as np
import jax
import jax.numpy as jnp
from jax import lax
from jax.experimental import pallas as pl
from jax.experimental.pallas import tpu as pltpu

F32, BF16 = jnp.float32, jnp.bfloat16

D_MODEL = 1024
D_FF = 2816
CONV_CH = 512
CONV_WIDTH = 31
SSD_HEADS = 16
SSD_HEAD_DIM = 64
SSD_INNER = SSD_HEADS * SSD_HEAD_DIM
SSD_GROUPS = 2
SSD_STATE = 128
SSD_CONV_WIDTH = 4
SSD_CONV_CH = SSD_INNER + 2 * SSD_GROUPS * SSD_STATE
CHUNK = 64
CHUNK_SHIFT = 6
ATT_HEADS = 4
ATT_QK_DIM = 64
ATT_V_DIM = 2 * ATT_QK_DIM
ATT_WIDTH = ATT_HEADS * ATT_V_DIM
ATT_SCALE = ATT_QK_DIM ** -0.5
ALIBI_SLOPES = tuple(2.0 ** (-8.0 * (h + 1) / ATT_HEADS) for h in range(ATT_HEADS))
N_BRANCH = 3
N_SUB = 3
EPS = 1e-6
SUBLN_EPS = 1e-5
CONV_LN_EPS = 1e-5
NEG_BIG = -1e30
LOG2E = math.log2(math.e)
BIAS_TERMS = 3
ATT_ONES_ROWS = 16
ATT_SKIP_LOG2 = 160.0
NORM_SLACK = 1.0 + 2.0 ** -6
ATT_PHASES = (((0, 1, 2, 3), 1), ((1, 2, 3), 1), ((2, 3), 2), ((3,), 4))
assert all(len(heads) * width <= ATT_HEADS for heads, width in ATT_PHASES)

VMEM_LIMIT_BYTES = 58 * 1024 * 1024
LANES = 128
SUBLANES = 8

FFN_ROWS = 512
FFN_CHUNKS = 2
MIX_ROWS = 256
ATT_BLOCK = 256
ADA_COLS = 1152
CONV_PAD = 32
SCONV_PAD = 8


def _lambda_init(layer):
    return 0.8 - 0.6 * math.exp(-0.3 * layer)


def _params(sem):
    return pltpu.CompilerParams(dimension_semantics=sem, vmem_limit_bytes=VMEM_LIMIT_BYTES)


def _resident(shape):
    nd = len(shape)
    return pl.BlockSpec(shape, lambda *_: (0,) * nd, pipeline_mode=pl.Buffered(1))


def _dot(a, b):
    return jnp.dot(a, b, preferred_element_type=F32)


def _dot_nt(a, b):
    return lax.dot_general(a, b, (((1,), (1,)), ((), ())), preferred_element_type=F32)


def _dot_tn(a, b):
    return lax.dot_general(a, b, (((0,), (0,)), ((), ())), preferred_element_type=F32)


def _rms(x, g, eps=EPS):
    return x * lax.rsqrt(jnp.mean(x * x, axis=-1, keepdims=True) + eps) * g


def _silu(x):
    return x * jax.nn.sigmoid(x)


def _split3(a):
    hi = a.astype(BF16)
    r = a - hi.astype(F32)
    mid = r.astype(BF16)
    lo = (r - mid.astype(F32)).astype(BF16)
    return hi, mid, lo


def _ada_kernel(c_ref, w_ref, b_ref, o_ref):
    sc = _silu(c_ref[...]).astype(BF16)
    o_ref[...] = _dot(sc, w_ref[...].astype(BF16)) + b_ref[...]


def _ada_call(c_all, w_ada, b_ada):
    depth, d, n = w_ada.shape
    rows = c_all.shape[0]
    return pl.pallas_call(
        _ada_kernel,
        grid=(depth, n // ADA_COLS),
        in_specs=[pl.BlockSpec((rows, d), lambda l, j: (0, 0)),
                  pl.BlockSpec((None, d, ADA_COLS), lambda l, j: (l, 0, j)),
                  pl.BlockSpec((None, 1, ADA_COLS), lambda l, j: (l, 0, j))],
        out_specs=pl.BlockSpec((None, rows, ADA_COLS), lambda l, j: (l, 0, j)),
        out_shape=jax.ShapeDtypeStruct((depth, rows, n), F32),
        compiler_params=_params(("arbitrary", "arbitrary")),
        name="adaln",
    )(c_all, w_ada, b_ada.reshape(depth, 1, n))


def _ffn_kernel(*refs, with_mix, lam0):
    if with_mix:
        (x_ref, p_ref, g2_ref, o_ref, subg_ref, wbr_ref, wmix_ref, gate1_ref, gpost1_ref, *rest) = refs
    else:
        x_ref, *rest = refs
    shift_ref, scale_ref, gate_ref, gpre_ref, gpost_ref, win_ref, wout_ref, out_ref = rest
    x = x_ref[...]
    if with_mix:
        o = o_ref[...]
        subg = subg_ref[...]
        heads = []
        for h in range(ATT_HEADS):
            oh = o[:, h * ATT_V_DIM:(h + 1) * ATT_V_DIM]
            heads.append(_rms(oh, subg, SUBLN_EPS) * (1.0 - lam0))
        c_out = _dot(jnp.concatenate(heads, axis=1).astype(BF16), wbr_ref[...])
        merged = (p_ref[...] + g2_ref[...] * c_out).astype(BF16)
        x = x + gate1_ref[...] * _rms(_dot(merged, wmix_ref[...]), gpost1_ref[...])
    h = (_rms(x, gpre_ref[...]) * (1.0 + scale_ref[...]) + shift_ref[...]).astype(BF16)
    fc = D_FF // FFN_CHUNKS
    f = None
    for c in range(FFN_CHUNKS):
        up = _dot(h, win_ref[:, c * fc:(c + 1) * fc])
        gt = _dot(h, win_ref[:, D_FF + c * fc:D_FF + (c + 1) * fc])
        part = _dot((_silu(gt) * up).astype(BF16), wout_ref[c * fc:(c + 1) * fc, :])
        f = part if f is None else f + part
    out_ref[...] = x + 0.5 * gate_ref[...] * _rms(f, gpost_ref[...])


def _ffn_call(x, mod, gpre, gpost, w_in, w_out, mix=None, lam0=0.0):
    nseq, L, D = x.shape
    bm = min(FFN_ROWS, L)
    row_spec = pl.BlockSpec((None, bm, D), lambda s, i: (s, i, 0))

    def mod_spec(m):
        if m.shape[1] == 1:
            return pl.BlockSpec((None, 1, D), lambda s, i: (s, 0, 0))
        return row_spec

    args, specs = [x], [row_spec]
    if mix is not None:
        args += [mix["p"], mix["g2"], mix["o"], mix["subg"], mix["w_br_attn"], mix["w_mix_out"], mix["gate"], mix["gpost"]]
        specs += [row_spec, row_spec, pl.BlockSpec((None, bm, ATT_WIDTH), lambda s, i: (s, i, 0)),
                  _resident((1, ATT_V_DIM)), _resident((ATT_WIDTH, D)), _resident((D, D)),
                  mod_spec(mix["gate"]), _resident((1, D))]
    args += [mod[0], mod[1], mod[2], gpre, gpost, w_in, w_out]
    specs += [mod_spec(mod[0]), mod_spec(mod[1]), mod_spec(mod[2]), _resident((1, D)), _resident((1, D)),
              _resident(w_in.shape), _resident(w_out.shape)]
    return pl.pallas_call(
        functools.partial(_ffn_kernel, with_mix=mix is not None, lam0=lam0),
        grid=(nseq, L // bm),
        in_specs=specs,
        out_specs=row_spec,
        out_shape=jax.ShapeDtypeStruct(x.shape, F32),
        compiler_params=_params(("arbitrary", "arbitrary")),
        name="merge_ffn" if mix is not None else "ffn",
    )(*args)


def _dwconv(src_ref, first, w, b, rows, rc, shift_ref=None):
    width, ch = w.shape
    if shift_ref is not None:
        span = shift_ref.shape[1]
        for s in sorted({(first + j) % SUBLANES for j in range(width)} - {0}):
            shift_ref[s - 1] = src_ref[s:s + span, :]

    def tap(j, r0):
        a, s = divmod(first + j, SUBLANES)
        if shift_ref is None or s == 0:
            return src_ref[first + j + r0:first + j + r0 + rc, :]
        return shift_ref[s - 1, a * SUBLANES + r0:a * SUBLANES + r0 + rc, :]

    outs = []
    for r0 in range(0, rows, rc):
        acc = jnp.broadcast_to(b, (rc, ch))
        for j in range(width):
            acc = acc + w[j:j + 1, :] * tap(j, r0)
        outs.append(acc)
    return jnp.concatenate(outs, axis=0)


def _mix_kernel(*refs, bm, q, n_prev):
    (x_ref, shift_ref, scale_ref, gpre_ref,
     wglu_ref, wz_ref, wxbc_ref, wdtr_ref, wdtt_ref, wqkv_ref, wgate_ref, bgate_ref,
     cw_ref, cb_ref, lng_ref, lnb_ref, wbrc_ref,
     sw_ref, sb_ref, dtbr_ref, dtbc_ref, alr_ref, alc_ref, dr_ref, ng_ref, wbrs_ref,
     conv0_ref, sconv0_ref, ssd0_ref,
     p_ref, g2_ref, k_ref, v_ref, qb_ref, kb_ref, vb_ref, convn_ref, sconvn_ref, ssdn_ref,
     conv_s, sconv_s, state_s, shift_s) = refs[n_prev:]

    @pl.when(pl.program_id(1) == 0)
    def _():
        conv_s[0:CONV_PAD, :] = conv0_ref[...]
        sconv_s[0:SCONV_PAD, :] = sconv0_ref[...]
        state_s[...] = ssd0_ref[...].reshape(SSD_INNER, SSD_STATE).T

    h = (_rms(x_ref[...], gpre_ref[...]) * (1.0 + scale_ref[...]) + shift_ref[...]).astype(BF16)

    glu = _dot(h, wglu_ref[...])
    conv_s[CONV_PAD:CONV_PAD + bm, :] = glu[:, :CONV_CH] * jax.nn.sigmoid(glu[:, CONV_CH:])
    sconv_s[SCONV_PAD:SCONV_PAD + bm, :] = _dot(h, wxbc_ref[...])
    dt_raw = _dot(h, wdtr_ref[...])
    dt_raw_t = _dot_nt(wdtt_ref[...], h)
    z = _dot(h, wz_ref[...])
    gate_logits = _dot(h, wgate_ref[...])
    qkv = _dot(h, wqkv_ref[...])
    k = qkv[:, ATT_WIDTH:2 * ATT_WIDTH]
    v = qkv[:, 2 * ATT_WIDTH:]
    for hd in range(ATT_HEADS):
        k_ref[:, hd, :] = k[:, hd * ATT_V_DIM:(hd + 1) * ATT_V_DIM]
        v_ref[:, hd, :] = v[:, hd * ATT_V_DIM:(hd + 1) * ATT_V_DIM]
    qb_ref[...] = (qkv[:, :ATT_WIDTH] * (ATT_SCALE * LOG2E)).astype(BF16)
    kb_ref[...] = k.astype(BF16)
    vb_ref[...] = v.astype(BF16)

    a = _dwconv(conv_s, CONV_PAD - (CONV_WIDTH - 1), cw_ref[...], cb_ref[...], bm, 32, shift_s)
    mu = jnp.mean(a, axis=-1, keepdims=True)
    ac = a - mu
    a = ac * lax.rsqrt(jnp.mean(ac * ac, axis=-1, keepdims=True) + CONV_LN_EPS) * lng_ref[...] + lnb_ref[...]
    a_out = _dot(_silu(a).astype(BF16), wbrc_ref[...])
    convn_ref[...] = conv_s[CONV_PAD + bm - (CONV_WIDTH - 1):CONV_PAD + bm, :]
    conv_s[0:CONV_PAD, :] = conv_s[bm:bm + CONV_PAD, :]

    xbc = _silu(_dwconv(sconv_s, SCONV_PAD - (SSD_CONV_WIDTH - 1), sw_ref[...], sb_ref[...], bm, 16))
    sconvn_ref[...] = sconv_s[SCONV_PAD + bm - (SSD_CONV_WIDTH - 1):SCONV_PAD + bm, :]
    sconv_s[0:SCONV_PAD, :] = sconv_s[bm:bm + SCONV_PAD, :]

    dt_e = jax.nn.softplus(dt_raw + dtbr_ref[...])
    dt_t = jax.nn.softplus(dt_raw_t + dtbc_ref[...])
    a_rep = -jnp.exp(alr_ref[...])
    a_col = -jnp.exp(alc_ref[...])
    row = lax.broadcasted_iota(jnp.int32, (q, q), 0)
    col = lax.broadcasted_iota(jnp.int32, (q, q), 1)
    causal = row >= col
    tril = causal.astype(BF16)
    triu = (row <= col).astype(BF16)
    gw = SSD_INNER // SSD_GROUPS
    ys = []
    for c in range(bm // q):
        rows = slice(c * q, (c + 1) * q)
        dt_c = dt_e[rows]
        hi, mid, lo = _split3(dt_c * a_rep)
        acum = _dot(tril, hi) + _dot(tril, mid) + _dot(tril, lo)
        hi, mid, lo = _split3(dt_t[:, rows] * a_col)
        acum_t = _dot(hi, triu) + _dot(mid, triu) + _dot(lo, triu)
        xs_c = xbc[rows, :SSD_INNER]
        xdt = xs_c * dt_c
        a_last = acum[q - 1:q, :]
        x_end = (xdt * jnp.exp(a_last - acum)).astype(BF16)
        e_in = jnp.exp(acum)
        xdt_b = xdt.astype(BF16)
        state = state_s[...]
        state_b = state.astype(BF16)
        y_parts = []
        for g in range(SSD_GROUPS):
            bg = xbc[rows, SSD_INNER + g * SSD_STATE:SSD_INNER + (g + 1) * SSD_STATE]
            cg = xbc[rows, SSD_INNER + (SSD_GROUPS + g) * SSD_STATE:SSD_INNER + (SSD_GROUPS + g + 1) * SSD_STATE]
            bg_b, cg_b = bg.astype(BF16), cg.astype(BF16)
            cb = _dot_nt(cg_b, bg_b)
            y_off = _dot(cg_b, state_b[:, g * gw:(g + 1) * gw]) * e_in[:, g * gw:(g + 1) * gw]
            y_diag = []
            for r in range(SSD_HEADS // SSD_GROUPS):
                hd = g * (SSD_HEADS // SSD_GROUPS) + r
                lo_l = hd * SSD_HEAD_DIM
                seg = acum[:, lo_l:lo_l + q] - acum_t[hd:hd + 1, :]
                m = (cb * jnp.exp(jnp.where(causal, seg, NEG_BIG))).astype(BF16)
                y_diag.append(_dot(m, xdt_b[:, lo_l:lo_l + SSD_HEAD_DIM]))
            y_parts.append(jnp.concatenate(y_diag, axis=1) + y_off)
            state_s[:, g * gw:(g + 1) * gw] = (state[:, g * gw:(g + 1) * gw] * jnp.exp(a_last[:, g * gw:(g + 1) * gw])
                                               + _dot_tn(bg_b, x_end[:, g * gw:(g + 1) * gw]))
        ys.append(jnp.concatenate(y_parts, axis=1) + dr_ref[...] * xs_c)

    @pl.when(pl.program_id(1) == pl.num_programs(1) - 1)
    def _():
        ssdn_ref[...] = state_s[...].T.reshape(SSD_HEADS, SSD_HEAD_DIM, SSD_STATE)

    y = jnp.concatenate(ys, axis=0) if len(ys) > 1 else ys[0]
    yz = y * _silu(z)
    yz = jnp.concatenate([_rms(yz[:, g * gw:(g + 1) * gw], 1.0) for g in range(SSD_GROUPS)], axis=1) * ng_ref[...]
    b_out = _dot(yz.astype(BF16), wbrs_ref[...])

    gates = jax.nn.sigmoid(gate_logits + bgate_ref[...])
    p_ref[...] = gates[:, :D_MODEL] * a_out + gates[:, D_MODEL:2 * D_MODEL] * b_out
    g2_ref[...] = gates[:, 2 * D_MODEL:]


def _mix_call(x, shift, scale, gpre, w, conv0, sconv0, ssd0, layer, depth, kv_prev):
    nseq, L, D = x.shape
    bm = min(MIX_ROWS, L)
    q = min(CHUNK, bm)
    kv_shape = jax.ShapeDtypeStruct((depth, nseq, L, ATT_HEADS, ATT_V_DIM), F32)
    kv_spec = pl.BlockSpec((None, None, bm, ATT_HEADS, ATT_V_DIM), lambda s, i: (layer, s, i, 0, 0))
    prev = list(kv_prev or ())
    rows = lambda n: pl.BlockSpec((None, bm, n), lambda s, i: (s, i, 0))
    per_seq = lambda a: pl.BlockSpec((None,) + a.shape[1:], lambda s, i: (s,) + (0,) * (len(a.shape) - 1))
    weights = [gpre, w["w_glu"], w["w_z"], w["w_xbc"], w["w_dt_rep"], w["w_dt_t"], w["w_qkv"], w["w_gate"], w["b_gate"],
               w["conv_w"], w["conv_b"], w["ln_g"], w["ln_b"], w["w_br_conv"],
               w["sconv_w"], w["sconv_b"], w["dtb_rep"], w["dtb_col"], w["alog_rep"], w["alog_col"], w["d_rep"],
               w["norm_g"], w["w_br_ssd"]]
    out_shape = [jax.ShapeDtypeStruct((nseq, L, D), F32), jax.ShapeDtypeStruct((nseq, L, D), F32), kv_shape, kv_shape,
                 jax.ShapeDtypeStruct((nseq, L, ATT_WIDTH), BF16), jax.ShapeDtypeStruct((nseq, L, ATT_WIDTH), BF16),
                 jax.ShapeDtypeStruct((nseq, L, ATT_WIDTH), BF16),
                 jax.ShapeDtypeStruct((nseq, CONV_WIDTH - 1, CONV_CH), F32),
                 jax.ShapeDtypeStruct((nseq, SSD_CONV_WIDTH - 1, SSD_CONV_CH), F32),
                 jax.ShapeDtypeStruct((nseq, SSD_HEADS, SSD_HEAD_DIM, SSD_STATE), F32)]
    out_specs = [rows(D), rows(D), kv_spec, kv_spec, rows(ATT_WIDTH), rows(ATT_WIDTH), rows(ATT_WIDTH),
                 per_seq(out_shape[7]), per_seq(out_shape[8]), per_seq(out_shape[9])]
    return pl.pallas_call(
        functools.partial(_mix_kernel, bm=bm, q=q, n_prev=len(prev)),
        grid=(nseq, L // bm),
        in_specs=[pl.BlockSpec(memory_space=pl.ANY)] * len(prev)
                 + [rows(D), per_seq(shift), per_seq(scale)] + [_resident(a.shape) for a in weights]
                 + [per_seq(conv0), per_seq(sconv0), per_seq(ssd0)],
        input_output_aliases={i: 2 + i for i in range(len(prev))},
        out_specs=out_specs,
        out_shape=out_shape,
        scratch_shapes=[pltpu.VMEM((CONV_PAD + bm, CONV_CH), F32), pltpu.VMEM((SCONV_PAD + bm, SSD_CONV_CH), F32),
                        pltpu.VMEM((SSD_STATE, SSD_INNER), F32),
                        pltpu.VMEM((SUBLANES - 1, CONV_PAD + bm - SUBLANES, CONV_CH), F32)],
        compiler_params=_params(("arbitrary", "arbitrary")),
        name="mix_front",
    )(*prev, x, shift, scale, *weights, conv0, sconv0, ssd0)


def _lambda(lq_ref, lk_ref, lam0):
    lq, lk = lq_ref[...], lk_ref[...]
    e0 = jnp.exp(jnp.sum(lq[0:1] * lk[0:1], axis=-1, keepdims=True))
    e1 = jnp.exp(jnp.sum(lq[1:2] * lk[1:2], axis=-1, keepdims=True))
    return e0 - e1 + lam0


def _attn_prompt_kernel(q_ref, k_ref, vt_ref, ext_ref, corr_ref, lq_ref, lk_ref, o_ref,
                        qa_s, acc_s, m_s, kn_s, sa_s, sb_s, *, lam0):
    blk = ATT_BLOCK
    qi = pl.program_id(1)
    lane = lax.broadcasted_iota(jnp.int32, (blk, LANES), 1)
    first = lane < ATT_QK_DIM
    ones_from = lambda c: jnp.where((lane >= c) & (lane < c + BIAS_TERMS), 1.0, 0.0).astype(BF16)
    head_cols = lambda h: slice(h * ATT_V_DIM, (h + 1) * ATT_V_DIM)
    qk_shift = ATT_QK_DIM.bit_length() - 1
    sel = (jnp.right_shift(lax.broadcasted_iota(jnp.int32, (ATT_WIDTH, LANES), 0), qk_shift)
           == lax.broadcasted_iota(jnp.int32, (ATT_WIDTH, LANES), 1)).astype(BF16)

    def max_sq_norm(x):
        xf = x.astype(F32)
        return jnp.max(_dot((xf * xf).astype(BF16), sel), axis=0, keepdims=True)

    @pl.when(qi == 0)
    def _():
        def body(t, mx):
            return jnp.maximum(mx, max_sq_norm(k_ref[pl.ds(pl.multiple_of(t * blk, blk), blk), :]))
        kn_s[0:1, :] = lax.fori_loop(0, k_ref.shape[0] // blk, body, jnp.zeros((1, LANES), F32))

    q = q_ref[...]
    for h in range(ATT_HEADS):
        qa_s[2 * h] = jnp.where(first, q[:, head_cols(h)], ones_from(ATT_QK_DIM))
        qa_s[2 * h + 1] = jnp.where(first, ones_from(0), q[:, head_cols(h)])
    acc_s[...] = jnp.zeros_like(acc_s)
    m_s[...] = jnp.full(m_s.shape, NEG_BIG, F32)
    ones_rows = jnp.ones((ATT_ONES_ROWS, blk), BF16)

    slot_of = lambda pos, t, c, nt: 2 * (pos * nt + t) + c

    def score_jobs(kjs, heads, s_ref):
        def job(pos, h, t, c):
            kh = k_ref[pl.ds(pl.multiple_of(kjs[t] * blk, blk), blk), head_cols(h)]
            ka = jnp.where(first, kh, ext_ref[h, 0]) if c == 0 else jnp.where(first, ext_ref[h, 1], kh)
            s_ref[slot_of(pos, t, c, len(kjs))] = _dot_nt(ka, qa_s[2 * h + c])
        return [functools.partial(job, pos, h, t, c)
                for pos, h in enumerate(heads) for c in range(2) for t in range(len(kjs))]

    def fold_jobs(kjs, heads, s_ref, diagonal=False):
        k0s = [pl.multiple_of(kj * blk, blk) for kj in kjs]
        nt = len(k0s)

        def job(pos, h, c):
            bases = [(ALIBI_SLOPES[h] * LOG2E) * k0.astype(F32) for k0 in k0s]
            vt = jnp.concatenate([vt_ref[head_cols(h), pl.ds(k0, blk)] for k0 in k0s], axis=1)
            vt = jnp.concatenate([vt, jnp.concatenate([ones_rows] * nt, axis=1)], axis=0)
            sc = [s_ref[slot_of(pos, t, c, nt)] for t in range(nt)]
            if diagonal:
                sc = [x + corr_ref[h] for x in sc]
            m_old = m_s[2 * h + c:2 * h + c + 1, :]
            m_new = m_old
            for t in range(nt):
                m_new = jnp.maximum(m_new, jnp.max(sc[t], axis=0, keepdims=True) + bases[t])
            m_s[2 * h + c:2 * h + c + 1, :] = m_new
            p = jnp.concatenate([jnp.exp2(sc[t] - (m_new - bases[t])).astype(BF16) for t in range(nt)], axis=0)
            acc_s[2 * h + c] = jnp.exp2(m_old - m_new) * acc_s[2 * h + c] + _dot(vt, p)
        return [functools.partial(job, pos, h, c) for pos, h in enumerate(heads) for c in range(2)]

    def run(jobs):
        for job in jobs:
            job()

    run(score_jobs([qi], range(ATT_HEADS), sa_s))
    run(fold_jobs([qi], range(ATT_HEADS), sa_s, diagonal=True))

    bound = jnp.sqrt(max_sq_norm(q)) * jnp.sqrt(kn_s[0:1, :]) * NORM_SLACK
    lo, limit = [], qi
    for h in range(ATT_HEADS):
        gap = None
        for c in range(2):
            g = bound[:, 2 * h + c:2 * h + c + 1] - jnp.min(m_s[2 * h + c:2 * h + c + 1, :], axis=1, keepdims=True)
            gap = g if gap is None else jnp.maximum(gap, g)
        x = (-ATT_SKIP_LOG2 - gap) / (ALIBI_SLOPES[h] * LOG2E)
        need = jnp.ceil((x - (blk - 1)) / blk)
        need = jnp.clip(jnp.where(need == need, need, 0.0), 0.0, qi.astype(F32))
        limit = jnp.minimum(limit, need.astype(jnp.int32)[0, 0])
        lo.append(limit)

    hi = qi
    widths = [w for _, w in ATT_PHASES]
    for i, (heads, width) in enumerate(ATT_PHASES):
        align = max(width, widths[min(i + 1, len(widths) - 1)]).bit_length() - 1
        lo_i = jnp.left_shift(jnp.right_shift(lo[heads[0]], align), align)

        n_i = jnp.right_shift(hi - lo_i, width.bit_length() - 1)
        tiles = lambda j, hi=hi, width=width: [jnp.maximum(hi - (j + 1) * width + t, 0) for t in range(width)]

        @pl.when(n_i > 0)
        def _(tiles=tiles, heads=heads):
            run(score_jobs(tiles(0), heads, sa_s))

        def body(j, carry, tiles=tiles, heads=heads):
            def trip(cur, nxt):
                ahead, folds = score_jobs(tiles(j + 1), heads, nxt), fold_jobs(tiles(j), heads, cur)
                per = len(ahead) // len(folds)
                for f, job in enumerate(folds):
                    run(ahead[f * per:(f + 1) * per])
                    job()
            lax.cond(jnp.bitwise_and(j, 1) == 0, lambda: trip(sa_s, sb_s), lambda: trip(sb_s, sa_s))
            return carry

        lax.fori_loop(0, n_i, body, 0)
        hi = lo_i

    lam = _lambda(lq_ref, lk_ref, lam0)
    for h in range(ATT_HEADS):
        a1, a2 = acc_s[2 * h], acc_s[2 * h + 1]
        o_t = (a1[:ATT_V_DIM] / a1[ATT_V_DIM:ATT_V_DIM + 1]
               - lam * (a2[:ATT_V_DIM] / a2[ATT_V_DIM:ATT_V_DIM + 1]))
        o_ref[:, head_cols(h)] = o_t.T


def _attn_tables():
    blk = ATT_BLOCK
    pos = np.arange(blk, dtype=np.float64)
    ext = np.zeros((ATT_HEADS, 2, blk, LANES), np.float32)
    corr = np.zeros((ATT_HEADS, blk, blk), np.float32)
    kpos, qpos = pos[:, None], pos[None, :]
    visible = (kpos // CHUNK) <= (qpos // CHUNK)
    to_bf16 = lambda x: x.astype(ml_dtypes.bfloat16).astype(np.float64)
    for h, slope in enumerate(ALIBI_SLOPES):
        rest = slope * LOG2E * pos
        for t in range(BIAS_TERMS):
            term = to_bf16(rest)
            ext[h, 0, :, ATT_QK_DIM + t] = term
            ext[h, 1, :, t] = term
            rest = rest - term
        corr[h] = np.where(visible, -2.0 * slope * LOG2E * np.maximum(kpos - qpos, 0.0), NEG_BIG)
    return jnp.asarray(ext, BF16), jnp.asarray(corr, F32)


def _attn_prompt_call(qb, kb, vb, lq, lk, lam0):
    b, S, W = qb.shape
    blk = ATT_BLOCK
    vt = jnp.swapaxes(vb, 1, 2)
    ext, corr = _attn_tables()
    whole = lambda shape: pl.BlockSpec((None,) + shape, lambda s, i: (s, 0, 0), pipeline_mode=pl.Buffered(1))
    return pl.pallas_call(
        functools.partial(_attn_prompt_kernel, lam0=lam0),
        grid=(b, S // blk),
        in_specs=[pl.BlockSpec((None, blk, W), lambda s, i: (s, i, 0)), whole((S, W)), whole((W, S)),
                  _resident(ext.shape), _resident(corr.shape), _resident(lq.shape), _resident(lk.shape)],
        out_specs=pl.BlockSpec((None, blk, W), lambda s, i: (s, i, 0)),
        out_shape=jax.ShapeDtypeStruct((b, S, W), F32),
        scratch_shapes=[pltpu.VMEM((2 * ATT_HEADS, blk, LANES), BF16),
                        pltpu.VMEM((2 * ATT_HEADS, ATT_V_DIM + ATT_ONES_ROWS, blk), F32),
                        pltpu.VMEM((2 * ATT_HEADS, blk), F32), pltpu.VMEM((SUBLANES, LANES), F32),
                        pltpu.VMEM((2 * ATT_HEADS, blk, blk), F32), pltpu.VMEM((2 * ATT_HEADS, blk, blk), F32)],
        compiler_params=_params(("arbitrary", "arbitrary")),
        name="attn_prompt",
    )(qb, kb, vt, ext, corr, lq, lk)


def _attn_sample_kernel(q_ref, kn_ref, vn_ref, kc_ref, vc_ref, lq_ref, lk_ref, o_ref, *, lam0):
    ls, past = q_ref.shape[0], kc_ref.shape[0]
    lam = _lambda(lq_ref, lk_ref, lam0)
    lane = lax.broadcasted_iota(jnp.int32, (ls, LANES), 1)
    first = lane < ATT_QK_DIM

    def bias_mask(nk, k_first):
        qpos = past + lax.broadcasted_iota(jnp.int32, (ls, nk), 0)
        kpos = k_first + lax.broadcasted_iota(jnp.int32, (ls, nk), 1)
        dist = jnp.abs(qpos - kpos).astype(F32)
        visible = jnp.right_shift(kpos, CHUNK_SHIFT) <= jnp.right_shift(qpos, CHUNK_SHIFT)
        return dist, visible

    dist_c, vis_c = bias_mask(past, 0)
    dist_n, vis_n = bias_mask(ls, past)
    outs = []
    for h in range(ATT_HEADS):
        cols = slice(h * ATT_V_DIM, (h + 1) * ATT_V_DIM)
        qh = q_ref[:, cols]
        kc, kn = kc_ref[:, h, :].astype(BF16), kn_ref[:, cols]
        vc, vn = vc_ref[:, h, :].astype(BF16), vn_ref[:, cols]
        o = []
        for c in range(2):
            qm = jnp.where(first == (c == 0), qh, jnp.zeros_like(qh))
            sc = jnp.where(vis_c, _dot_nt(qm, kc) - (ALIBI_SLOPES[h] * LOG2E) * dist_c, NEG_BIG)
            sn = jnp.where(vis_n, _dot_nt(qm, kn) - (ALIBI_SLOPES[h] * LOG2E) * dist_n, NEG_BIG)
            m = jnp.maximum(jnp.max(sc, axis=-1, keepdims=True), jnp.max(sn, axis=-1, keepdims=True))
            pc, pn = jnp.exp2(sc - m), jnp.exp2(sn - m)
            l = jnp.sum(pc, axis=-1, keepdims=True) + jnp.sum(pn, axis=-1, keepdims=True)
            o.append((_dot(pc.astype(BF16), vc) + _dot(pn.astype(BF16), vn)) / l)
        outs.append(o[0] - lam * o[1])
    o_ref[...] = jnp.concatenate(outs, axis=1)


def _attn_sample_call(qb, kb, vb, cache_k, cache_v, layer, lq, lk, lam0):
    b, ls, W = qb.shape
    past = cache_k.shape[2]
    new = pl.BlockSpec((None, ls, W), lambda s: (s, 0, 0))
    old = pl.BlockSpec((None, None, past, ATT_HEADS, ATT_V_DIM), lambda s: (layer, s, 0, 0, 0))
    return pl.pallas_call(
        functools.partial(_attn_sample_kernel, lam0=lam0),
        grid=(b,),
        in_specs=[new, new, new, old, old, _resident(lq.shape), _resident(lk.shape)],
        out_specs=new,
        out_shape=jax.ShapeDtypeStruct((b, ls, W), F32),
        compiler_params=_params(("arbitrary",)),
        name="attn_sample",
    )(qb, kb, vb, cache_k, cache_v, lq, lk)


def _prep_layer(l, w_in, b_gate, conv_dw_w, conv_dw_b, conv_ln_g, conv_ln_b, w_br_conv, ssd_conv_w, ssd_conv_b,
                ssd_dt_bias, ssd_A_log, ssd_D, ssd_norm_g, w_br_ssd):
    splits = (2 * CONV_CH, SSD_INNER, SSD_CONV_CH, SSD_HEADS, 3 * ATT_WIDTH, N_BRANCH * D_MODEL)
    pts = np.cumsum(splits)[:-1].tolist()
    w_glu, w_z, w_xbc, w_dt, w_qkv, w_gate = jnp.split(w_in[l].astype(BF16), pts, axis=1)
    rep = lambda a: jnp.repeat(a, SSD_HEAD_DIM, axis=-1)
    return dict(
        w_glu=w_glu, w_z=w_z, w_xbc=w_xbc, w_dt_rep=rep(w_dt), w_dt_t=w_dt.T, w_qkv=w_qkv, w_gate=w_gate,
        b_gate=b_gate[l].reshape(1, N_BRANCH * D_MODEL),
        conv_w=conv_dw_w[l], conv_b=conv_dw_b[l][None], ln_g=conv_ln_g[l][None], ln_b=conv_ln_b[l][None],
        w_br_conv=w_br_conv[l].astype(BF16),
        sconv_w=ssd_conv_w[l], sconv_b=ssd_conv_b[l][None],
        dtb_rep=rep(ssd_dt_bias[l])[None], dtb_col=ssd_dt_bias[l][:, None],
        alog_rep=rep(ssd_A_log[l])[None], alog_col=ssd_A_log[l][:, None],
        d_rep=rep(ssd_D[l])[None], norm_g=ssd_norm_g[l][None], w_br_ssd=w_br_ssd[l].astype(BF16))


def _state_in(conv, sconv, ssd):
    conv = jnp.pad(conv, ((0, 0), (CONV_PAD - (CONV_WIDTH - 1), 0), (0, 0)))
    sconv = jnp.pad(sconv, ((0, 0), (SCONV_PAD - (SSD_CONV_WIDTH - 1), 0), (0, 0)))
    return conv, sconv, ssd


def kernel(x_prompt, x_sample, cache_attn_k, cache_attn_v, state_conv, state_ssd_conv, state_ssd, c_prompt, c_sample,
           w_ada, b_ada, norm_pre, norm_post, w_ffn_in, w_ffn_out, w_in, b_gate, conv_dw_w, conv_dw_b, conv_ln_g,
           conv_ln_b, w_br_conv, ssd_conv_w, ssd_conv_b, ssd_dt_bias, ssd_A_log, ssd_D, ssd_norm_g, w_br_ssd,
           lambda_q, lambda_k, attn_subln_g, w_br_attn, w_mix_out):
    depth = w_ada.shape[0]
    bp, S, D = x_prompt.shape
    bs, ls, _ = x_sample.shape

    mods = _ada_call(jnp.concatenate([c_prompt, c_sample], axis=0), w_ada, b_ada)
    mods = mods.reshape(depth, bp + bs, N_SUB, 3, D)

    zeros = _state_in(jnp.zeros((bp, CONV_WIDTH - 1, CONV_CH), F32), jnp.zeros((bp, SSD_CONV_WIDTH - 1, SSD_CONV_CH), F32),
                      jnp.zeros((bp, SSD_HEADS, SSD_HEAD_DIM, SSD_STATE), F32))
    xp = x_prompt
    xs = x_sample.reshape(1, bs * ls, D)
    st_p, st_s = [], []
    kv_p = kv_s = None
    for l in range(depth):
        lam0 = _lambda_init(l)
        w = _prep_layer(l, w_in, b_gate, conv_dw_w, conv_dw_b, conv_ln_g, conv_ln_b, w_br_conv, ssd_conv_w, ssd_conv_b,
                        ssd_dt_bias, ssd_A_log, ssd_D, ssd_norm_g, w_br_ssd)
        wf_in, wf_out = w_ffn_in[l].astype(BF16), w_ffn_out[l].astype(BF16)
        w_bra, w_mix = w_br_attn[l].astype(BF16), w_mix_out[l].astype(BF16)
        gpre = lambda s: norm_pre[l, s][None]
        gpost = lambda s: norm_post[l, s][None]
        subg = attn_subln_g[l][None]
        mp = lambda s, k: mods[l, :bp, s, k][:, None, :]
        ms_seq = lambda s, k: mods[l, bp:, s, k][:, None, :]
        ms_tok = lambda s, k: jnp.repeat(mods[l, bp:, s, k], ls, axis=0)[None]

        xp = _ffn_call(xp, (mp(0, 0), mp(0, 1), mp(0, 2)), gpre(0), gpost(0), wf_in[0], wf_out[0])
        p, g2, *kv_p, qb, kb, vb, convn, sconvn, ssdn = _mix_call(xp, mp(1, 0), mp(1, 1), gpre(1), w, *zeros,
                                                                  l, depth, kv_p)
        o = _attn_prompt_call(qb, kb, vb, lambda_q[l], lambda_k[l], lam0)
        mix = dict(p=p, g2=g2, o=o, subg=subg, w_br_attn=w_bra, w_mix_out=w_mix, gate=mp(1, 2), gpost=gpost(1))
        xp = _ffn_call(xp, (mp(2, 0), mp(2, 1), mp(2, 2)), gpre(2), gpost(2), wf_in[1], wf_out[1], mix=mix, lam0=lam0)
        st_p.append((convn, sconvn, ssdn))

        xs = _ffn_call(xs, (ms_tok(0, 0), ms_tok(0, 1), ms_tok(0, 2)), gpre(0), gpost(0), wf_in[0], wf_out[0])
        states = _state_in(state_conv[l], state_ssd_conv[l], state_ssd[l])
        p, g2, *kv_s, qb, kb, vb, convn, sconvn, ssdn = _mix_call(xs.reshape(bs, ls, D), ms_seq(1, 0), ms_seq(1, 1),
                                                                  gpre(1), w, *states, l, depth, kv_s)
        o = _attn_sample_call(qb, kb, vb, cache_attn_k, cache_attn_v, l, lambda_q[l], lambda_k[l], lam0)
        flat = lambda t: t.reshape(1, bs * ls, t.shape[-1])
        mix = dict(p=flat(p), g2=flat(g2), o=flat(o), subg=subg, w_br_attn=w_bra, w_mix_out=w_mix, gate=ms_tok(1, 2),
                   gpost=gpost(1))
        xs = _ffn_call(xs, (ms_tok(2, 0), ms_tok(2, 1), ms_tok(2, 2)), gpre(2), gpost(2), wf_in[1], wf_out[1], mix=mix,
                       lam0=lam0)
        st_s.append((convn, sconvn, ssdn))

    stack = lambda st, i: jnp.stack([s[i] for s in st])
    return (xp, xs.reshape(bs, ls, D),
            kv_p[0], kv_p[1], stack(st_p, 0), stack(st_p, 1), stack(st_p, 2),
            kv_s[0], kv_s[1], stack(st_s, 0), stack(st_s, 1), stack(st_s, 2))
```

```python
import functools
import math

import ml_dtypes
import numpy as np
import jax
import jax.numpy as jnp
from jax import lax
from jax.experimental import pallas as pl
from jax.experimental.pallas import tpu as pltpu

F32, BF16 = jnp.float32, jnp.bfloat16

D_MODEL = 1024
D_FF = 2816
CONV_CH = 512
CONV_WIDTH = 31
SSD_HEADS = 16
SSD_HEAD_DIM = 64
SSD_INNER = SSD_HEADS * SSD_HEAD_DIM
SSD_GROUPS = 2
SSD_STATE = 128
SSD_CONV_WIDTH = 4
SSD_CONV_CH = SSD_INNER + 2 * SSD_GROUPS * SSD_STATE
CHUNK = 64
CHUNK_SHIFT = 6
ATT_HEADS = 4
ATT_QK_DIM = 64
ATT_V_DIM = 2 * ATT_QK_DIM
ATT_WIDTH = ATT_HEADS * ATT_V_DIM
ATT_SCALE = ATT_QK_DIM ** -0.5
ALIBI_SLOPES = tuple(2.0 ** (-8.0 * (h + 1) / ATT_HEADS) for h in range(ATT_HEADS))
N_BRANCH = 3
N_SUB = 3
EPS = 1e-6
SUBLN_EPS = 1e-5
CONV_LN_EPS = 1e-5
NEG_BIG = -1e30
LOG2E = math.log2(math.e)
BIAS_TERMS = 3
ATT_ONES_ROWS = 16
ATT_SKIP_LOG2 = 160.0
NORM_SLACK = 1.0 + 2.0 ** -6
ATT_PHASES = (((0, 1, 2, 3), 1), ((1, 2, 3), 1), ((2, 3), 2), ((3,), 4))
assert all(len(heads) * width <= ATT_HEADS for heads, width in ATT_PHASES)

VMEM_LIMIT_BYTES = 58 * 1024 * 1024
LANES = 128
SUBLANES = 8

FFN_ROWS = 512
FFN_CHUNKS = 2
MIX_ROWS = 256
ATT_BLOCK = 256
ATT_QBLOCK = 512
ADA_COLS = 1152
CONV_PAD = 32
SCONV_PAD = 8


def _lambda_init(layer):
    return 0.8 - 0.6 * math.exp(-0.3 * layer)


def _params(sem):
    return pltpu.CompilerParams(dimension_semantics=sem, vmem_limit_bytes=VMEM_LIMIT_BYTES)


def _resident(shape):
    nd = len(shape)
    return pl.BlockSpec(shape, lambda *_: (0,) * nd, pipeline_mode=pl.Buffered(1))


def _dot(a, b):
    return jnp.dot(a, b, preferred_element_type=F32)


def _dot_nt(a, b):
    return lax.dot_general(a, b, (((1,), (1,)), ((), ())), preferred_element_type=F32)


def _dot_tn(a, b):
    return lax.dot_general(a, b, (((0,), (0,)), ((), ())), preferred_element_type=F32)


def _rms(x, g, eps=EPS):
    return x * lax.rsqrt(jnp.mean(x * x, axis=-1, keepdims=True) + eps) * g


def _silu(x):
    return x * jax.nn.sigmoid(x)


def _split3(a):
    hi = a.astype(BF16)
    r = a - hi.astype(F32)
    mid = r.astype(BF16)
    lo = (r - mid.astype(F32)).astype(BF16)
    return hi, mid, lo


def _ada_kernel(c_ref, w_ref, b_ref, o_ref):
    sc = _silu(c_ref[...]).astype(BF16)
    o_ref[...] = _dot(sc, w_ref[...].astype(BF16)) + b_ref[...]


def _ada_call(c_all, w_ada, b_ada):
    depth, d, n = w_ada.shape
    rows = c_all.shape[0]
    return pl.pallas_call(
        _ada_kernel,
        grid=(depth, n // ADA_COLS),
        in_specs=[pl.BlockSpec((rows, d), lambda l, j: (0, 0)),
                  pl.BlockSpec((None, d, ADA_COLS), lambda l, j: (l, 0, j)),
                  pl.BlockSpec((None, 1, ADA_COLS), lambda l, j: (l, 0, j))],
        out_specs=pl.BlockSpec((None, rows, ADA_COLS), lambda l, j: (l, 0, j)),
        out_shape=jax.ShapeDtypeStruct((depth, rows, n), F32),
        compiler_params=_params(("arbitrary", "arbitrary")),
        name="adaln",
    )(c_all, w_ada, b_ada.reshape(depth, 1, n))


def _ffn_kernel(*refs, with_mix, lam0):
    if with_mix:
        (x_ref, p_ref, g2_ref, o_ref, subg_ref, wbr_ref, wmix_ref, gate1_ref, gpost1_ref, *rest) = refs
    else:
        x_ref, *rest = refs
    shift_ref, scale_ref, gate_ref, gpre_ref, gpost_ref, win_ref, wout_ref, out_ref = rest
    x = x_ref[...]
    if with_mix:
        o = o_ref[...]
        subg = subg_ref[...]
        heads = []
        for h in range(ATT_HEADS):
            oh = o[:, h * ATT_V_DIM:(h + 1) * ATT_V_DIM]
            heads.append(_rms(oh, subg, SUBLN_EPS) * (1.0 - lam0))
        c_out = _dot(jnp.concatenate(heads, axis=1).astype(BF16), wbr_ref[...])
        merged = (p_ref[...] + g2_ref[...] * c_out).astype(BF16)
        x = x + gate1_ref[...] * _rms(_dot(merged, wmix_ref[...]), gpost1_ref[...])
    h = (_rms(x, gpre_ref[...]) * (1.0 + scale_ref[...]) + shift_ref[...]).astype(BF16)
    fc = D_FF // FFN_CHUNKS
    f = None
    for c in range(FFN_CHUNKS):
        up = _dot(h, win_ref[:, c * fc:(c + 1) * fc])
        gt = _dot(h, win_ref[:, D_FF + c * fc:D_FF + (c + 1) * fc])
        part = _dot((_silu(gt) * up).astype(BF16), wout_ref[c * fc:(c + 1) * fc, :])
        f = part if f is None else f + part
    out_ref[...] = x + 0.5 * gate_ref[...] * _rms(f, gpost_ref[...])


def _ffn_call(x, mod, gpre, gpost, w_in, w_out, mix=None, lam0=0.0):
    nseq, L, D = x.shape
    bm = min(FFN_ROWS, L)
    row_spec = pl.BlockSpec((None, bm, D), lambda s, i: (s, i, 0))

    def mod_spec(m):
        if m.shape[1] == 1:
            return pl.BlockSpec((None, 1, D), lambda s, i: (s, 0, 0))
        return row_spec

    args, specs = [x], [row_spec]
    if mix is not None:
        args += [mix["p"], mix["g2"], mix["o"], mix["subg"], mix["w_br_attn"], mix["w_mix_out"], mix["gate"], mix["gpost"]]
        specs += [row_spec, row_spec, pl.BlockSpec((None, bm, ATT_WIDTH), lambda s, i: (s, i, 0)),
                  _resident((1, ATT_V_DIM)), _resident((ATT_WIDTH, D)), _resident((D, D)),
                  mod_spec(mix["gate"]), _resident((1, D))]
    args += [mod[0], mod[1], mod[2], gpre, gpost, w_in, w_out]
    specs += [mod_spec(mod[0]), mod_spec(mod[1]), mod_spec(mod[2]), _resident((1, D)), _resident((1, D)),
              _resident(w_in.shape), _resident(w_out.shape)]
    return pl.pallas_call(
        functools.partial(_ffn_kernel, with_mix=mix is not None, lam0=lam0),
        grid=(nseq, L // bm),
        in_specs=specs,
        out_specs=row_spec,
        out_shape=jax.ShapeDtypeStruct(x.shape, F32),
        compiler_params=_params(("arbitrary", "arbitrary")),
        name="merge_ffn" if mix is not None else "ffn",
    )(*args)


def _dwconv(src_ref, first, w, b, rows, rc, shift_ref=None):
    width, ch = w.shape
    if shift_ref is not None:
        span = shift_ref.shape[1]
        for s in sorted({(first + j) % SUBLANES for j in range(width)} - {0}):
            shift_ref[s - 1] = src_ref[s:s + span, :]

    def tap(j, r0):
        a, s = divmod(first + j, SUBLANES)
        if shift_ref is None or s == 0:
            return src_ref[first + j + r0:first + j + r0 + rc, :]
        return shift_ref[s - 1, a * SUBLANES + r0:a * SUBLANES + r0 + rc, :]

    outs = []
    for r0 in range(0, rows, rc):
        acc = jnp.broadcast_to(b, (rc, ch))
        for j in range(width):
            acc = acc + w[j:j + 1, :] * tap(j, r0)
        outs.append(acc)
    return jnp.concatenate(outs, axis=0)


def _mix_kernel(*refs, bm, q, n_prev):
    (x_ref, shift_ref, scale_ref, gpre_ref,
     wglu_ref, wz_ref, wxbc_ref, wdtr_ref, wdtt_ref, wqkv_ref, wgate_ref, bgate_ref,
     cw_ref, cb_ref, lng_ref, lnb_ref, wbrc_ref,
     sw_ref, sb_ref, dtbr_ref, dtbc_ref, alr_ref, alc_ref, dr_ref, ng_ref, wbrs_ref,
     conv0_ref, sconv0_ref, ssd0_ref,
     p_ref, g2_ref, k_ref, v_ref, qb_ref, kb_ref, vb_ref, convn_ref, sconvn_ref, ssdn_ref,
     conv_s, sconv_s, state_s, shift_s) = refs[n_prev:]

    @pl.when(pl.program_id(1) == 0)
    def _():
        conv_s[0:CONV_PAD, :] = conv0_ref[...]
        sconv_s[0:SCONV_PAD, :] = sconv0_ref[...]
        state_s[...] = ssd0_ref[...].reshape(SSD_INNER, SSD_STATE).T

    h = (_rms(x_ref[...], gpre_ref[...]) * (1.0 + scale_ref[...]) + shift_ref[...]).astype(BF16)

    glu = _dot(h, wglu_ref[...])
    conv_s[CONV_PAD:CONV_PAD + bm, :] = glu[:, :CONV_CH] * jax.nn.sigmoid(glu[:, CONV_CH:])
    sconv_s[SCONV_PAD:SCONV_PAD + bm, :] = _dot(h, wxbc_ref[...])
    dt_raw = _dot(h, wdtr_ref[...])
    dt_raw_t = _dot_nt(wdtt_ref[...], h)
    z = _dot(h, wz_ref[...])
    gate_logits = _dot(h, wgate_ref[...])
    qkv = _dot(h, wqkv_ref[...])
    k = qkv[:, ATT_WIDTH:2 * ATT_WIDTH]
    v = qkv[:, 2 * ATT_WIDTH:]
    for hd in range(ATT_HEADS):
        k_ref[:, hd, :] = k[:, hd * ATT_V_DIM:(hd + 1) * ATT_V_DIM]
        v_ref[:, hd, :] = v[:, hd * ATT_V_DIM:(hd + 1) * ATT_V_DIM]
    qb_ref[...] = (qkv[:, :ATT_WIDTH] * (ATT_SCALE * LOG2E)).astype(BF16)
    kb_ref[...] = k.astype(BF16)
    vb_ref[...] = v.astype(BF16)

    a = _dwconv(conv_s, CONV_PAD - (CONV_WIDTH - 1), cw_ref[...], cb_ref[...], bm, 32, shift_s)
    mu = jnp.mean(a, axis=-1, keepdims=True)
    ac = a - mu
    a = ac * lax.rsqrt(jnp.mean(ac * ac, axis=-1, keepdims=True) + CONV_LN_EPS) * lng_ref[...] + lnb_ref[...]
    a_out = _dot(_silu(a).astype(BF16), wbrc_ref[...])
    convn_ref[...] = conv_s[CONV_PAD + bm - (CONV_WIDTH - 1):CONV_PAD + bm, :]
    conv_s[0:CONV_PAD, :] = conv_s[bm:bm + CONV_PAD, :]

    xbc = _silu(_dwconv(sconv_s, SCONV_PAD - (SSD_CONV_WIDTH - 1), sw_ref[...], sb_ref[...], bm, 16))
    sconvn_ref[...] = sconv_s[SCONV_PAD + bm - (SSD_CONV_WIDTH - 1):SCONV_PAD + bm, :]
    sconv_s[0:SCONV_PAD, :] = sconv_s[bm:bm + SCONV_PAD, :]

    dt_e = jax.nn.softplus(dt_raw + dtbr_ref[...])
    dt_t = jax.nn.softplus(dt_raw_t + dtbc_ref[...])
    a_rep = -jnp.exp(alr_ref[...])
    a_col = -jnp.exp(alc_ref[...])
    row = lax.broadcasted_iota(jnp.int32, (q, q), 0)
    col = lax.broadcasted_iota(jnp.int32, (q, q), 1)
    causal = row >= col
    tril = causal.astype(BF16)
    triu = (row <= col).astype(BF16)
    gw = SSD_INNER // SSD_GROUPS
    ys = []
    for c in range(bm // q):
        rows = slice(c * q, (c + 1) * q)
        dt_c = dt_e[rows]
        hi, mid, lo = _split3(dt_c * a_rep)
        acum = _dot(tril, hi) + _dot(tril, mid) + _dot(tril, lo)
        hi, mid, lo = _split3(dt_t[:, rows] * a_col)
        acum_t = _dot(hi, triu) + _dot(mid, triu) + _dot(lo, triu)
        xs_c = xbc[rows, :SSD_INNER]
        xdt = xs_c * dt_c
        a_last = acum[q - 1:q, :]
        x_end = (xdt * jnp.exp(a_last - acum)).astype(BF16)
        e_in = jnp.exp(acum)
        xdt_b = xdt.astype(BF16)
        state = state_s[...]
        state_b = state.astype(BF16)
        y_parts = []
        for g in range(SSD_GROUPS):
            bg = xbc[rows, SSD_INNER + g * SSD_STATE:SSD_INNER + (g + 1) * SSD_STATE]
            cg = xbc[rows, SSD_INNER + (SSD_GROUPS + g) * SSD_STATE:SSD_INNER + (SSD_GROUPS + g + 1) * SSD_STATE]
            bg_b, cg_b = bg.astype(BF16), cg.astype(BF16)
            cb = _dot_nt(cg_b, bg_b)
            y_off = _dot(cg_b, state_b[:, g * gw:(g + 1) * gw]) * e_in[:, g * gw:(g + 1) * gw]
            y_diag = []
            for r in range(SSD_HEADS // SSD_GROUPS):
                hd = g * (SSD_HEADS // SSD_GROUPS) + r
                lo_l = hd * SSD_HEAD_DIM
                seg = acum[:, lo_l:lo_l + q] - acum_t[hd:hd + 1, :]
                m = (cb * jnp.exp(jnp.where(causal, seg, NEG_BIG))).astype(BF16)
                y_diag.append(_dot(m, xdt_b[:, lo_l:lo_l + SSD_HEAD_DIM]))
            y_parts.append(jnp.concatenate(y_diag, axis=1) + y_off)
            state_s[:, g * gw:(g + 1) * gw] = (state[:, g * gw:(g + 1) * gw] * jnp.exp(a_last[:, g * gw:(g + 1) * gw])
                                               + _dot_tn(bg_b, x_end[:, g * gw:(g + 1) * gw]))
        ys.append(jnp.concatenate(y_parts, axis=1) + dr_ref[...] * xs_c)

    @pl.when(pl.program_id(1) == pl.num_programs(1) - 1)
    def _():
        ssdn_ref[...] = state_s[...].T.reshape(SSD_HEADS, SSD_HEAD_DIM, SSD_STATE)

    y = jnp.concatenate(ys, axis=0) if len(ys) > 1 else ys[0]
    yz = y * _silu(z)
    yz = jnp.concatenate([_rms(yz[:, g * gw:(g + 1) * gw], 1.0) for g in range(SSD_GROUPS)], axis=1) * ng_ref[...]
    b_out = _dot(yz.astype(BF16), wbrs_ref[...])

    gates = jax.nn.sigmoid(gate_logits + bgate_ref[...])
    p_ref[...] = (gates[:, :D_MODEL] * a_out + gates[:, D_MODEL:2 * D_MODEL] * b_out).astype(BF16)
    g2_ref[...] = gates[:, 2 * D_MODEL:].astype(BF16)


def _mix_call(x, shift, scale, gpre, w, conv0, sconv0, ssd0, layer, depth, kv_prev):
    nseq, L, D = x.shape
    bm = min(MIX_ROWS, L)
    q = min(CHUNK, bm)
    kv_shape = jax.ShapeDtypeStruct((depth, nseq, L, ATT_HEADS, ATT_V_DIM), F32)
    kv_spec = pl.BlockSpec((None, None, bm, ATT_HEADS, ATT_V_DIM), lambda s, i: (layer, s, i, 0, 0))
    prev = list(kv_prev or ())
    rows = lambda n: pl.BlockSpec((None, bm, n), lambda s, i: (s, i, 0))
    per_seq = lambda a: pl.BlockSpec((None,) + a.shape[1:], lambda s, i: (s,) + (0,) * (len(a.shape) - 1))
    weights = [gpre, w["w_glu"], w["w_z"], w["w_xbc"], w["w_dt_rep"], w["w_dt_t"], w["w_qkv"], w["w_gate"], w["b_gate"],
               w["conv_w"], w["conv_b"], w["ln_g"], w["ln_b"], w["w_br_conv"],
               w["sconv_w"], w["sconv_b"], w["dtb_rep"], w["dtb_col"], w["alog_rep"], w["alog_col"], w["d_rep"],
               w["norm_g"], w["w_br_ssd"]]
    out_shape = [jax.ShapeDtypeStruct((nseq, L, D), BF16), jax.ShapeDtypeStruct((nseq, L, D), BF16), kv_shape, kv_shape,
                 jax.ShapeDtypeStruct((nseq, L, ATT_WIDTH), BF16), jax.ShapeDtypeStruct((nseq, L, ATT_WIDTH), BF16),
                 jax.ShapeDtypeStruct((nseq, L, ATT_WIDTH), BF16),
                 jax.ShapeDtypeStruct((nseq, CONV_WIDTH - 1, CONV_CH), F32),
                 jax.ShapeDtypeStruct((nseq, SSD_CONV_WIDTH - 1, SSD_CONV_CH), F32),
                 jax.ShapeDtypeStruct((nseq, SSD_HEADS, SSD_HEAD_DIM, SSD_STATE), F32)]
    out_specs = [rows(D), rows(D), kv_spec, kv_spec, rows(ATT_WIDTH), rows(ATT_WIDTH), rows(ATT_WIDTH),
                 per_seq(out_shape[7]), per_seq(out_shape[8]), per_seq(out_shape[9])]
    return pl.pallas_call(
        functools.partial(_mix_kernel, bm=bm, q=q, n_prev=len(prev)),
        grid=(nseq, L // bm),
        in_specs=[pl.BlockSpec(memory_space=pl.ANY)] * len(prev)
                 + [rows(D), per_seq(shift), per_seq(scale)] + [_resident(a.shape) for a in weights]
                 + [per_seq(conv0), per_seq(sconv0), per_seq(ssd0)],
        input_output_aliases={i: 2 + i for i in range(len(prev))},
        out_specs=out_specs,
        out_shape=out_shape,
        scratch_shapes=[pltpu.VMEM((CONV_PAD + bm, CONV_CH), F32), pltpu.VMEM((SCONV_PAD + bm, SSD_CONV_CH), F32),
                        pltpu.VMEM((SSD_STATE, SSD_INNER), F32),
                        pltpu.VMEM((SUBLANES - 1, CONV_PAD + bm - SUBLANES, CONV_CH), F32)],
        compiler_params=_params(("arbitrary", "arbitrary")),
        name="mix_front",
    )(*prev, x, shift, scale, *weights, conv0, sconv0, ssd0)


def _lambda(lq_ref, lk_ref, lam0):
    lq, lk = lq_ref[...], lk_ref[...]
    e0 = jnp.exp(jnp.sum(lq[0:1] * lk[0:1], axis=-1, keepdims=True))
    e1 = jnp.exp(jnp.sum(lq[1:2] * lk[1:2], axis=-1, keepdims=True))
    return e0 - e1 + lam0


def _attn_prompt_kernel(q_ref, k_ref, vt_ref, ext_ref, corr_ref, lq_ref, lk_ref, o_ref,
                        qa_s, acc_s, m_s, kn_s, sa_s, sb_s, *, lam0):
    blk, qw = ATT_BLOCK, ATT_QBLOCK
    per_q = qw // blk
    qi = pl.program_id(1)
    first = lax.broadcasted_iota(jnp.int32, (blk, LANES), 1) < ATT_QK_DIM
    lane_q = lax.broadcasted_iota(jnp.int32, (qw, LANES), 1)
    first_q = lane_q < ATT_QK_DIM
    ones_from = lambda c: jnp.where((lane_q >= c) & (lane_q < c + BIAS_TERMS), 1.0, 0.0).astype(BF16)
    head_cols = lambda h: slice(h * ATT_V_DIM, (h + 1) * ATT_V_DIM)
    qk_shift = ATT_QK_DIM.bit_length() - 1
    sel = (jnp.right_shift(lax.broadcasted_iota(jnp.int32, (ATT_WIDTH, LANES), 0), qk_shift)
           == lax.broadcasted_iota(jnp.int32, (ATT_WIDTH, LANES), 1)).astype(BF16)

    def max_sq_norm(x):
        xf = x.astype(F32)
        return jnp.max(_dot((xf * xf).astype(BF16), sel), axis=0, keepdims=True)

    @pl.when(qi == 0)
    def _():
        def body(t, mx):
            return jnp.maximum(mx, max_sq_norm(k_ref[pl.ds(pl.multiple_of(t * blk, blk), blk), :]))
        kn_s[0:1, :] = lax.fori_loop(0, k_ref.shape[0] // blk, body, jnp.zeros((1, LANES), F32))

    q = q_ref[...]
    for h in range(ATT_HEADS):
        qa_s[2 * h] = jnp.where(first_q, q[:, head_cols(h)], ones_from(ATT_QK_DIM))
        qa_s[2 * h + 1] = jnp.where(first_q, ones_from(0), q[:, head_cols(h)])
    acc_s[...] = jnp.zeros_like(acc_s)
    m_s[...] = jnp.full(m_s.shape, NEG_BIG, F32)
    ones_rows = jnp.ones((ATT_ONES_ROWS, blk), BF16)

    slot_of = lambda pos, t, c, nt: 2 * (pos * nt + t) + c

    def score_jobs(kjs, heads, s_ref):
        def job(pos, h, t, c):
            kh = k_ref[pl.ds(pl.multiple_of(kjs[t] * blk, blk), blk), head_cols(h)]
            ka = jnp.where(first, kh, ext_ref[h, 0]) if c == 0 else jnp.where(first, ext_ref[h, 1], kh)
            s_ref[slot_of(pos, t, c, len(kjs))] = _dot_nt(ka, qa_s[2 * h + c])
        return [functools.partial(job, pos, h, t, c)
                for pos, h in enumerate(heads) for c in range(2) for t in range(len(kjs))]

    def fold_jobs(kjs, heads, s_ref, diagonal=None):
        k0s = [pl.multiple_of(kj * blk, blk) for kj in kjs]
        nt = len(k0s)

        def job(pos, h, c):
            bases = [(ALIBI_SLOPES[h] * LOG2E) * k0.astype(F32) for k0 in k0s]
            vt = jnp.concatenate([vt_ref[head_cols(h), pl.ds(k0, blk)] for k0 in k0s], axis=1)
            vt = jnp.concatenate([vt, jnp.concatenate([ones_rows] * nt, axis=1)], axis=0)
            sc = [s_ref[slot_of(pos, t, c, nt)] for t in range(nt)]
            if diagonal is not None:
                sc = [x + (-2.0 * ALIBI_SLOPES[h] * LOG2E) * corr_ref[diagonal] for x in sc]
            m_old = m_s[2 * h + c:2 * h + c + 1, :]
            m_new = m_old
            for t in range(nt):
                m_new = jnp.maximum(m_new, jnp.max(sc[t], axis=0, keepdims=True) + bases[t])
            m_s[2 * h + c:2 * h + c + 1, :] = m_new
            p = jnp.concatenate([jnp.exp2(sc[t] - (m_new - bases[t])).astype(BF16) for t in range(nt)], axis=0)
            acc_s[2 * h + c] = jnp.exp2(m_old - m_new) * acc_s[2 * h + c] + _dot(vt, p)
        return [functools.partial(job, pos, h, c) for pos, h in enumerate(heads) for c in range(2)]

    def run(jobs):
        for job in jobs:
            job()

    def interleave(ahead, folds):
        per = len(ahead) // len(folds)
        for f, job in enumerate(folds):
            run(ahead[f * per:(f + 1) * per])
            job()

    bufs = (sa_s, sb_s)
    all_heads = range(ATT_HEADS)
    run(score_jobs([per_q * qi], all_heads, bufs[0]))
    for d in range(per_q):
        ahead = score_jobs([per_q * qi + d + 1], all_heads, bufs[(d + 1) % 2]) if d + 1 < per_q else []
        folds = fold_jobs([per_q * qi + d], all_heads, bufs[d % 2], diagonal=d)
        if ahead:
            interleave(ahead, folds)
        else:
            run(folds)

    bound = jnp.sqrt(max_sq_norm(q)) * jnp.sqrt(kn_s[0:1, :]) * NORM_SLACK
    top = per_q * qi
    lo, limit = [], top
    for h in range(ATT_HEADS):
        gap = None
        for c in range(2):
            g = bound[:, 2 * h + c:2 * h + c + 1] - jnp.min(m_s[2 * h + c:2 * h + c + 1, :], axis=1, keepdims=True)
            gap = g if gap is None else jnp.maximum(gap, g)
        x = (-ATT_SKIP_LOG2 - gap) / (ALIBI_SLOPES[h] * LOG2E)
        need = jnp.ceil((x - (blk - 1)) / blk)
        need = jnp.clip(jnp.where(need == need, need, 0.0), 0.0, top.astype(F32))
        limit = jnp.minimum(limit, need.astype(jnp.int32)[0, 0])
        lo.append(limit)

    hi = top
    widths = [w for _, w in ATT_PHASES]
    for i, (heads, width) in enumerate(ATT_PHASES):
        align = max(width, widths[min(i + 1, len(widths) - 1)]).bit_length() - 1
        lo_i = jnp.left_shift(jnp.right_shift(lo[heads[0]], align), align)

        n_i = jnp.right_shift(hi - lo_i, width.bit_length() - 1)
        tiles = lambda j, hi=hi, width=width: [jnp.maximum(hi - (j + 1) * width + t, 0) for t in range(width)]

        @pl.when(n_i > 0)
        def _(tiles=tiles, heads=heads):
            run(score_jobs(tiles(0), heads, sa_s))

        def body(j, carry, tiles=tiles, heads=heads):
            def trip(cur, nxt):
                interleave(score_jobs(tiles(j + 1), heads, nxt), fold_jobs(tiles(j), heads, cur))
            lax.cond(jnp.bitwise_and(j, 1) == 0, lambda: trip(sa_s, sb_s), lambda: trip(sb_s, sa_s))
            return carry

        lax.fori_loop(0, n_i, body, 0)
        hi = lo_i

    lam = _lambda(lq_ref, lk_ref, lam0)
    for h in range(ATT_HEADS):
        a1, a2 = acc_s[2 * h], acc_s[2 * h + 1]
        o_t = (a1[:ATT_V_DIM] / a1[ATT_V_DIM:ATT_V_DIM + 1]
               - lam * (a2[:ATT_V_DIM] / a2[ATT_V_DIM:ATT_V_DIM + 1]))
        o_ref[:, head_cols(h)] = o_t.T


def _attn_tables():
    blk, qw = ATT_BLOCK, ATT_QBLOCK
    pos = np.arange(blk, dtype=np.float64)
    ext = np.zeros((ATT_HEADS, 2, blk, LANES), np.float32)
    to_bf16 = lambda x: x.astype(ml_dtypes.bfloat16).astype(np.float64)
    for h, slope in enumerate(ALIBI_SLOPES):
        rest = slope * LOG2E * pos
        for t in range(BIAS_TERMS):
            term = to_bf16(rest)
            ext[h, 0, :, ATT_QK_DIM + t] = term
            ext[h, 1, :, t] = term
            rest = rest - term
    ahead = np.zeros((qw // blk, blk, qw), np.float32)
    qpos = np.arange(qw, dtype=np.float64)[None, :]
    for d in range(qw // blk):
        kpos = (d * blk + pos)[:, None]
        ahead[d] = np.where((kpos // CHUNK) <= (qpos // CHUNK), np.maximum(kpos - qpos, 0.0), -NEG_BIG)
    return jnp.asarray(ext, BF16), jnp.asarray(ahead, F32)


def _attn_prompt_call(qb, kb, vb, lq, lk, lam0):
    b, S, W = qb.shape
    blk, qw = ATT_BLOCK, ATT_QBLOCK
    vt = jnp.swapaxes(vb, 1, 2)
    ext, ahead = _attn_tables()
    whole = lambda shape: pl.BlockSpec((None,) + shape, lambda s, i: (s, 0, 0), pipeline_mode=pl.Buffered(1))
    return pl.pallas_call(
        functools.partial(_attn_prompt_kernel, lam0=lam0),
        grid=(b, S // qw),
        in_specs=[pl.BlockSpec((None, qw, W), lambda s, i: (s, i, 0)), whole((S, W)), whole((W, S)),
                  _resident(ext.shape), _resident(ahead.shape), _resident(lq.shape), _resident(lk.shape)],
        out_specs=pl.BlockSpec((None, qw, W), lambda s, i: (s, i, 0)),
        out_shape=jax.ShapeDtypeStruct((b, S, W), F32),
        scratch_shapes=[pltpu.VMEM((2 * ATT_HEADS, qw, LANES), BF16),
                        pltpu.VMEM((2 * ATT_HEADS, ATT_V_DIM + ATT_ONES_ROWS, qw), F32),
                        pltpu.VMEM((2 * ATT_HEADS, qw), F32), pltpu.VMEM((SUBLANES, LANES), F32),
                        pltpu.VMEM((2 * ATT_HEADS, blk, qw), F32), pltpu.VMEM((2 * ATT_HEADS, blk, qw), F32)],
        compiler_params=_params(("arbitrary", "arbitrary")),
        name="attn_prompt",
    )(qb, kb, vt, ext, ahead, lq, lk)


def _attn_sample_kernel(q_ref, kn_ref, vn_ref, kc_ref, vc_ref, lq_ref, lk_ref, o_ref, *, lam0):
    ls, past = q_ref.shape[0], kc_ref.shape[0]
    lam = _lambda(lq_ref, lk_ref, lam0)
    lane = lax.broadcasted_iota(jnp.int32, (ls, LANES), 1)
    first = lane < ATT_QK_DIM

    def bias_mask(nk, k_first):
        qpos = past + lax.broadcasted_iota(jnp.int32, (ls, nk), 0)
        kpos = k_first + lax.broadcasted_iota(jnp.int32, (ls, nk), 1)
        dist = jnp.abs(qpos - kpos).astype(F32)
        visible = jnp.right_shift(kpos, CHUNK_SHIFT) <= jnp.right_shift(qpos, CHUNK_SHIFT)
        return dist, visible

    dist_c, vis_c = bias_mask(past, 0)
    dist_n, vis_n = bias_mask(ls, past)
    outs = []
    for h in range(ATT_HEADS):
        cols = slice(h * ATT_V_DIM, (h + 1) * ATT_V_DIM)
        qh = q_ref[:, cols]
        kc, kn = kc_ref[:, h, :].astype(BF16), kn_ref[:, cols]
        vc, vn = vc_ref[:, h, :].astype(BF16), vn_ref[:, cols]
        o = []
        for c in range(2):
            qm = jnp.where(first == (c == 0), qh, jnp.zeros_like(qh))
            sc = jnp.where(vis_c, _dot_nt(qm, kc) - (ALIBI_SLOPES[h] * LOG2E) * dist_c, NEG_BIG)
            sn = jnp.where(vis_n, _dot_nt(qm, kn) - (ALIBI_SLOPES[h] * LOG2E) * dist_n, NEG_BIG)
            m = jnp.maximum(jnp.max(sc, axis=-1, keepdims=True), jnp.max(sn, axis=-1, keepdims=True))
            pc, pn = jnp.exp2(sc - m), jnp.exp2(sn - m)
            l = jnp.sum(pc, axis=-1, keepdims=True) + jnp.sum(pn, axis=-1, keepdims=True)
            o.append((_dot(pc.astype(BF16), vc) + _dot(pn.astype(BF16), vn)) / l)
        outs.append(o[0] - lam * o[1])
    o_ref[...] = jnp.concatenate(outs, axis=1)


def _attn_sample_call(qb, kb, vb, cache_k, cache_v, layer, lq, lk, lam0):
    b, ls, W = qb.shape
    past = cache_k.shape[2]
    new = pl.BlockSpec((None, ls, W), lambda s: (s, 0, 0))
    old = pl.BlockSpec((None, None, past, ATT_HEADS, ATT_V_DIM), lambda s: (layer, s, 0, 0, 0))
    return pl.pallas_call(
        functools.partial(_attn_sample_kernel, lam0=lam0),
        grid=(b,),
        in_specs=[new, new, new, old, old, _resident(lq.shape), _resident(lk.shape)],
        out_specs=new,
        out_shape=jax.ShapeDtypeStruct((b, ls, W), F32),
        compiler_params=_params(("arbitrary",)),
        name="attn_sample",
    )(qb, kb, vb, cache_k, cache_v, lq, lk)


def _prep_layer(l, w_in, b_gate, conv_dw_w, conv_dw_b, conv_ln_g, conv_ln_b, w_br_conv, ssd_conv_w, ssd_conv_b,
                ssd_dt_bias, ssd_A_log, ssd_D, ssd_norm_g, w_br_ssd):
    splits = (2 * CONV_CH, SSD_INNER, SSD_CONV_CH, SSD_HEADS, 3 * ATT_WIDTH, N_BRANCH * D_MODEL)
    pts = np.cumsum(splits)[:-1].tolist()
    w_glu, w_z, w_xbc, w_dt, w_qkv, w_gate = jnp.split(w_in[l].astype(BF16), pts, axis=1)
    rep = lambda a: jnp.repeat(a, SSD_HEAD_DIM, axis=-1)
    return dict(
        w_glu=w_glu, w_z=w_z, w_xbc=w_xbc, w_dt_rep=rep(w_dt), w_dt_t=w_dt.T, w_qkv=w_qkv, w_gate=w_gate,
        b_gate=b_gate[l].reshape(1, N_BRANCH * D_MODEL),
        conv_w=conv_dw_w[l], conv_b=conv_dw_b[l][None], ln_g=conv_ln_g[l][None], ln_b=conv_ln_b[l][None],
        w_br_conv=w_br_conv[l].astype(BF16),
        sconv_w=ssd_conv_w[l], sconv_b=ssd_conv_b[l][None],
        dtb_rep=rep(ssd_dt_bias[l])[None], dtb_col=ssd_dt_bias[l][:, None],
        alog_rep=rep(ssd_A_log[l])[None], alog_col=ssd_A_log[l][:, None],
        d_rep=rep(ssd_D[l])[None], norm_g=ssd_norm_g[l][None], w_br_ssd=w_br_ssd[l].astype(BF16))


def _state_in(conv, sconv, ssd):
    conv = jnp.pad(conv, ((0, 0), (CONV_PAD - (CONV_WIDTH - 1), 0), (0, 0)))
    sconv = jnp.pad(sconv, ((0, 0), (SCONV_PAD - (SSD_CONV_WIDTH - 1), 0), (0, 0)))
    return conv, sconv, ssd


def kernel(x_prompt, x_sample, cache_attn_k, cache_attn_v, state_conv, state_ssd_conv, state_ssd, c_prompt, c_sample,
           w_ada, b_ada, norm_pre, norm_post, w_ffn_in, w_ffn_out, w_in, b_gate, conv_dw_w, conv_dw_b, conv_ln_g,
           conv_ln_b, w_br_conv, ssd_conv_w, ssd_conv_b, ssd_dt_bias, ssd_A_log, ssd_D, ssd_norm_g, w_br_ssd,
           lambda_q, lambda_k, attn_subln_g, w_br_attn, w_mix_out):
    depth = w_ada.shape[0]
    bp, S, D = x_prompt.shape
    bs, ls, _ = x_sample.shape

    mods = _ada_call(jnp.concatenate([c_prompt, c_sample], axis=0), w_ada, b_ada)
    mods = mods.reshape(depth, bp + bs, N_SUB, 3, D)

    zeros = _state_in(jnp.zeros((bp, CONV_WIDTH - 1, CONV_CH), F32), jnp.zeros((bp, SSD_CONV_WIDTH - 1, SSD_CONV_CH), F32),
                      jnp.zeros((bp, SSD_HEADS, SSD_HEAD_DIM, SSD_STATE), F32))
    xp = x_prompt
    xs = x_sample.reshape(1, bs * ls, D)
    st_p, st_s = [], []
    kv_p = kv_s = None
    for l in range(depth):
        lam0 = _lambda_init(l)
        w = _prep_layer(l, w_in, b_gate, conv_dw_w, conv_dw_b, conv_ln_g, conv_ln_b, w_br_conv, ssd_conv_w, ssd_conv_b,
                        ssd_dt_bias, ssd_A_log, ssd_D, ssd_norm_g, w_br_ssd)
        wf_in, wf_out = w_ffn_in[l].astype(BF16), w_ffn_out[l].astype(BF16)
        w_bra, w_mix = w_br_attn[l].astype(BF16), w_mix_out[l].astype(BF16)
        gpre = lambda s: norm_pre[l, s][None]
        gpost = lambda s: norm_post[l, s][None]
        subg = attn_subln_g[l][None]
        mp = lambda s, k: mods[l, :bp, s, k][:, None, :]
        ms_seq = lambda s, k: mods[l, bp:, s, k][:, None, :]
        ms_tok = lambda s, k: jnp.repeat(mods[l, bp:, s, k], ls, axis=0)[None]

        xp = _ffn_call(xp, (mp(0, 0), mp(0, 1), mp(0, 2)), gpre(0), gpost(0), wf_in[0], wf_out[0])
        p, g2, *kv_p, qb, kb, vb, convn, sconvn, ssdn = _mix_call(xp, mp(1, 0), mp(1, 1), gpre(1), w, *zeros,
                                                                  l, depth, kv_p)
        o = _attn_prompt_call(qb, kb, vb, lambda_q[l], lambda_k[l], lam0)
        mix = dict(p=p, g2=g2, o=o, subg=subg, w_br_attn=w_bra, w_mix_out=w_mix, gate=mp(1, 2), gpost=gpost(1))
        xp = _ffn_call(xp, (mp(2, 0), mp(2, 1), mp(2, 2)), gpre(2), gpost(2), wf_in[1], wf_out[1], mix=mix, lam0=lam0)
        st_p.append((convn, sconvn, ssdn))

        xs = _ffn_call(xs, (ms_tok(0, 0), ms_tok(0, 1), ms_tok(0, 2)), gpre(0), gpost(0), wf_in[0], wf_out[0])
        states = _state_in(state_conv[l], state_ssd_conv[l], state_ssd[l])
        p, g2, *kv_s, qb, kb, vb, convn, sconvn, ssdn = _mix_call(xs.reshape(bs, ls, D), ms_seq(1, 0), ms_seq(1, 1),
                                                                  gpre(1), w, *states, l, depth, kv_s)
        o = _attn_sample_call(qb, kb, vb, cache_attn_k, cache_attn_v, l, lambda_q[l], lambda_k[l], lam0)
        flat = lambda t: t.reshape(1, bs * ls, t.shape[-1])
        mix = dict(p=flat(p), g2=flat(g2), o=flat(o), subg=subg, w_br_attn=w_bra, w_mix_out=w_mix, gate=ms_tok(1, 2),
                   gpost=gpost(1))
        xs = _ffn_call(xs, (ms_tok(2, 0), ms_tok(2, 1), ms_tok(2, 2)), gpre(2), gpost(2), wf_in[1], wf_out[1], mix=mix,
                       lam0=lam0)
        st_s.append((convn, sconvn, ssdn))

    stack = lambda st, i: jnp.stack([s[i] for s in st])
    return (xp, xs.reshape(bs, ls, D),
            kv_p[0], kv_p[1], stack(st_p, 0), stack(st_p, 1), stack(st_p, 2),
            kv_s[0], kv_s[1], stack(st_s, 0), stack(st_s, 1), stack(st_s, 2))
```

```python
import functools
import math

import ml_dtypes
import numpy as np
import jax
import jax.numpy as jnp
from jax import lax
from jax.experimental import pallas as pl
from jax.experimental.pallas import tpu as pltpu

F32, BF16 = jnp.float32, jnp.bfloat16

D_MODEL = 1024
D_FF = 2816
CONV_CH = 512
CONV_WIDTH = 31
SSD_HEADS = 16
SSD_HEAD_DIM = 64
SSD_INNER = SSD_HEADS * SSD_HEAD_DIM
SSD_GROUPS = 2
SSD_STATE = 128
SSD_CONV_WIDTH = 4
SSD_CONV_CH = SSD_INNER + 2 * SSD_GROUPS * SSD_STATE
CHUNK = 64
CHUNK_SHIFT = 6
ATT_HEADS = 4
ATT_QK_DIM = 64
ATT_V_DIM = 2 * ATT_QK_DIM
ATT_WIDTH = ATT_HEADS * ATT_V_DIM
ATT_SCALE = ATT_QK_DIM ** -0.5
ALIBI_SLOPES = tuple(2.0 ** (-8.0 * (h + 1) / ATT_HEADS) for h in range(ATT_HEADS))
N_BRANCH = 3
N_SUB = 3
EPS = 1e-6
SUBLN_EPS = 1e-5
CONV_LN_EPS = 1e-5
NEG_BIG = -1e30
LOG2E = math.log2(math.e)
BIAS_TERMS = 3
ATT_ONES_ROWS = 16
ATT_SKIP_LOG2 = 160.0
NORM_SLACK = 1.0 + 2.0 ** -6
ATT_PHASES = (((0, 1, 2, 3), 1), ((1, 2, 3), 1), ((2, 3), 2), ((3,), 4))
assert all(len(heads) * width <= ATT_HEADS for heads, width in ATT_PHASES)

VMEM_LIMIT_BYTES = 58 * 1024 * 1024
LANES = 128
SUBLANES = 8

FFN_ROWS = 512
FFN_EDGES = (0, 768, 1792, D_FF)
MIX_COLS = 512
CONV_ROWS = 32
MIX_ROWS = 256
ATT_BLOCK = 256
ATT_QBLOCK = 512
ADA_COLS = 1152
CONV_PAD = 32
SCONV_PAD = 8


def _lambda_init(layer):
    return 0.8 - 0.6 * math.exp(-0.3 * layer)


def _params(sem):
    return pltpu.CompilerParams(dimension_semantics=sem, vmem_limit_bytes=VMEM_LIMIT_BYTES)


def _resident(shape):
    nd = len(shape)
    return pl.BlockSpec(shape, lambda *_: (0,) * nd, pipeline_mode=pl.Buffered(1))


def _dot(a, b):
    return jnp.dot(a, b, preferred_element_type=F32)


def _dot_nt(a, b):
    return lax.dot_general(a, b, (((1,), (1,)), ((), ())), preferred_element_type=F32)


def _dot_tn(a, b):
    return lax.dot_general(a, b, (((0,), (0,)), ((), ())), preferred_element_type=F32)


def _rms(x, g, eps=EPS):
    return x * lax.rsqrt(jnp.mean(x * x, axis=-1, keepdims=True) + eps) * g


def _silu(x):
    return x * jax.nn.sigmoid(x)


def _split3(a):
    hi = a.astype(BF16)
    r = a - hi.astype(F32)
    mid = r.astype(BF16)
    lo = (r - mid.astype(F32)).astype(BF16)
    return hi, mid, lo


def _ada_kernel(c_ref, w_ref, b_ref, o_ref):
    sc = _silu(c_ref[...]).astype(BF16)
    o_ref[...] = _dot(sc, w_ref[...].astype(BF16)) + b_ref[...]


def _ada_call(c_all, w_ada, b_ada):
    depth, d, n = w_ada.shape
    rows = c_all.shape[0]
    return pl.pallas_call(
        _ada_kernel,
        grid=(depth, n // ADA_COLS),
        in_specs=[pl.BlockSpec((rows, d), lambda l, j: (0, 0)),
                  pl.BlockSpec((None, d, ADA_COLS), lambda l, j: (l, 0, j)),
                  pl.BlockSpec((None, 1, ADA_COLS), lambda l, j: (l, 0, j))],
        out_specs=pl.BlockSpec((None, rows, ADA_COLS), lambda l, j: (l, 0, j)),
        out_shape=jax.ShapeDtypeStruct((depth, rows, n), F32),
        compiler_params=_params(("arbitrary", "arbitrary")),
        name="adaln",
    )(c_all, w_ada, b_ada.reshape(depth, 1, n))


def _ffn_kernel(*refs, with_mix, lam0):
    if with_mix:
        (x_ref, p_ref, g2_ref, o_ref, subg_ref, wbr_ref, wmix_ref, gate1_ref, gpost1_ref, *rest) = refs
    else:
        x_ref, *rest = refs
    shift_ref, scale_ref, gate_ref, gpre_ref, gpost_ref, win_ref, wout_ref, out_ref = rest
    x = x_ref[...]
    if with_mix:
        o = o_ref[...]
        subg = subg_ref[...]
        heads = []
        for h in range(ATT_HEADS):
            oh = o[:, h * ATT_V_DIM:(h + 1) * ATT_V_DIM]
            heads.append(_rms(oh, subg, SUBLN_EPS) * (1.0 - lam0))
        c_out = _dot(jnp.concatenate(heads, axis=1).astype(BF16), wbr_ref[...])
        merged = (p_ref[...] + g2_ref[...] * c_out).astype(BF16)
        x = x + gate1_ref[...] * _rms(_dot(merged, wmix_ref[...]), gpost1_ref[...])
    h = (_rms(x, gpre_ref[...]) * (1.0 + scale_ref[...]) + shift_ref[...]).astype(BF16)
    edges = FFN_EDGES

    def project(c):
        return (_dot(h, win_ref[:, edges[c]:edges[c + 1]]), _dot(h, win_ref[:, D_FF + edges[c]:D_FF + edges[c + 1]]))

    f, cur = None, project(0)
    for c in range(len(edges) - 1):
        nxt = project(c + 1) if c + 2 < len(edges) else None
        up, gt = cur
        part = _dot((_silu(gt) * up).astype(BF16), wout_ref[edges[c]:edges[c + 1], :])
        f = part if f is None else f + part
        cur = nxt
    out_ref[...] = x + 0.5 * gate_ref[...] * _rms(f, gpost_ref[...])


def _ffn_call(x, mod, gpre, gpost, w_in, w_out, mix=None, lam0=0.0):
    nseq, L, D = x.shape
    bm = min(FFN_ROWS, L)
    row_spec = pl.BlockSpec((None, bm, D), lambda s, i: (s, i, 0))

    def mod_spec(m):
        if m.shape[1] == 1:
            return pl.BlockSpec((None, 1, D), lambda s, i: (s, 0, 0))
        return row_spec

    args, specs = [x], [row_spec]
    if mix is not None:
        args += [mix["p"], mix["g2"], mix["o"], mix["subg"], mix["w_br_attn"], mix["w_mix_out"], mix["gate"], mix["gpost"]]
        specs += [row_spec, row_spec, pl.BlockSpec((None, bm, ATT_WIDTH), lambda s, i: (s, i, 0)),
                  _resident((1, ATT_V_DIM)), _resident((ATT_WIDTH, D)), _resident((D, D)),
                  mod_spec(mix["gate"]), _resident((1, D))]
    args += [mod[0], mod[1], mod[2], gpre, gpost, w_in, w_out]
    specs += [mod_spec(mod[0]), mod_spec(mod[1]), mod_spec(mod[2]), _resident((1, D)), _resident((1, D)),
              _resident(w_in.shape), _resident(w_out.shape)]
    return pl.pallas_call(
        functools.partial(_ffn_kernel, with_mix=mix is not None, lam0=lam0),
        grid=(nseq, L // bm),
        in_specs=specs,
        out_specs=row_spec,
        out_shape=jax.ShapeDtypeStruct(x.shape, F32),
        compiler_params=_params(("arbitrary", "arbitrary")),
        name="merge_ffn" if mix is not None else "ffn",
    )(*args)


def _interleave(ahead, jobs):
    done = 0
    for i, job in enumerate(jobs):
        upto = -(-len(ahead) * (i + 1) // len(jobs))
        for a in ahead[done:upto]:
            a()
        done = upto
        job()


def _dwconv_jobs(src_ref, first, w, b, rows, rc, shift_ref=None):
    width, ch = w.shape
    if shift_ref is not None:
        span = shift_ref.shape[1]
        for s in sorted({(first + j) % SUBLANES for j in range(width)} - {0}):
            shift_ref[s - 1] = src_ref[s:s + span, :]

    def tap(j, r0):
        a, s = divmod(first + j, SUBLANES)
        if shift_ref is None or s == 0:
            return src_ref[first + j + r0:first + j + r0 + rc, :]
        return shift_ref[s - 1, a * SUBLANES + r0:a * SUBLANES + r0 + rc, :]

    outs = []

    def job(r0):
        acc = jnp.broadcast_to(b, (rc, ch))
        for j in range(width):
            acc = acc + w[j:j + 1, :] * tap(j, r0)
        outs.append(acc)

    return [functools.partial(job, r0) for r0 in range(0, rows, rc)], outs


def _mix_kernel(*refs, bm, q, n_prev):
    (x_ref, shift_ref, scale_ref, gpre_ref,
     wglu_ref, wz_ref, wxbc_ref, wdtr_ref, wdtt_ref, wqkv_ref, wgate_ref, bgate_ref,
     cw_ref, cb_ref, lng_ref, lnb_ref, wbrc_ref,
     sw_ref, sb_ref, dtbr_ref, dtbc_ref, alr_ref, alc_ref, dr_ref, ng_ref, wbrs_ref,
     conv0_ref, sconv0_ref, ssd0_ref,
     p_ref, g2_ref, k_ref, v_ref, qb_ref, kb_ref, vb_ref, convn_ref, sconvn_ref, ssdn_ref,
     conv_s, sconv_s, state_s, shift_s) = refs[n_prev:]

    @pl.when(pl.program_id(1) == 0)
    def _():
        conv_s[0:CONV_PAD, :] = conv0_ref[...]
        sconv_s[0:SCONV_PAD, :] = sconv0_ref[...]
        state_s[...] = ssd0_ref[...].reshape(SSD_INNER, SSD_STATE).T

    h = (_rms(x_ref[...], gpre_ref[...]) * (1.0 + scale_ref[...]) + shift_ref[...]).astype(BF16)

    glu = _dot(h, wglu_ref[...])
    conv_s[CONV_PAD:CONV_PAD + bm, :] = glu[:, :CONV_CH] * jax.nn.sigmoid(glu[:, CONV_CH:])
    sconv_s[SCONV_PAD:SCONV_PAD + bm, :] = _dot(h, wxbc_ref[...])
    dt_raw_t = _dot_nt(wdtt_ref[...], h)
    slices = {}

    def project(name, w_ref):
        def job(c0):
            slices[name].append(_dot(h, w_ref[:, c0:c0 + MIX_COLS]))
        slices[name] = []
        return [functools.partial(job, c0) for c0 in range(0, w_ref.shape[1], MIX_COLS)]

    projections = (project("dt", wdtr_ref) + project("z", wz_ref) + project("qkv", wqkv_ref)
                   + project("gate", wgate_ref))
    rc = min(CONV_ROWS, bm)
    conv_jobs, conv_rows = _dwconv_jobs(conv_s, CONV_PAD - (CONV_WIDTH - 1), cw_ref[...], cb_ref[...], bm, rc, shift_s)
    sconv_jobs, sconv_rows = _dwconv_jobs(sconv_s, SCONV_PAD - (SSD_CONV_WIDTH - 1), sw_ref[...], sb_ref[...], bm, rc)
    _interleave(projections, conv_jobs + sconv_jobs)
    dt_raw, z, qkv, gate_logits = (jnp.concatenate(slices[n], axis=1) for n in ("dt", "z", "qkv", "gate"))

    k = qkv[:, ATT_WIDTH:2 * ATT_WIDTH]
    v = qkv[:, 2 * ATT_WIDTH:]
    for hd in range(ATT_HEADS):
        k_ref[:, hd, :] = k[:, hd * ATT_V_DIM:(hd + 1) * ATT_V_DIM]
        v_ref[:, hd, :] = v[:, hd * ATT_V_DIM:(hd + 1) * ATT_V_DIM]
    qb_ref[...] = (qkv[:, :ATT_WIDTH] * (ATT_SCALE * LOG2E)).astype(BF16)
    kb_ref[...] = k.astype(BF16)
    vb_ref[...] = v.astype(BF16)

    a = jnp.concatenate(conv_rows, axis=0)
    mu = jnp.mean(a, axis=-1, keepdims=True)
    ac = a - mu
    a = ac * lax.rsqrt(jnp.mean(ac * ac, axis=-1, keepdims=True) + CONV_LN_EPS) * lng_ref[...] + lnb_ref[...]
    a_out = _dot(_silu(a).astype(BF16), wbrc_ref[...])
    convn_ref[...] = conv_s[CONV_PAD + bm - (CONV_WIDTH - 1):CONV_PAD + bm, :]
    conv_s[0:CONV_PAD, :] = conv_s[bm:bm + CONV_PAD, :]

    xbc = _silu(jnp.concatenate(sconv_rows, axis=0))
    sconvn_ref[...] = sconv_s[SCONV_PAD + bm - (SSD_CONV_WIDTH - 1):SCONV_PAD + bm, :]
    sconv_s[0:SCONV_PAD, :] = sconv_s[bm:bm + SCONV_PAD, :]

    dt_e = jax.nn.softplus(dt_raw + dtbr_ref[...])
    dt_t = jax.nn.softplus(dt_raw_t + dtbc_ref[...])
    a_rep = -jnp.exp(alr_ref[...])
    a_col = -jnp.exp(alc_ref[...])
    row = lax.broadcasted_iota(jnp.int32, (q, q), 0)
    col = lax.broadcasted_iota(jnp.int32, (q, q), 1)
    causal = row >= col
    tril = causal.astype(BF16)
    triu = (row <= col).astype(BF16)
    gw = SSD_INNER // SSD_GROUPS
    ys = []
    for c in range(bm // q):
        rows = slice(c * q, (c + 1) * q)
        dt_c = dt_e[rows]
        hi, mid, lo = _split3(dt_c * a_rep)
        acum = _dot(tril, hi) + _dot(tril, mid) + _dot(tril, lo)
        hi, mid, lo = _split3(dt_t[:, rows] * a_col)
        acum_t = _dot(hi, triu) + _dot(mid, triu) + _dot(lo, triu)
        xs_c = xbc[rows, :SSD_INNER]
        xdt = xs_c * dt_c
        a_last = acum[q - 1:q, :]
        x_end = (xdt * jnp.exp(a_last - acum)).astype(BF16)
        e_in = jnp.exp(acum)
        xdt_b = xdt.astype(BF16)
        state = state_s[...]
        state_b = state.astype(BF16)
        y_parts = []
        for g in range(SSD_GROUPS):
            bg = xbc[rows, SSD_INNER + g * SSD_STATE:SSD_INNER + (g + 1) * SSD_STATE]
            cg = xbc[rows, SSD_INNER + (SSD_GROUPS + g) * SSD_STATE:SSD_INNER + (SSD_GROUPS + g + 1) * SSD_STATE]
            bg_b, cg_b = bg.astype(BF16), cg.astype(BF16)
            cb = _dot_nt(cg_b, bg_b)
            y_off = _dot(cg_b, state_b[:, g * gw:(g + 1) * gw]) * e_in[:, g * gw:(g + 1) * gw]
            y_diag = []
            for r in range(SSD_HEADS // SSD_GROUPS):
                hd = g * (SSD_HEADS // SSD_GROUPS) + r
                lo_l = hd * SSD_HEAD_DIM
                seg = acum[:, lo_l:lo_l + q] - acum_t[hd:hd + 1, :]
                m = (cb * jnp.exp(jnp.where(causal, seg, NEG_BIG))).astype(BF16)
                y_diag.append(_dot(m, xdt_b[:, lo_l:lo_l + SSD_HEAD_DIM]))
            y_parts.append(jnp.concatenate(y_diag, axis=1) + y_off)
            state_s[:, g * gw:(g + 1) * gw] = (state[:, g * gw:(g + 1) * gw] * jnp.exp(a_last[:, g * gw:(g + 1) * gw])
                                               + _dot_tn(bg_b, x_end[:, g * gw:(g + 1) * gw]))
        ys.append(jnp.concatenate(y_parts, axis=1) + dr_ref[...] * xs_c)

    @pl.when(pl.program_id(1) == pl.num_programs(1) - 1)
    def _():
        ssdn_ref[...] = state_s[...].T.reshape(SSD_HEADS, SSD_HEAD_DIM, SSD_STATE)

    y = jnp.concatenate(ys, axis=0) if len(ys) > 1 else ys[0]
    yz = y * _silu(z)
    yz = jnp.concatenate([_rms(yz[:, g * gw:(g + 1) * gw], 1.0) for g in range(SSD_GROUPS)], axis=1) * ng_ref[...]
    b_out = _dot(yz.astype(BF16), wbrs_ref[...])

    gates = jax.nn.sigmoid(gate_logits + bgate_ref[...])
    p_ref[...] = (gates[:, :D_MODEL] * a_out + gates[:, D_MODEL:2 * D_MODEL] * b_out).astype(BF16)
    g2_ref[...] = gates[:, 2 * D_MODEL:].astype(BF16)


def _mix_call(x, shift, scale, gpre, w, conv0, sconv0, ssd0, layer, depth, kv_prev):
    nseq, L, D = x.shape
    bm = min(MIX_ROWS, L)
    q = min(CHUNK, bm)
    kv_shape = jax.ShapeDtypeStruct((depth, nseq, L, ATT_HEADS, ATT_V_DIM), F32)
    kv_spec = pl.BlockSpec((None, None, bm, ATT_HEADS, ATT_V_DIM), lambda s, i: (layer, s, i, 0, 0))
    prev = list(kv_prev or ())
    rows = lambda n: pl.BlockSpec((None, bm, n), lambda s, i: (s, i, 0))
    per_seq = lambda a: pl.BlockSpec((None,) + a.shape[1:], lambda s, i: (s,) + (0,) * (len(a.shape) - 1))
    weights = [gpre, w["w_glu"], w["w_z"], w["w_xbc"], w["w_dt_rep"], w["w_dt_t"], w["w_qkv"], w["w_gate"], w["b_gate"],
               w["conv_w"], w["conv_b"], w["ln_g"], w["ln_b"], w["w_br_conv"],
               w["sconv_w"], w["sconv_b"], w["dtb_rep"], w["dtb_col"], w["alog_rep"], w["alog_col"], w["d_rep"],
               w["norm_g"], w["w_br_ssd"]]
    out_shape = [jax.ShapeDtypeStruct((nseq, L, D), BF16), jax.ShapeDtypeStruct((nseq, L, D), BF16), kv_shape, kv_shape,
                 jax.ShapeDtypeStruct((nseq, L, ATT_WIDTH), BF16), jax.ShapeDtypeStruct((nseq, L, ATT_WIDTH), BF16),
                 jax.ShapeDtypeStruct((nseq, L, ATT_WIDTH), BF16),
                 jax.ShapeDtypeStruct((nseq, CONV_WIDTH - 1, CONV_CH), F32),
                 jax.ShapeDtypeStruct((nseq, SSD_CONV_WIDTH - 1, SSD_CONV_CH), F32),
                 jax.ShapeDtypeStruct((nseq, SSD_HEADS, SSD_HEAD_DIM, SSD_STATE), F32)]
    out_specs = [rows(D), rows(D), kv_spec, kv_spec, rows(ATT_WIDTH), rows(ATT_WIDTH), rows(ATT_WIDTH),
                 per_seq(out_shape[7]), per_seq(out_shape[8]), per_seq(out_shape[9])]
    return pl.pallas_call(
        functools.partial(_mix_kernel, bm=bm, q=q, n_prev=len(prev)),
        grid=(nseq, L // bm),
        in_specs=[pl.BlockSpec(memory_space=pl.ANY)] * len(prev)
                 + [rows(D), per_seq(shift), per_seq(scale)] + [_resident(a.shape) for a in weights]
                 + [per_seq(conv0), per_seq(sconv0), per_seq(ssd0)],
        input_output_aliases={i: 2 + i for i in range(len(prev))},
        out_specs=out_specs,
        out_shape=out_shape,
        scratch_shapes=[pltpu.VMEM((CONV_PAD + bm, CONV_CH), F32), pltpu.VMEM((SCONV_PAD + bm, SSD_CONV_CH), F32),
                        pltpu.VMEM((SSD_STATE, SSD_INNER), F32),
                        pltpu.VMEM((SUBLANES - 1, CONV_PAD + bm - SUBLANES, CONV_CH), F32)],
        compiler_params=_params(("arbitrary", "arbitrary")),
        name="mix_front",
    )(*prev, x, shift, scale, *weights, conv0, sconv0, ssd0)


def _lambda(lq_ref, lk_ref, lam0):
    lq, lk = lq_ref[...], lk_ref[...]
    e0 = jnp.exp(jnp.sum(lq[0:1] * lk[0:1], axis=-1, keepdims=True))
    e1 = jnp.exp(jnp.sum(lq[1:2] * lk[1:2], axis=-1, keepdims=True))
    return e0 - e1 + lam0


def _attn_prompt_kernel(q_ref, k_ref, vt_ref, ext_ref, corr_ref, lq_ref, lk_ref, o_ref,
                        qa_s, acc_s, m_s, kn_s, sa_s, sb_s, *, lam0):
    blk, qw = ATT_BLOCK, ATT_QBLOCK
    per_q = qw // blk
    qi = pl.program_id(1)
    first = lax.broadcasted_iota(jnp.int32, (blk, LANES), 1) < ATT_QK_DIM
    lane_q = lax.broadcasted_iota(jnp.int32, (qw, LANES), 1)
    first_q = lane_q < ATT_QK_DIM
    ones_from = lambda c: jnp.where((lane_q >= c) & (lane_q < c + BIAS_TERMS), 1.0, 0.0).astype(BF16)
    head_cols = lambda h: slice(h * ATT_V_DIM, (h + 1) * ATT_V_DIM)
    qk_shift = ATT_QK_DIM.bit_length() - 1
    sel = (jnp.right_shift(lax.broadcasted_iota(jnp.int32, (ATT_WIDTH, LANES), 0), qk_shift)
           == lax.broadcasted_iota(jnp.int32, (ATT_WIDTH, LANES), 1)).astype(BF16)

    def max_sq_norm(x):
        xf = x.astype(F32)
        return jnp.max(_dot((xf * xf).astype(BF16), sel), axis=0, keepdims=True)

    @pl.when(qi == 0)
    def _():
        def body(t, mx):
            return jnp.maximum(mx, max_sq_norm(k_ref[pl.ds(pl.multiple_of(t * blk, blk), blk), :]))
        kn_s[0:1, :] = lax.fori_loop(0, k_ref.shape[0] // blk, body, jnp.zeros((1, LANES), F32))

    q = q_ref[...]
    for h in range(ATT_HEADS):
        qa_s[2 * h] = jnp.where(first_q, q[:, head_cols(h)], ones_from(ATT_QK_DIM))
        qa_s[2 * h + 1] = jnp.where(first_q, ones_from(0), q[:, head_cols(h)])
    acc_s[...] = jnp.zeros_like(acc_s)
    m_s[...] = jnp.full(m_s.shape, NEG_BIG, F32)
    ones_rows = jnp.ones((ATT_ONES_ROWS, blk), BF16)

    slot_of = lambda pos, t, c, nt: 2 * (pos * nt + t) + c

    def score_jobs(kjs, heads, s_ref):
        def job(pos, h, t, c):
            kh = k_ref[pl.ds(pl.multiple_of(kjs[t] * blk, blk), blk), head_cols(h)]
            ka = jnp.where(first, kh, ext_ref[h, 0]) if c == 0 else jnp.where(first, ext_ref[h, 1], kh)
            s_ref[slot_of(pos, t, c, len(kjs))] = _dot_nt(ka, qa_s[2 * h + c])
        return [functools.partial(job, pos, h, t, c)
                for pos, h in enumerate(heads) for c in range(2) for t in range(len(kjs))]

    def fold_jobs(kjs, heads, s_ref, diagonal=None):
        k0s = [pl.multiple_of(kj * blk, blk) for kj in kjs]
        nt = len(k0s)

        def job(pos, h, c):
            bases = [(ALIBI_SLOPES[h] * LOG2E) * k0.astype(F32) for k0 in k0s]
            vt = jnp.concatenate([vt_ref[head_cols(h), pl.ds(k0, blk)] for k0 in k0s], axis=1)
            vt = jnp.concatenate([vt, jnp.concatenate([ones_rows] * nt, axis=1)], axis=0)
            sc = [s_ref[slot_of(pos, t, c, nt)] for t in range(nt)]
            if diagonal is not None:
                sc = [x + (-2.0 * ALIBI_SLOPES[h] * LOG2E) * corr_ref[diagonal] for x in sc]
            m_old = m_s[2 * h + c:2 * h + c + 1, :]
            m_new = m_old
            for t in range(nt):
                m_new = jnp.maximum(m_new, jnp.max(sc[t], axis=0, keepdims=True) + bases[t])
            m_s[2 * h + c:2 * h + c + 1, :] = m_new
            p = jnp.concatenate([jnp.exp2(sc[t] - (m_new - bases[t])).astype(BF16) for t in range(nt)], axis=0)
            acc_s[2 * h + c] = jnp.exp2(m_old - m_new) * acc_s[2 * h + c] + _dot(vt, p)
        return [functools.partial(job, pos, h, c) for pos, h in enumerate(heads) for c in range(2)]

    def run(jobs):
        for job in jobs:
            job()

    def interleave(ahead, folds):
        per = len(ahead) // len(folds)
        for f, job in enumerate(folds):
            run(ahead[f * per:(f + 1) * per])
            job()

    bufs = (sa_s, sb_s)
    all_heads = range(ATT_HEADS)
    run(score_jobs([per_q * qi], all_heads, bufs[0]))
    for d in range(per_q):
        ahead = score_jobs([per_q * qi + d + 1], all_heads, bufs[(d + 1) % 2]) if d + 1 < per_q else []
        folds = fold_jobs([per_q * qi + d], all_heads, bufs[d % 2], diagonal=d)
        if ahead:
            interleave(ahead, folds)
        else:
            run(folds)

    bound = jnp.sqrt(max_sq_norm(q)) * jnp.sqrt(kn_s[0:1, :]) * NORM_SLACK
    top = per_q * qi
    lo, limit = [], top
    for h in range(ATT_HEADS):
        gap = None
        for c in range(2):
            g = bound[:, 2 * h + c:2 * h + c + 1] - jnp.min(m_s[2 * h + c:2 * h + c + 1, :], axis=1, keepdims=True)
            gap = g if gap is None else jnp.maximum(gap, g)
        x = (-ATT_SKIP_LOG2 - gap) / (ALIBI_SLOPES[h] * LOG2E)
        need = jnp.ceil((x - (blk - 1)) / blk)
        need = jnp.clip(jnp.where(need == need, need, 0.0), 0.0, top.astype(F32))
        limit = jnp.minimum(limit, need.astype(jnp.int32)[0, 0])
        lo.append(limit)

    hi = top
    widths = [w for _, w in ATT_PHASES]
    for i, (heads, width) in enumerate(ATT_PHASES):
        align = max(width, widths[min(i + 1, len(widths) - 1)]).bit_length() - 1
        lo_i = jnp.left_shift(jnp.right_shift(lo[heads[0]], align), align)

        n_i = jnp.right_shift(hi - lo_i, width.bit_length() - 1)
        tiles = lambda j, hi=hi, width=width: [jnp.maximum(hi - (j + 1) * width + t, 0) for t in range(width)]

        @pl.when(n_i > 0)
        def _(tiles=tiles, heads=heads):
            run(score_jobs(tiles(0), heads, sa_s))

        def body(j, carry, tiles=tiles, heads=heads):
            def trip(cur, nxt):
                interleave(score_jobs(tiles(j + 1), heads, nxt), fold_jobs(tiles(j), heads, cur))
            lax.cond(jnp.bitwise_and(j, 1) == 0, lambda: trip(sa_s, sb_s), lambda: trip(sb_s, sa_s))
            return carry

        lax.fori_loop(0, n_i, body, 0)
        hi = lo_i

    lam = _lambda(lq_ref, lk_ref, lam0)
    for h in range(ATT_HEADS):
        a1, a2 = acc_s[2 * h], acc_s[2 * h + 1]
        o_t = (a1[:ATT_V_DIM] / a1[ATT_V_DIM:ATT_V_DIM + 1]
               - lam * (a2[:ATT_V_DIM] / a2[ATT_V_DIM:ATT_V_DIM + 1]))
        o_ref[:, head_cols(h)] = o_t.T


def _attn_tables():
    blk, qw = ATT_BLOCK, ATT_QBLOCK
    pos = np.arange(blk, dtype=np.float64)
    ext = np.zeros((ATT_HEADS, 2, blk, LANES), np.float32)
    to_bf16 = lambda x: x.astype(ml_dtypes.bfloat16).astype(np.float64)
    for h, slope in enumerate(ALIBI_SLOPES):
        rest = slope * LOG2E * pos
        for t in range(BIAS_TERMS):
            term = to_bf16(rest)
            ext[h, 0, :, ATT_QK_DIM + t] = term
            ext[h, 1, :, t] = term
            rest = rest - term
    ahead = np.zeros((qw // blk, blk, qw), np.float32)
    qpos = np.arange(qw, dtype=np.float64)[None, :]
    for d in range(qw // blk):
        kpos = (d * blk + pos)[:, None]
        ahead[d] = np.where((kpos // CHUNK) <= (qpos // CHUNK), np.maximum(kpos - qpos, 0.0), -NEG_BIG)
    return jnp.asarray(ext, BF16), jnp.asarray(ahead, F32)


def _attn_prompt_call(qb, kb, vb, lq, lk, lam0):
    b, S, W = qb.shape
    blk, qw = ATT_BLOCK, ATT_QBLOCK
    vt = jnp.swapaxes(vb, 1, 2)
    ext, ahead = _attn_tables()
    whole = lambda shape: pl.BlockSpec((None,) + shape, lambda s, i: (s, 0, 0), pipeline_mode=pl.Buffered(1))
    return pl.pallas_call(
        functools.partial(_attn_prompt_kernel, lam0=lam0),
        grid=(b, S // qw),
        in_specs=[pl.BlockSpec((None, qw, W), lambda s, i: (s, i, 0)), whole((S, W)), whole((W, S)),
                  _resident(ext.shape), _resident(ahead.shape), _resident(lq.shape), _resident(lk.shape)],
        out_specs=pl.BlockSpec((None, qw, W), lambda s, i: (s, i, 0)),
        out_shape=jax.ShapeDtypeStruct((b, S, W), F32),
        scratch_shapes=[pltpu.VMEM((2 * ATT_HEADS, qw, LANES), BF16),
                        pltpu.VMEM((2 * ATT_HEADS, ATT_V_DIM + ATT_ONES_ROWS, qw), F32),
                        pltpu.VMEM((2 * ATT_HEADS, qw), F32), pltpu.VMEM((SUBLANES, LANES), F32),
                        pltpu.VMEM((2 * ATT_HEADS, blk, qw), F32), pltpu.VMEM((2 * ATT_HEADS, blk, qw), F32)],
        compiler_params=_params(("arbitrary", "arbitrary")),
        name="attn_prompt",
    )(qb, kb, vt, ext, ahead, lq, lk)


def _attn_sample_kernel(q_ref, kn_ref, vn_ref, kc_ref, vc_ref, lq_ref, lk_ref, o_ref, *, lam0):
    ls, past = q_ref.shape[0], kc_ref.shape[0]
    lam = _lambda(lq_ref, lk_ref, lam0)
    lane = lax.broadcasted_iota(jnp.int32, (ls, LANES), 1)
    first = lane < ATT_QK_DIM

    def bias_mask(nk, k_first):
        qpos = past + lax.broadcasted_iota(jnp.int32, (ls, nk), 0)
        kpos = k_first + lax.broadcasted_iota(jnp.int32, (ls, nk), 1)
        dist = jnp.abs(qpos - kpos).astype(F32)
        visible = jnp.right_shift(kpos, CHUNK_SHIFT) <= jnp.right_shift(qpos, CHUNK_SHIFT)
        return dist, visible

    dist_c, vis_c = bias_mask(past, 0)
    dist_n, vis_n = bias_mask(ls, past)
    outs = []
    for h in range(ATT_HEADS):
        cols = slice(h * ATT_V_DIM, (h + 1) * ATT_V_DIM)
        qh = q_ref[:, cols]
        kc, kn = kc_ref[:, h, :].astype(BF16), kn_ref[:, cols]
        vc, vn = vc_ref[:, h, :].astype(BF16), vn_ref[:, cols]
        o = []
        for c in range(2):
            qm = jnp.where(first == (c == 0), qh, jnp.zeros_like(qh))
            sc = jnp.where(vis_c, _dot_nt(qm, kc) - (ALIBI_SLOPES[h] * LOG2E) * dist_c, NEG_BIG)
            sn = jnp.where(vis_n, _dot_nt(qm, kn) - (ALIBI_SLOPES[h] * LOG2E) * dist_n, NEG_BIG)
            m = jnp.maximum(jnp.max(sc, axis=-1, keepdims=True), jnp.max(sn, axis=-1, keepdims=True))
            pc, pn = jnp.exp2(sc - m), jnp.exp2(sn - m)
            l = jnp.sum(pc, axis=-1, keepdims=True) + jnp.sum(pn, axis=-1, keepdims=True)
            o.append((_dot(pc.astype(BF16), vc) + _dot(pn.astype(BF16), vn)) / l)
        outs.append(o[0] - lam * o[1])
    o_ref[...] = jnp.concatenate(outs, axis=1)


def _attn_sample_call(qb, kb, vb, cache_k, cache_v, layer, lq, lk, lam0):
    b, ls, W = qb.shape
    past = cache_k.shape[2]
    new = pl.BlockSpec((None, ls, W), lambda s: (s, 0, 0))
    old = pl.BlockSpec((None, None, past, ATT_HEADS, ATT_V_DIM), lambda s: (layer, s, 0, 0, 0))
    return pl.pallas_call(
        functools.partial(_attn_sample_kernel, lam0=lam0),
        grid=(b,),
        in_specs=[new, new, new, old, old, _resident(lq.shape), _resident(lk.shape)],
        out_specs=new,
        out_shape=jax.ShapeDtypeStruct((b, ls, W), F32),
        compiler_params=_params(("arbitrary",)),
        name="attn_sample",
    )(qb, kb, vb, cache_k, cache_v, lq, lk)


def _prep_layer(l, w_in, b_gate, conv_dw_w, conv_dw_b, conv_ln_g, conv_ln_b, w_br_conv, ssd_conv_w, ssd_conv_b,
                ssd_dt_bias, ssd_A_log, ssd_D, ssd_norm_g, w_br_ssd):
    splits = (2 * CONV_CH, SSD_INNER, SSD_CONV_CH, SSD_HEADS, 3 * ATT_WIDTH, N_BRANCH * D_MODEL)
    pts = np.cumsum(splits)[:-1].tolist()
    w_glu, w_z, w_xbc, w_dt, w_qkv, w_gate = (p.astype(BF16) for p in jnp.split(w_in[l], pts, axis=1))
    rep = lambda a: jnp.repeat(a, SSD_HEAD_DIM, axis=-1)
    return dict(
        w_glu=w_glu, w_z=w_z, w_xbc=w_xbc, w_dt_rep=rep(w_dt), w_dt_t=w_dt.T, w_qkv=w_qkv, w_gate=w_gate,
        b_gate=b_gate[l].reshape(1, N_BRANCH * D_MODEL),
        conv_w=conv_dw_w[l], conv_b=conv_dw_b[l][None], ln_g=conv_ln_g[l][None], ln_b=conv_ln_b[l][None],
        w_br_conv=w_br_conv[l].astype(BF16),
        sconv_w=ssd_conv_w[l], sconv_b=ssd_conv_b[l][None],
        dtb_rep=rep(ssd_dt_bias[l])[None], dtb_col=ssd_dt_bias[l][:, None],
        alog_rep=rep(ssd_A_log[l])[None], alog_col=ssd_A_log[l][:, None],
        d_rep=rep(ssd_D[l])[None], norm_g=ssd_norm_g[l][None], w_br_ssd=w_br_ssd[l].astype(BF16))


def _state_in(conv, sconv, ssd):
    conv = jnp.pad(conv, ((0, 0), (CONV_PAD - (CONV_WIDTH - 1), 0), (0, 0)))
    sconv = jnp.pad(sconv, ((0, 0), (SCONV_PAD - (SSD_CONV_WIDTH - 1), 0), (0, 0)))
    return conv, sconv, ssd


def kernel(x_prompt, x_sample, cache_attn_k, cache_attn_v, state_conv, state_ssd_conv, state_ssd, c_prompt, c_sample,
           w_ada, b_ada, norm_pre, norm_post, w_ffn_in, w_ffn_out, w_in, b_gate, conv_dw_w, conv_dw_b, conv_ln_g,
           conv_ln_b, w_br_conv, ssd_conv_w, ssd_conv_b, ssd_dt_bias, ssd_A_log, ssd_D, ssd_norm_g, w_br_ssd,
           lambda_q, lambda_k, attn_subln_g, w_br_attn, w_mix_out):
    depth = w_ada.shape[0]
    bp, S, D = x_prompt.shape
    bs, ls, _ = x_sample.shape

    mods = _ada_call(jnp.concatenate([c_prompt, c_sample], axis=0), w_ada, b_ada)
    mods = mods.reshape(depth, bp + bs, N_SUB, 3, D)

    zeros = _state_in(jnp.zeros((bp, CONV_WIDTH - 1, CONV_CH), F32), jnp.zeros((bp, SSD_CONV_WIDTH - 1, SSD_CONV_CH), F32),
                      jnp.zeros((bp, SSD_HEADS, SSD_HEAD_DIM, SSD_STATE), F32))
    xp = x_prompt
    xs = x_sample.reshape(1, bs * ls, D)
    st_p, st_s = [], []
    kv_p = kv_s = None
    for l in range(depth):
        lam0 = _lambda_init(l)
        w = _prep_layer(l, w_in, b_gate, conv_dw_w, conv_dw_b, conv_ln_g, conv_ln_b, w_br_conv, ssd_conv_w, ssd_conv_b,
                        ssd_dt_bias, ssd_A_log, ssd_D, ssd_norm_g, w_br_ssd)
        wf_in, wf_out = w_ffn_in[l].astype(BF16), w_ffn_out[l].astype(BF16)
        w_bra, w_mix = w_br_attn[l].astype(BF16), w_mix_out[l].astype(BF16)
        gpre = lambda s: norm_pre[l, s][None]
        gpost = lambda s: norm_post[l, s][None]
        subg = attn_subln_g[l][None]
        mp = lambda s, k: mods[l, :bp, s, k][:, None, :]
        ms_seq = lambda s, k: mods[l, bp:, s, k][:, None, :]
        ms_tok = lambda s, k: jnp.repeat(mods[l, bp:, s, k], ls, axis=0)[None]

        xp = _ffn_call(xp, (mp(0, 0), mp(0, 1), mp(0, 2)), gpre(0), gpost(0), wf_in[0], wf_out[0])
        p, g2, *kv_p, qb, kb, vb, convn, sconvn, ssdn = _mix_call(xp, mp(1, 0), mp(1, 1), gpre(1), w, *zeros,
                                                                  l, depth, kv_p)
        o = _attn_prompt_call(qb, kb, vb, lambda_q[l], lambda_k[l], lam0)
        mix = dict(p=p, g2=g2, o=o, subg=subg, w_br_attn=w_bra, w_mix_out=w_mix, gate=mp(1, 2), gpost=gpost(1))
        xp = _ffn_call(xp, (mp(2, 0), mp(2, 1), mp(2, 2)), gpre(2), gpost(2), wf_in[1], wf_out[1], mix=mix, lam0=lam0)
        st_p.append((convn, sconvn, ssdn))

        xs = _ffn_call(xs, (ms_tok(0, 0), ms_tok(0, 1), ms_tok(0, 2)), gpre(0), gpost(0), wf_in[0], wf_out[0])
        states = _state_in(state_conv[l], state_ssd_conv[l], state_ssd[l])
        p, g2, *kv_s, qb, kb, vb, convn, sconvn, ssdn = _mix_call(xs.reshape(bs, ls, D), ms_seq(1, 0), ms_seq(1, 1),
                                                                  gpre(1), w, *states, l, depth, kv_s)
        o = _attn_sample_call(qb, kb, vb, cache_attn_k, cache_attn_v, l, lambda_q[l], lambda_k[l], lam0)
        flat = lambda t: t.reshape(1, bs * ls, t.shape[-1])
        mix = dict(p=flat(p), g2=flat(g2), o=flat(o), subg=subg, w_br_attn=w_bra, w_mix_out=w_mix, gate=ms_tok(1, 2),
                   gpost=gpost(1))
        xs = _ffn_call(xs, (ms_tok(2, 0), ms_tok(2, 1), ms_tok(2, 2)), gpre(2), gpost(2), wf_in[1], wf_out[1], mix=mix,
                       lam0=lam0)
        st_s.append((convn, sconvn, ssdn))

    stack = lambda st, i: jnp.stack([s[i] for s in st])
    return (xp, xs.reshape(bs, ls, D),
            kv_p[0], kv_p[1], stack(st_p, 0), stack(st_p, 1), stack(st_p, 2),
            kv_s[0], kv_s[1], stack(st_s, 0), stack(st_s, 1), stack(st_s, 2))
```

```python
import functools
import math

import ml_dtypes
import numpy as np
import jax
import jax.numpy as jnp
from jax import lax
from jax.experimental import pallas as pl
from jax.experimental.pallas import tpu as pltpu

F32, BF16 = jnp.float32, jnp.bfloat16

D_MODEL = 1024
D_FF = 2816
CONV_CH = 512
CONV_WIDTH = 31
SSD_HEADS = 16
SSD_HEAD_DIM = 64
SSD_INNER = SSD_HEADS * SSD_HEAD_DIM
SSD_GROUPS = 2
SSD_STATE = 128
SSD_CONV_WIDTH = 4
SSD_CONV_CH = SSD_INNER + 2 * SSD_GROUPS * SSD_STATE
CHUNK = 64
CHUNK_SHIFT = 6
ATT_HEADS = 4
ATT_QK_DIM = 64
ATT_V_DIM = 2 * ATT_QK_DIM
ATT_WIDTH = ATT_HEADS * ATT_V_DIM
ATT_SCALE = ATT_QK_DIM ** -0.5
ALIBI_SLOPES = tuple(2.0 ** (-8.0 * (h + 1) / ATT_HEADS) for h in range(ATT_HEADS))
N_BRANCH = 3
N_SUB = 3
EPS = 1e-6
SUBLN_EPS = 1e-5
CONV_LN_EPS = 1e-5
NEG_BIG = -1e30
LOG2E = math.log2(math.e)
BIAS_TERMS = 3
ATT_ONES_ROWS = 16
ATT_SKIP_LOG2 = 152.0
NORM_SLACK = 1.0 + 2.0 ** -6
ATT_PHASES = (((0, 1, 2, 3), 1), ((1, 2, 3), 1), ((2, 3), 2), ((3,), 4))
assert all(len(heads) * width <= ATT_HEADS for heads, width in ATT_PHASES)

VMEM_LIMIT_BYTES = 58 * 1024 * 1024
LANES = 128
SUBLANES = 8

FFN_ROWS = 512
FFN_EDGES = (0, 768, 1792, D_FF)
MIX_COLS = 512
CONV_ROWS = 32
MIX_ROWS = 256
ATT_BLOCK = 256
ATT_QBLOCK = 512
ADA_COLS = 1152
CONV_PAD = 32
SCONV_PAD = 8


def _lambda_init(layer):
    return 0.8 - 0.6 * math.exp(-0.3 * layer)


def _params(sem):
    return pltpu.CompilerParams(dimension_semantics=sem, vmem_limit_bytes=VMEM_LIMIT_BYTES)


def _resident(shape):
    nd = len(shape)
    return pl.BlockSpec(shape, lambda *_: (0,) * nd, pipeline_mode=pl.Buffered(1))


def _dot(a, b):
    return jnp.dot(a, b, preferred_element_type=F32)


def _dot_nt(a, b):
    return lax.dot_general(a, b, (((1,), (1,)), ((), ())), preferred_element_type=F32)


def _dot_tn(a, b):
    return lax.dot_general(a, b, (((0,), (0,)), ((), ())), preferred_element_type=F32)


def _rms(x, g, eps=EPS):
    return x * lax.rsqrt(jnp.mean(x * x, axis=-1, keepdims=True) + eps) * g


def _silu(x):
    return x * jax.nn.sigmoid(x)


def _split3(a):
    hi = a.astype(BF16)
    r = a - hi.astype(F32)
    mid = r.astype(BF16)
    lo = (r - mid.astype(F32)).astype(BF16)
    return hi, mid, lo


def _ada_kernel(c_ref, w_ref, b_ref, o_ref):
    sc = _silu(c_ref[...]).astype(BF16)
    o_ref[...] = _dot(sc, w_ref[...].astype(BF16)) + b_ref[...]


def _ada_call(c_all, w_ada, b_ada):
    depth, d, n = w_ada.shape
    rows = c_all.shape[0]
    return pl.pallas_call(
        _ada_kernel,
        grid=(depth, n // ADA_COLS),
        in_specs=[pl.BlockSpec((rows, d), lambda l, j: (0, 0)),
                  pl.BlockSpec((None, d, ADA_COLS), lambda l, j: (l, 0, j)),
                  pl.BlockSpec((None, 1, ADA_COLS), lambda l, j: (l, 0, j))],
        out_specs=pl.BlockSpec((None, rows, ADA_COLS), lambda l, j: (l, 0, j)),
        out_shape=jax.ShapeDtypeStruct((depth, rows, n), F32),
        compiler_params=_params(("arbitrary", "arbitrary")),
        name="adaln",
    )(c_all, w_ada, b_ada.reshape(depth, 1, n))


def _ffn_kernel(*refs, with_mix, lam0):
    if with_mix:
        (x_ref, p_ref, g2_ref, o_ref, subg_ref, wbr_ref, wmix_ref, gate1_ref, gpost1_ref, *rest) = refs
    else:
        x_ref, *rest = refs
    shift_ref, scale_ref, gate_ref, gpre_ref, gpost_ref, win_ref, wout_ref, out_ref = rest
    x = x_ref[...]
    if with_mix:
        o = o_ref[...]
        subg = subg_ref[...]
        heads = []
        for h in range(ATT_HEADS):
            oh = o[:, h * ATT_V_DIM:(h + 1) * ATT_V_DIM]
            heads.append(_rms(oh, subg, SUBLN_EPS) * (1.0 - lam0))
        c_out = _dot(jnp.concatenate(heads, axis=1).astype(BF16), wbr_ref[...])
        merged = (p_ref[...] + g2_ref[...] * c_out).astype(BF16)
        x = x + gate1_ref[...] * _rms(_dot(merged, wmix_ref[...]), gpost1_ref[...])
    h = (_rms(x, gpre_ref[...]) * (1.0 + scale_ref[...]) + shift_ref[...]).astype(BF16)
    edges = FFN_EDGES

    def project(c):
        return (_dot(h, win_ref[:, edges[c]:edges[c + 1]]), _dot(h, win_ref[:, D_FF + edges[c]:D_FF + edges[c + 1]]))

    f, cur = None, project(0)
    for c in range(len(edges) - 1):
        nxt = project(c + 1) if c + 2 < len(edges) else None
        up, gt = cur
        part = _dot((_silu(gt) * up).astype(BF16), wout_ref[edges[c]:edges[c + 1], :])
        f = part if f is None else f + part
        cur = nxt
    out_ref[...] = x + 0.5 * gate_ref[...] * _rms(f, gpost_ref[...])


def _ffn_call(x, mod, gpre, gpost, w_in, w_out, mix=None, lam0=0.0):
    nseq, L, D = x.shape
    bm = min(FFN_ROWS, L)
    row_spec = pl.BlockSpec((None, bm, D), lambda s, i: (s, i, 0))

    def mod_spec(m):
        if m.shape[1] == 1:
            return pl.BlockSpec((None, 1, D), lambda s, i: (s, 0, 0))
        return row_spec

    args, specs = [x], [row_spec]
    if mix is not None:
        args += [mix["p"], mix["g2"], mix["o"], mix["subg"], mix["w_br_attn"], mix["w_mix_out"], mix["gate"], mix["gpost"]]
        specs += [row_spec, row_spec, pl.BlockSpec((None, bm, ATT_WIDTH), lambda s, i: (s, i, 0)),
                  _resident((1, ATT_V_DIM)), _resident((ATT_WIDTH, D)), _resident((D, D)),
                  mod_spec(mix["gate"]), _resident((1, D))]
    args += [mod[0], mod[1], mod[2], gpre, gpost, w_in, w_out]
    specs += [mod_spec(mod[0]), mod_spec(mod[1]), mod_spec(mod[2]), _resident((1, D)), _resident((1, D)),
              _resident(w_in.shape), _resident(w_out.shape)]
    return pl.pallas_call(
        functools.partial(_ffn_kernel, with_mix=mix is not None, lam0=lam0),
        grid=(nseq, L // bm),
        in_specs=specs,
        out_specs=row_spec,
        out_shape=jax.ShapeDtypeStruct(x.shape, F32),
        compiler_params=_params(("arbitrary", "arbitrary")),
        name="merge_ffn" if mix is not None else "ffn",
    )(*args)


def _interleave(ahead, jobs):
    done = 0
    for i, job in enumerate(jobs):
        upto = -(-len(ahead) * (i + 1) // len(jobs))
        for a in ahead[done:upto]:
            a()
        done = upto
        job()


def _dwconv_jobs(src_ref, first, w, b, rows, rc, shift_ref=None):
    width, ch = w.shape
    if shift_ref is not None:
        span = shift_ref.shape[1]
        for s in sorted({(first + j) % SUBLANES for j in range(width)} - {0}):
            shift_ref[s - 1] = src_ref[s:s + span, :]

    def tap(j, r0):
        a, s = divmod(first + j, SUBLANES)
        if shift_ref is None or s == 0:
            return src_ref[first + j + r0:first + j + r0 + rc, :]
        return shift_ref[s - 1, a * SUBLANES + r0:a * SUBLANES + r0 + rc, :]

    outs = []

    def job(r0):
        acc = jnp.broadcast_to(b, (rc, ch))
        for j in range(width):
            acc = acc + w[j:j + 1, :] * tap(j, r0)
        outs.append(acc)

    return [functools.partial(job, r0) for r0 in range(0, rows, rc)], outs


def _mix_kernel(*refs, bm, q, n_prev):
    (x_ref, shift_ref, scale_ref, gpre_ref,
     wglu_ref, wz_ref, wxbc_ref, wdtr_ref, wdtt_ref, wqkv_ref, wgate_ref, bgate_ref,
     cw_ref, cb_ref, lng_ref, lnb_ref, wbrc_ref,
     sw_ref, sb_ref, dtbr_ref, dtbc_ref, alr_ref, alc_ref, dr_ref, ng_ref, wbrs_ref,
     conv0_ref, sconv0_ref, ssd0_ref,
     p_ref, g2_ref, k_ref, v_ref, qb_ref, kb_ref, vb_ref, convn_ref, sconvn_ref, ssdn_ref,
     conv_s, sconv_s, state_s, shift_s) = refs[n_prev:]

    @pl.when(pl.program_id(1) == 0)
    def _():
        conv_s[0:CONV_PAD, :] = conv0_ref[...]
        sconv_s[0:SCONV_PAD, :] = sconv0_ref[...]
        state_s[...] = ssd0_ref[...].reshape(SSD_INNER, SSD_STATE).T

    h = (_rms(x_ref[...], gpre_ref[...]) * (1.0 + scale_ref[...]) + shift_ref[...]).astype(BF16)

    glu = _dot(h, wglu_ref[...])
    conv_s[CONV_PAD:CONV_PAD + bm, :] = glu[:, :CONV_CH] * jax.nn.sigmoid(glu[:, CONV_CH:])
    sconv_s[SCONV_PAD:SCONV_PAD + bm, :] = _dot(h, wxbc_ref[...])
    dt_raw_t = _dot_nt(wdtt_ref[...], h)
    slices = {}

    def project(name, w_ref):
        def job(c0):
            slices[name].append(_dot(h, w_ref[:, c0:c0 + MIX_COLS]))
        slices[name] = []
        return [functools.partial(job, c0) for c0 in range(0, w_ref.shape[1], MIX_COLS)]

    projections = (project("dt", wdtr_ref) + project("z", wz_ref) + project("qkv", wqkv_ref)
                   + project("gate", wgate_ref))
    rc = min(CONV_ROWS, bm)
    conv_jobs, conv_rows = _dwconv_jobs(conv_s, CONV_PAD - (CONV_WIDTH - 1), cw_ref[...], cb_ref[...], bm, rc, shift_s)
    sconv_jobs, sconv_rows = _dwconv_jobs(sconv_s, SCONV_PAD - (SSD_CONV_WIDTH - 1), sw_ref[...], sb_ref[...], bm, rc)
    _interleave(projections, conv_jobs + sconv_jobs)
    dt_raw, z, qkv, gate_logits = (jnp.concatenate(slices[n], axis=1) for n in ("dt", "z", "qkv", "gate"))

    k = qkv[:, ATT_WIDTH:2 * ATT_WIDTH]
    v = qkv[:, 2 * ATT_WIDTH:]
    for hd in range(ATT_HEADS):
        k_ref[:, hd, :] = k[:, hd * ATT_V_DIM:(hd + 1) * ATT_V_DIM]
        v_ref[:, hd, :] = v[:, hd * ATT_V_DIM:(hd + 1) * ATT_V_DIM]
    qb_ref[...] = (qkv[:, :ATT_WIDTH] * (ATT_SCALE * LOG2E)).astype(BF16)
    kb_ref[...] = k.astype(BF16)
    vb_ref[...] = v.astype(BF16)

    a = jnp.concatenate(conv_rows, axis=0)
    mu = jnp.mean(a, axis=-1, keepdims=True)
    ac = a - mu
    a = ac * lax.rsqrt(jnp.mean(ac * ac, axis=-1, keepdims=True) + CONV_LN_EPS) * lng_ref[...] + lnb_ref[...]
    a_out = _dot(_silu(a).astype(BF16), wbrc_ref[...])
    convn_ref[...] = conv_s[CONV_PAD + bm - (CONV_WIDTH - 1):CONV_PAD + bm, :]
    conv_s[0:CONV_PAD, :] = conv_s[bm:bm + CONV_PAD, :]

    xbc = _silu(jnp.concatenate(sconv_rows, axis=0))
    sconvn_ref[...] = sconv_s[SCONV_PAD + bm - (SSD_CONV_WIDTH - 1):SCONV_PAD + bm, :]
    sconv_s[0:SCONV_PAD, :] = sconv_s[bm:bm + SCONV_PAD, :]

    dt_e = jax.nn.softplus(dt_raw + dtbr_ref[...])
    dt_t = jax.nn.softplus(dt_raw_t + dtbc_ref[...])
    a_rep = -jnp.exp(alr_ref[...])
    a_col = -jnp.exp(alc_ref[...])
    row = lax.broadcasted_iota(jnp.int32, (q, q), 0)
    col = lax.broadcasted_iota(jnp.int32, (q, q), 1)
    causal = row >= col
    tril = causal.astype(BF16)
    triu = (row <= col).astype(BF16)
    gw = SSD_INNER // SSD_GROUPS
    ys = []
    for c in range(bm // q):
        rows = slice(c * q, (c + 1) * q)
        dt_c = dt_e[rows]
        hi, mid, lo = _split3(dt_c * a_rep)
        acum = _dot(tril, hi) + _dot(tril, mid) + _dot(tril, lo)
        hi, mid, lo = _split3(dt_t[:, rows] * a_col)
        acum_t = _dot(hi, triu) + _dot(mid, triu) + _dot(lo, triu)
        xs_c = xbc[rows, :SSD_INNER]
        xdt = xs_c * dt_c
        a_last = acum[q - 1:q, :]
        x_end = (xdt * jnp.exp(a_last - acum)).astype(BF16)
        e_in = jnp.exp(acum)
        xdt_b = xdt.astype(BF16)
        state = state_s[...]
        state_b = state.astype(BF16)
        y_parts = []
        for g in range(SSD_GROUPS):
            bg = xbc[rows, SSD_INNER + g * SSD_STATE:SSD_INNER + (g + 1) * SSD_STATE]
            cg = xbc[rows, SSD_INNER + (SSD_GROUPS + g) * SSD_STATE:SSD_INNER + (SSD_GROUPS + g + 1) * SSD_STATE]
            bg_b, cg_b = bg.astype(BF16), cg.astype(BF16)
            cb = _dot_nt(cg_b, bg_b)
            y_off = _dot(cg_b, state_b[:, g * gw:(g + 1) * gw]) * e_in[:, g * gw:(g + 1) * gw]
            y_diag = []
            for r in range(SSD_HEADS // SSD_GROUPS):
                hd = g * (SSD_HEADS // SSD_GROUPS) + r
                lo_l = hd * SSD_HEAD_DIM
                seg = acum[:, lo_l:lo_l + q] - acum_t[hd:hd + 1, :]
                m = (cb * jnp.exp(jnp.where(causal, seg, NEG_BIG))).astype(BF16)
                y_diag.append(_dot(m, xdt_b[:, lo_l:lo_l + SSD_HEAD_DIM]))
            y_parts.append(jnp.concatenate(y_diag, axis=1) + y_off)
            state_s[:, g * gw:(g + 1) * gw] = (state[:, g * gw:(g + 1) * gw] * jnp.exp(a_last[:, g * gw:(g + 1) * gw])
                                               + _dot_tn(bg_b, x_end[:, g * gw:(g + 1) * gw]))
        ys.append(jnp.concatenate(y_parts, axis=1) + dr_ref[...] * xs_c)

    @pl.when(pl.program_id(1) == pl.num_programs(1) - 1)
    def _():
        ssdn_ref[...] = state_s[...].T.reshape(SSD_HEADS, SSD_HEAD_DIM, SSD_STATE)

    y = jnp.concatenate(ys, axis=0) if len(ys) > 1 else ys[0]
    yz = y * _silu(z)
    yz = jnp.concatenate([_rms(yz[:, g * gw:(g + 1) * gw], 1.0) for g in range(SSD_GROUPS)], axis=1) * ng_ref[...]
    b_out = _dot(yz.astype(BF16), wbrs_ref[...])

    gates = jax.nn.sigmoid(gate_logits + bgate_ref[...])
    p_ref[...] = (gates[:, :D_MODEL] * a_out + gates[:, D_MODEL:2 * D_MODEL] * b_out).astype(BF16)
    g2_ref[...] = gates[:, 2 * D_MODEL:].astype(BF16)


def _mix_call(x, shift, scale, gpre, w, conv0, sconv0, ssd0, layer, depth, kv_prev):
    nseq, L, D = x.shape
    bm = min(MIX_ROWS, L)
    q = min(CHUNK, bm)
    kv_shape = jax.ShapeDtypeStruct((depth, nseq, L, ATT_HEADS, ATT_V_DIM), F32)
    kv_spec = pl.BlockSpec((None, None, bm, ATT_HEADS, ATT_V_DIM), lambda s, i: (layer, s, i, 0, 0))
    prev = list(kv_prev or ())
    rows = lambda n: pl.BlockSpec((None, bm, n), lambda s, i: (s, i, 0))
    per_seq = lambda a: pl.BlockSpec((None,) + a.shape[1:], lambda s, i: (s,) + (0,) * (len(a.shape) - 1))
    weights = [gpre, w["w_glu"], w["w_z"], w["w_xbc"], w["w_dt_rep"], w["w_dt_t"], w["w_qkv"], w["w_gate"], w["b_gate"],
               w["conv_w"], w["conv_b"], w["ln_g"], w["ln_b"], w["w_br_conv"],
               w["sconv_w"], w["sconv_b"], w["dtb_rep"], w["dtb_col"], w["alog_rep"], w["alog_col"], w["d_rep"],
               w["norm_g"], w["w_br_ssd"]]
    out_shape = [jax.ShapeDtypeStruct((nseq, L, D), BF16), jax.ShapeDtypeStruct((nseq, L, D), BF16), kv_shape, kv_shape,
                 jax.ShapeDtypeStruct((nseq, L, ATT_WIDTH), BF16), jax.ShapeDtypeStruct((nseq, L, ATT_WIDTH), BF16),
                 jax.ShapeDtypeStruct((nseq, L, ATT_WIDTH), BF16),
                 jax.ShapeDtypeStruct((nseq, CONV_WIDTH - 1, CONV_CH), F32),
                 jax.ShapeDtypeStruct((nseq, SSD_CONV_WIDTH - 1, SSD_CONV_CH), F32),
                 jax.ShapeDtypeStruct((nseq, SSD_HEADS, SSD_HEAD_DIM, SSD_STATE), F32)]
    out_specs = [rows(D), rows(D), kv_spec, kv_spec, rows(ATT_WIDTH), rows(ATT_WIDTH), rows(ATT_WIDTH),
                 per_seq(out_shape[7]), per_seq(out_shape[8]), per_seq(out_shape[9])]
    return pl.pallas_call(
        functools.partial(_mix_kernel, bm=bm, q=q, n_prev=len(prev)),
        grid=(nseq, L // bm),
        in_specs=[pl.BlockSpec(memory_space=pl.ANY)] * len(prev)
                 + [rows(D), per_seq(shift), per_seq(scale)] + [_resident(a.shape) for a in weights]
                 + [per_seq(conv0), per_seq(sconv0), per_seq(ssd0)],
        input_output_aliases={i: 2 + i for i in range(len(prev))},
        out_specs=out_specs,
        out_shape=out_shape,
        scratch_shapes=[pltpu.VMEM((CONV_PAD + bm, CONV_CH), F32), pltpu.VMEM((SCONV_PAD + bm, SSD_CONV_CH), F32),
                        pltpu.VMEM((SSD_STATE, SSD_INNER), F32),
                        pltpu.VMEM((SUBLANES - 1, CONV_PAD + bm - SUBLANES, CONV_CH), F32)],
        compiler_params=_params(("arbitrary", "arbitrary")),
        name="mix_front",
    )(*prev, x, shift, scale, *weights, conv0, sconv0, ssd0)


def _lambda(lq_ref, lk_ref, lam0):
    lq, lk = lq_ref[...], lk_ref[...]
    e0 = jnp.exp(jnp.sum(lq[0:1] * lk[0:1], axis=-1, keepdims=True))
    e1 = jnp.exp(jnp.sum(lq[1:2] * lk[1:2], axis=-1, keepdims=True))
    return e0 - e1 + lam0


def _attn_prompt_kernel(q_ref, k_ref, vt_ref, ext_ref, corr_ref, lq_ref, lk_ref, o_ref,
                        qa_s, acc_s, m_s, kn_s, sa_s, sb_s, *, lam0):
    blk, qw = ATT_BLOCK, ATT_QBLOCK
    per_q = qw // blk
    qi = pl.program_id(1)
    first = lax.broadcasted_iota(jnp.int32, (blk, LANES), 1) < ATT_QK_DIM
    lane_q = lax.broadcasted_iota(jnp.int32, (qw, LANES), 1)
    first_q = lane_q < ATT_QK_DIM
    ones_from = lambda c: jnp.where((lane_q >= c) & (lane_q < c + BIAS_TERMS), 1.0, 0.0).astype(BF16)
    head_cols = lambda h: slice(h * ATT_V_DIM, (h + 1) * ATT_V_DIM)
    qk_shift = ATT_QK_DIM.bit_length() - 1
    sel = (jnp.right_shift(lax.broadcasted_iota(jnp.int32, (ATT_WIDTH, LANES), 0), qk_shift)
           == lax.broadcasted_iota(jnp.int32, (ATT_WIDTH, LANES), 1)).astype(BF16)

    def max_sq_norm(x):
        xf = x.astype(F32)
        return jnp.max(_dot((xf * xf).astype(BF16), sel), axis=0, keepdims=True)

    @pl.when(qi == 0)
    def _():
        def body(t, mx):
            return jnp.maximum(mx, max_sq_norm(k_ref[pl.ds(pl.multiple_of(t * blk, blk), blk), :]))
        kn_s[0:1, :] = lax.fori_loop(0, k_ref.shape[0] // blk, body, jnp.zeros((1, LANES), F32))

    q = q_ref[...]
    for h in range(ATT_HEADS):
        qa_s[2 * h] = jnp.where(first_q, q[:, head_cols(h)], ones_from(ATT_QK_DIM))
        qa_s[2 * h + 1] = jnp.where(first_q, ones_from(0), q[:, head_cols(h)])
    acc_s[...] = jnp.zeros_like(acc_s)
    m_s[...] = jnp.full(m_s.shape, NEG_BIG, F32)
    ones_rows = jnp.ones((ATT_ONES_ROWS, blk), BF16)

    slot_of = lambda pos, h, t, c, nt: 2 * (h if nt == 1 else pos * nt + t) + c

    def score_jobs(kjs, heads, s_ref):
        def job(pos, h, t, c):
            kh = k_ref[pl.ds(pl.multiple_of(kjs[t] * blk, blk), blk), head_cols(h)]
            ka = jnp.where(first, kh, ext_ref[h, 0]) if c == 0 else jnp.where(first, ext_ref[h, 1], kh)
            s_ref[slot_of(pos, h, t, c, len(kjs))] = _dot_nt(ka, qa_s[2 * h + c])
        return [functools.partial(job, pos, h, t, c)
                for pos, h in enumerate(heads) for c in range(2) for t in range(len(kjs))]

    def fold_jobs(kjs, heads, s_ref, diagonal=None):
        k0s = [pl.multiple_of(kj * blk, blk) for kj in kjs]
        nt = len(k0s)

        def job(pos, h, c):
            bases = [(ALIBI_SLOPES[h] * LOG2E) * k0.astype(F32) for k0 in k0s]
            vt = jnp.concatenate([vt_ref[head_cols(h), pl.ds(k0, blk)] for k0 in k0s], axis=1)
            vt = jnp.concatenate([vt, jnp.concatenate([ones_rows] * nt, axis=1)], axis=0)
            sc = [s_ref[slot_of(pos, h, t, c, nt)] for t in range(nt)]
            if diagonal is not None:
                sc = [x + (-2.0 * ALIBI_SLOPES[h] * LOG2E) * corr_ref[diagonal] for x in sc]
            m_old = m_s[2 * h + c:2 * h + c + 1, :]
            m_new = m_old
            for t in range(nt):
                m_new = jnp.maximum(m_new, jnp.max(sc[t], axis=0, keepdims=True) + bases[t])
            m_s[2 * h + c:2 * h + c + 1, :] = m_new
            p = jnp.concatenate([jnp.exp2(sc[t] - (m_new - bases[t])).astype(BF16) for t in range(nt)], axis=0)
            acc_s[2 * h + c] = jnp.exp2(m_old - m_new) * acc_s[2 * h + c] + _dot(vt, p)
        return [functools.partial(job, pos, h, c) for pos, h in enumerate(heads) for c in range(2)]

    def run(jobs):
        for job in jobs:
            job()

    def interleave(ahead, folds):
        per = len(ahead) // len(folds)
        for f, job in enumerate(folds):
            run(ahead[f * per:(f + 1) * per])
            job()

    bufs = (sa_s, sb_s)
    all_heads = range(ATT_HEADS)
    run(score_jobs([per_q * qi], all_heads, bufs[0]))
    for d in range(per_q):
        nxt = per_q * qi + d + 1 if d + 1 < per_q else jnp.maximum(per_q * qi - 1, 0)
        interleave(score_jobs([nxt], all_heads, bufs[(d + 1) % 2]),
                   fold_jobs([per_q * qi + d], all_heads, bufs[d % 2], diagonal=d))

    bound = jnp.sqrt(max_sq_norm(q)) * jnp.sqrt(kn_s[0:1, :]) * NORM_SLACK
    top = per_q * qi
    lo, limit = [], top
    for h in range(ATT_HEADS):
        gap = None
        for c in range(2):
            g = bound[:, 2 * h + c:2 * h + c + 1] - jnp.min(m_s[2 * h + c:2 * h + c + 1, :], axis=1, keepdims=True)
            gap = g if gap is None else jnp.maximum(gap, g)
        x = (-ATT_SKIP_LOG2 - gap) / (ALIBI_SLOPES[h] * LOG2E)
        need = jnp.ceil((x - (blk - 1)) / blk)
        need = jnp.clip(jnp.where(need == need, need, 0.0), 0.0, top.astype(F32))
        limit = jnp.minimum(limit, need.astype(jnp.int32)[0, 0])
        lo.append(limit)

    hi = top
    steps = per_q
    widths = [w for _, w in ATT_PHASES]
    for i, (heads, width) in enumerate(ATT_PHASES):
        align = max(width, widths[min(i + 1, len(widths) - 1)]).bit_length() - 1
        lo_i = jnp.left_shift(jnp.right_shift(lo[heads[0]], align), align)

        n_i = jnp.right_shift(hi - lo_i, width.bit_length() - 1)
        tiles = lambda j, hi=hi, width=width: [jnp.maximum(hi - (j + 1) * width + t, 0) for t in range(width)]

        if width == 1:
            first_buf = steps
            steps = steps + n_i
        else:
            first_buf = 0

            @pl.when(n_i > 0)
            def _(tiles=tiles, heads=heads):
                run(score_jobs(tiles(0), heads, bufs[0]))

        def body(j, carry, tiles=tiles, heads=heads, first_buf=first_buf):
            def trip(cur, nxt):
                interleave(score_jobs(tiles(j + 1), heads, nxt), fold_jobs(tiles(j), heads, cur))
            lax.cond(jnp.bitwise_and(j + first_buf, 1) == 0, lambda: trip(bufs[0], bufs[1]), lambda: trip(bufs[1], bufs[0]))
            return carry

        lax.fori_loop(0, n_i, body, 0)
        hi = lo_i

    lam = _lambda(lq_ref, lk_ref, lam0)
    for h in range(ATT_HEADS):
        a1, a2 = acc_s[2 * h], acc_s[2 * h + 1]
        o_t = (a1[:ATT_V_DIM] / a1[ATT_V_DIM:ATT_V_DIM + 1]
               - lam * (a2[:ATT_V_DIM] / a2[ATT_V_DIM:ATT_V_DIM + 1]))
        o_ref[:, head_cols(h)] = o_t.T


def _attn_tables():
    blk, qw = ATT_BLOCK, ATT_QBLOCK
    pos = np.arange(blk, dtype=np.float64)
    ext = np.zeros((ATT_HEADS, 2, blk, LANES), np.float32)
    to_bf16 = lambda x: x.astype(ml_dtypes.bfloat16).astype(np.float64)
    for h, slope in enumerate(ALIBI_SLOPES):
        rest = slope * LOG2E * pos
        for t in range(BIAS_TERMS):
            term = to_bf16(rest)
            ext[h, 0, :, ATT_QK_DIM + t] = term
            ext[h, 1, :, t] = term
            rest = rest - term
    ahead = np.zeros((qw // blk, blk, qw), np.float32)
    qpos = np.arange(qw, dtype=np.float64)[None, :]
    for d in range(qw // blk):
        kpos = (d * blk + pos)[:, None]
        ahead[d] = np.where((kpos // CHUNK) <= (qpos // CHUNK), np.maximum(kpos - qpos, 0.0), -NEG_BIG)
    return jnp.asarray(ext, BF16), jnp.asarray(ahead, F32)


def _attn_prompt_call(qb, kb, vb, lq, lk, lam0):
    b, S, W = qb.shape
    blk, qw = ATT_BLOCK, ATT_QBLOCK
    vt = jnp.swapaxes(vb, 1, 2)
    ext, ahead = _attn_tables()
    whole = lambda shape: pl.BlockSpec((None,) + shape, lambda s, i: (s, 0, 0), pipeline_mode=pl.Buffered(1))
    return pl.pallas_call(
        functools.partial(_attn_prompt_kernel, lam0=lam0),
        grid=(b, S // qw),
        in_specs=[pl.BlockSpec((None, qw, W), lambda s, i: (s, i, 0)), whole((S, W)), whole((W, S)),
                  _resident(ext.shape), _resident(ahead.shape), _resident(lq.shape), _resident(lk.shape)],
        out_specs=pl.BlockSpec((None, qw, W), lambda s, i: (s, i, 0)),
        out_shape=jax.ShapeDtypeStruct((b, S, W), F32),
        scratch_shapes=[pltpu.VMEM((2 * ATT_HEADS, qw, LANES), BF16),
                        pltpu.VMEM((2 * ATT_HEADS, ATT_V_DIM + ATT_ONES_ROWS, qw), F32),
                        pltpu.VMEM((2 * ATT_HEADS, qw), F32), pltpu.VMEM((SUBLANES, LANES), F32),
                        pltpu.VMEM((2 * ATT_HEADS, blk, qw), F32), pltpu.VMEM((2 * ATT_HEADS, blk, qw), F32)],
        compiler_params=_params(("arbitrary", "arbitrary")),
        name="attn_prompt",
    )(qb, kb, vt, ext, ahead, lq, lk)


def _attn_sample_kernel(q_ref, kn_ref, vn_ref, kc_ref, vc_ref, lq_ref, lk_ref, o_ref, *, lam0):
    ls, past = q_ref.shape[0], kc_ref.shape[0]
    lam = _lambda(lq_ref, lk_ref, lam0)
    lane = lax.broadcasted_iota(jnp.int32, (ls, LANES), 1)
    first = lane < ATT_QK_DIM

    def bias_mask(nk, k_first):
        qpos = past + lax.broadcasted_iota(jnp.int32, (ls, nk), 0)
        kpos = k_first + lax.broadcasted_iota(jnp.int32, (ls, nk), 1)
        dist = jnp.abs(qpos - kpos).astype(F32)
        visible = jnp.right_shift(kpos, CHUNK_SHIFT) <= jnp.right_shift(qpos, CHUNK_SHIFT)
        return dist, visible

    dist_c, vis_c = bias_mask(past, 0)
    dist_n, vis_n = bias_mask(ls, past)
    outs = []
    for h in range(ATT_HEADS):
        cols = slice(h * ATT_V_DIM, (h + 1) * ATT_V_DIM)
        qh = q_ref[:, cols]
        kc, kn = kc_ref[:, h, :].astype(BF16), kn_ref[:, cols]
        vc, vn = vc_ref[:, h, :].astype(BF16), vn_ref[:, cols]
        o = []
        for c in range(2):
            qm = jnp.where(first == (c == 0), qh, jnp.zeros_like(qh))
            sc = jnp.where(vis_c, _dot_nt(qm, kc) - (ALIBI_SLOPES[h] * LOG2E) * dist_c, NEG_BIG)
            sn = jnp.where(vis_n, _dot_nt(qm, kn) - (ALIBI_SLOPES[h] * LOG2E) * dist_n, NEG_BIG)
            m = jnp.maximum(jnp.max(sc, axis=-1, keepdims=True), jnp.max(sn, axis=-1, keepdims=True))
            pc, pn = jnp.exp2(sc - m), jnp.exp2(sn - m)
            l = jnp.sum(pc, axis=-1, keepdims=True) + jnp.sum(pn, axis=-1, keepdims=True)
            o.append((_dot(pc.astype(BF16), vc) + _dot(pn.astype(BF16), vn)) / l)
        outs.append(o[0] - lam * o[1])
    o_ref[...] = jnp.concatenate(outs, axis=1)


def _attn_sample_call(qb, kb, vb, cache_k, cache_v, layer, lq, lk, lam0):
    b, ls, W = qb.shape
    past = cache_k.shape[2]
    new = pl.BlockSpec((None, ls, W), lambda s: (s, 0, 0))
    old = pl.BlockSpec((None, None, past, ATT_HEADS, ATT_V_DIM), lambda s: (layer, s, 0, 0, 0))
    return pl.pallas_call(
        functools.partial(_attn_sample_kernel, lam0=lam0),
        grid=(b,),
        in_specs=[new, new, new, old, old, _resident(lq.shape), _resident(lk.shape)],
        out_specs=new,
        out_shape=jax.ShapeDtypeStruct((b, ls, W), F32),
        compiler_params=_params(("arbitrary",)),
        name="attn_sample",
    )(qb, kb, vb, cache_k, cache_v, lq, lk)


def _prep_layer(l, w_in, b_gate, conv_dw_w, conv_dw_b, conv_ln_g, conv_ln_b, w_br_conv, ssd_conv_w, ssd_conv_b,
                ssd_dt_bias, ssd_A_log, ssd_D, ssd_norm_g, w_br_ssd):
    splits = (2 * CONV_CH, SSD_INNER, SSD_CONV_CH, SSD_HEADS, 3 * ATT_WIDTH, N_BRANCH * D_MODEL)
    pts = np.cumsum(splits)[:-1].tolist()
    w_glu, w_z, w_xbc, w_dt, w_qkv, w_gate = (p.astype(BF16) for p in jnp.split(w_in[l], pts, axis=1))
    rep = lambda a: jnp.repeat(a, SSD_HEAD_DIM, axis=-1)
    return dict(
        w_glu=w_glu, w_z=w_z, w_xbc=w_xbc, w_dt_rep=rep(w_dt), w_dt_t=w_dt.T, w_qkv=w_qkv, w_gate=w_gate,
        b_gate=b_gate[l].reshape(1, N_BRANCH * D_MODEL),
        conv_w=conv_dw_w[l], conv_b=conv_dw_b[l][None], ln_g=conv_ln_g[l][None], ln_b=conv_ln_b[l][None],
        w_br_conv=w_br_conv[l].astype(BF16),
        sconv_w=ssd_conv_w[l], sconv_b=ssd_conv_b[l][None],
        dtb_rep=rep(ssd_dt_bias[l])[None], dtb_col=ssd_dt_bias[l][:, None],
        alog_rep=rep(ssd_A_log[l])[None], alog_col=ssd_A_log[l][:, None],
        d_rep=rep(ssd_D[l])[None], norm_g=ssd_norm_g[l][None], w_br_ssd=w_br_ssd[l].astype(BF16))


def _state_in(conv, sconv, ssd):
    conv = jnp.pad(conv, ((0, 0), (CONV_PAD - (CONV_WIDTH - 1), 0), (0, 0)))
    sconv = jnp.pad(sconv, ((0, 0), (SCONV_PAD - (SSD_CONV_WIDTH - 1), 0), (0, 0)))
    return conv, sconv, ssd


def kernel(x_prompt, x_sample, cache_attn_k, cache_attn_v, state_conv, state_ssd_conv, state_ssd, c_prompt, c_sample,
           w_ada, b_ada, norm_pre, norm_post, w_ffn_in, w_ffn_out, w_in, b_gate, conv_dw_w, conv_dw_b, conv_ln_g,
           conv_ln_b, w_br_conv, ssd_conv_w, ssd_conv_b, ssd_dt_bias, ssd_A_log, ssd_D, ssd_norm_g, w_br_ssd,
           lambda_q, lambda_k, attn_subln_g, w_br_attn, w_mix_out):
    depth = w_ada.shape[0]
    bp, S, D = x_prompt.shape
    bs, ls, _ = x_sample.shape

    mods = _ada_call(jnp.concatenate([c_prompt, c_sample], axis=0), w_ada, b_ada)
    mods = mods.reshape(depth, bp + bs, N_SUB, 3, D)

    zeros = _state_in(jnp.zeros((bp, CONV_WIDTH - 1, CONV_CH), F32), jnp.zeros((bp, SSD_CONV_WIDTH - 1, SSD_CONV_CH), F32),
                      jnp.zeros((bp, SSD_HEADS, SSD_HEAD_DIM, SSD_STATE), F32))
    xp = x_prompt
    xs = x_sample.reshape(1, bs * ls, D)
    st_p, st_s = [], []
    kv_p = kv_s = None
    for l in range(depth):
        lam0 = _lambda_init(l)
        w = _prep_layer(l, w_in, b_gate, conv_dw_w, conv_dw_b, conv_ln_g, conv_ln_b, w_br_conv, ssd_conv_w, ssd_conv_b,
                        ssd_dt_bias, ssd_A_log, ssd_D, ssd_norm_g, w_br_ssd)
        wf_in, wf_out = w_ffn_in[l].astype(BF16), w_ffn_out[l].astype(BF16)
        w_bra, w_mix = w_br_attn[l].astype(BF16), w_mix_out[l].astype(BF16)
        gpre = lambda s: norm_pre[l, s][None]
        gpost = lambda s: norm_post[l, s][None]
        subg = attn_subln_g[l][None]
        mp = lambda s, k: mods[l, :bp, s, k][:, None, :]
        ms_seq = lambda s, k: mods[l, bp:, s, k][:, None, :]
        ms_tok = lambda s, k: jnp.repeat(mods[l, bp:, s, k], ls, axis=0)[None]

        xp = _ffn_call(xp, (mp(0, 0), mp(0, 1), mp(0, 2)), gpre(0), gpost(0), wf_in[0], wf_out[0])
        p, g2, *kv_p, qb, kb, vb, convn, sconvn, ssdn = _mix_call(xp, mp(1, 0), mp(1, 1), gpre(1), w, *zeros,
                                                                  l, depth, kv_p)
        o = _attn_prompt_call(qb, kb, vb, lambda_q[l], lambda_k[l], lam0)
        mix = dict(p=p, g2=g2, o=o, subg=subg, w_br_attn=w_bra, w_mix_out=w_mix, gate=mp(1, 2), gpost=gpost(1))
        xp = _ffn_call(xp, (mp(2, 0), mp(2, 1), mp(2, 2)), gpre(2), gpost(2), wf_in[1], wf_out[1], mix=mix, lam0=lam0)
        st_p.append((convn, sconvn, ssdn))

        xs = _ffn_call(xs, (ms_tok(0, 0), ms_tok(0, 1), ms_tok(0, 2)), gpre(0), gpost(0), wf_in[0], wf_out[0])
        states = _state_in(state_conv[l], state_ssd_conv[l], state_ssd[l])
        p, g2, *kv_s, qb, kb, vb, convn, sconvn, ssdn = _mix_call(xs.reshape(bs, ls, D), ms_seq(1, 0), ms_seq(1, 1),
                                                                  gpre(1), w, *states, l, depth, kv_s)
        o = _attn_sample_call(qb, kb, vb, cache_attn_k, cache_attn_v, l, lambda_q[l], lambda_k[l], lam0)
        flat = lambda t: t.reshape(1, bs * ls, t.shape[-1])
        mix = dict(p=flat(p), g2=flat(g2), o=flat(o), subg=subg, w_br_attn=w_bra, w_mix_out=w_mix, gate=ms_tok(1, 2),
                   gpost=gpost(1))
        xs = _ffn_call(xs, (ms_tok(2, 0), ms_tok(2, 1), ms_tok(2, 2)), gpre(2), gpost(2), wf_in[1], wf_out[1], mix=mix,
                       lam0=lam0)
        st_s.append((convn, sconvn, ssdn))

    stack = lambda st, i: jnp.stack([s[i] for s in st])
    return (xp, xs.reshape(bs, ls, D),
            kv_p[0], kv_p[1], stack(st_p, 0), stack(st_p, 1), stack(st_p, 2),
            kv_s[0], kv_s[1], stack(st_s, 0), stack(st_s, 1), stack(st_s, 2))
```

```python
import functools
import math

import ml_dtypes
import numpy as np
import jax
import jax.numpy as jnp
from jax import lax
from jax.experimental import pallas as pl
from jax.experimental.pallas import tpu as pltpu

F32, BF16 = jnp.float32, jnp.bfloat16

D_MODEL = 1024
D_FF = 2816
CONV_CH = 512
CONV_WIDTH = 31
SSD_HEADS = 16
SSD_HEAD_DIM = 64
SSD_INNER = SSD_HEADS * SSD_HEAD_DIM
SSD_GROUPS = 2
SSD_STATE = 128
SSD_CONV_WIDTH = 4
SSD_CONV_CH = SSD_INNER + 2 * SSD_GROUPS * SSD_STATE
CHUNK = 64
CHUNK_SHIFT = 6
ATT_HEADS = 4
ATT_QK_DIM = 64
ATT_V_DIM = 2 * ATT_QK_DIM
ATT_WIDTH = ATT_HEADS * ATT_V_DIM
ATT_SCALE = ATT_QK_DIM ** -0.5
ALIBI_SLOPES = tuple(2.0 ** (-8.0 * (h + 1) / ATT_HEADS) for h in range(ATT_HEADS))
N_BRANCH = 3
N_SUB = 3
EPS = 1e-6
SUBLN_EPS = 1e-5
CONV_LN_EPS = 1e-5
NEG_BIG = -1e30
LOG2E = math.log2(math.e)
BIAS_TERMS = 3
ATT_ONES_ROWS = 16
ATT_SKIP_LOG2 = 152.0
NORM_SLACK = 1.0 + 2.0 ** -6
ATT_PHASES = (((0, 1, 2, 3), 1), ((1, 2, 3), 1), ((2, 3), 2), ((3,), 4))
assert all(len(heads) * width <= ATT_HEADS for heads, width in ATT_PHASES)

VMEM_LIMIT_BYTES = 58 * 1024 * 1024
LANES = 128
SUBLANES = 8

FFN_ROWS = 512
FFN_EDGES = (0, 768, 1792, D_FF)
MIX_COLS = 512
CONV_ROWS = 32
MIX_ROWS = 256
ATT_BLOCK = 256
ATT_QBLOCK = 512
ADA_COLS = 1152
CONV_PAD = 32
SCONV_PAD = 8


def _lambda_init(layer):
    return 0.8 - 0.6 * math.exp(-0.3 * layer)


def _params(sem):
    return pltpu.CompilerParams(dimension_semantics=sem, vmem_limit_bytes=VMEM_LIMIT_BYTES)


def _resident(shape):
    nd = len(shape)
    return pl.BlockSpec(shape, lambda *_: (0,) * nd, pipeline_mode=pl.Buffered(1))


def _dot(a, b):
    return jnp.dot(a, b, preferred_element_type=F32)


def _dot_nt(a, b):
    return lax.dot_general(a, b, (((1,), (1,)), ((), ())), preferred_element_type=F32)


def _dot_tn(a, b):
    return lax.dot_general(a, b, (((0,), (0,)), ((), ())), preferred_element_type=F32)


def _rms(x, g, eps=EPS):
    return x * lax.rsqrt(jnp.mean(x * x, axis=-1, keepdims=True) + eps) * g


def _sigmoid(x):
    return 0.5 * jnp.tanh(0.5 * x) + 0.5


def _silu(x):
    return x * _sigmoid(x)


def _split3(a):
    hi = a.astype(BF16)
    r = a - hi.astype(F32)
    mid = r.astype(BF16)
    lo = (r - mid.astype(F32)).astype(BF16)
    return hi, mid, lo


def _ada_kernel(c_ref, w_ref, b_ref, o_ref):
    sc = _silu(c_ref[...]).astype(BF16)
    o_ref[...] = _dot(sc, w_ref[...].astype(BF16)) + b_ref[...]


def _ada_call(c_all, w_ada, b_ada):
    depth, d, n = w_ada.shape
    rows = c_all.shape[0]
    return pl.pallas_call(
        _ada_kernel,
        grid=(depth, n // ADA_COLS),
        in_specs=[pl.BlockSpec((rows, d), lambda l, j: (0, 0)),
                  pl.BlockSpec((None, d, ADA_COLS), lambda l, j: (l, 0, j)),
                  pl.BlockSpec((None, 1, ADA_COLS), lambda l, j: (l, 0, j))],
        out_specs=pl.BlockSpec((None, rows, ADA_COLS), lambda l, j: (l, 0, j)),
        out_shape=jax.ShapeDtypeStruct((depth, rows, n), F32),
        compiler_params=_params(("arbitrary", "arbitrary")),
        name="adaln",
    )(c_all, w_ada, b_ada.reshape(depth, 1, n))


def _ffn_kernel(*refs, with_mix, lam0):
    if with_mix:
        (x_ref, p_ref, g2_ref, o_ref, subg_ref, wbr_ref, wmix_ref, gate1_ref, gpost1_ref, *rest) = refs
    else:
        x_ref, *rest = refs
    shift_ref, scale_ref, gate_ref, gpre_ref, gpost_ref, win_ref, wout_ref, out_ref = rest
    x = x_ref[...]
    if with_mix:
        o = o_ref[...]
        subg = subg_ref[...]
        heads = []
        for h in range(ATT_HEADS):
            oh = o[:, h * ATT_V_DIM:(h + 1) * ATT_V_DIM]
            heads.append(_rms(oh, subg, SUBLN_EPS) * (1.0 - lam0))
        c_out = _dot(jnp.concatenate(heads, axis=1).astype(BF16), wbr_ref[...])
        merged = (p_ref[...] + g2_ref[...] * c_out).astype(BF16)
        x = x + gate1_ref[...] * _rms(_dot(merged, wmix_ref[...]), gpost1_ref[...])
    h = (_rms(x, gpre_ref[...]) * (1.0 + scale_ref[...]) + shift_ref[...]).astype(BF16)
    edges = FFN_EDGES

    def project(c):
        return (_dot(h, win_ref[:, edges[c]:edges[c + 1]]), _dot(h, win_ref[:, D_FF + edges[c]:D_FF + edges[c + 1]]))

    f, cur = None, project(0)
    for c in range(len(edges) - 1):
        nxt = project(c + 1) if c + 2 < len(edges) else None
        up, gt = cur
        part = _dot((_silu(gt) * up).astype(BF16), wout_ref[edges[c]:edges[c + 1], :])
        f = part if f is None else f + part
        cur = nxt
    out_ref[...] = x + 0.5 * gate_ref[...] * _rms(f, gpost_ref[...])


def _ffn_call(x, mod, gpre, gpost, w_in, w_out, mix=None, lam0=0.0):
    nseq, L, D = x.shape
    bm = min(FFN_ROWS, L)
    row_spec = pl.BlockSpec((None, bm, D), lambda s, i: (s, i, 0))

    def mod_spec(m):
        if m.shape[1] == 1:
            return pl.BlockSpec((None, 1, D), lambda s, i: (s, 0, 0))
        return row_spec

    args, specs = [x], [row_spec]
    if mix is not None:
        args += [mix["p"], mix["g2"], mix["o"], mix["subg"], mix["w_br_attn"], mix["w_mix_out"], mix["gate"], mix["gpost"]]
        specs += [row_spec, row_spec, pl.BlockSpec((None, bm, ATT_WIDTH), lambda s, i: (s, i, 0)),
                  _resident((1, ATT_V_DIM)), _resident((ATT_WIDTH, D)), _resident((D, D)),
                  mod_spec(mix["gate"]), _resident((1, D))]
    args += [mod[0], mod[1], mod[2], gpre, gpost, w_in, w_out]
    specs += [mod_spec(mod[0]), mod_spec(mod[1]), mod_spec(mod[2]), _resident((1, D)), _resident((1, D)),
              _resident(w_in.shape), _resident(w_out.shape)]
    return pl.pallas_call(
        functools.partial(_ffn_kernel, with_mix=mix is not None, lam0=lam0),
        grid=(nseq, L // bm),
        in_specs=specs,
        out_specs=row_spec,
        out_shape=jax.ShapeDtypeStruct(x.shape, F32),
        compiler_params=_params(("arbitrary", "arbitrary")),
        name="merge_ffn" if mix is not None else "ffn",
    )(*args)


def _interleave(ahead, jobs):
    done = 0
    for i, job in enumerate(jobs):
        upto = -(-len(ahead) * (i + 1) // len(jobs))
        for a in ahead[done:upto]:
            a()
        done = upto
        job()


def _dwconv_jobs(src_ref, first, w, b, rows, rc, shift_ref=None):
    width, ch = w.shape
    if shift_ref is not None:
        span = shift_ref.shape[1]
        for s in sorted({(first + j) % SUBLANES for j in range(width)} - {0}):
            shift_ref[s - 1] = src_ref[s:s + span, :]

    def tap(j, r0):
        a, s = divmod(first + j, SUBLANES)
        if shift_ref is None or s == 0:
            return src_ref[first + j + r0:first + j + r0 + rc, :]
        return shift_ref[s - 1, a * SUBLANES + r0:a * SUBLANES + r0 + rc, :]

    outs = []

    def job(r0):
        acc = jnp.broadcast_to(b, (rc, ch))
        for j in range(width):
            acc = acc + w[j:j + 1, :] * tap(j, r0)
        outs.append(acc)

    return [functools.partial(job, r0) for r0 in range(0, rows, rc)], outs


def _mix_kernel(*refs, bm, q, n_prev):
    (x_ref, shift_ref, scale_ref, gpre_ref,
     wglu_ref, wz_ref, wxbc_ref, wdtr_ref, wdtt_ref, wqkv_ref, wgate_ref, bgate_ref,
     cw_ref, cb_ref, lng_ref, lnb_ref, wbrc_ref,
     sw_ref, sb_ref, dtbr_ref, dtbc_ref, alr_ref, alc_ref, dr_ref, ng_ref, wbrs_ref,
     conv0_ref, sconv0_ref, ssd0_ref,
     p_ref, g2_ref, k_ref, v_ref, qb_ref, kb_ref, vb_ref, convn_ref, sconvn_ref, ssdn_ref,
     conv_s, sconv_s, state_s, shift_s) = refs[n_prev:]

    @pl.when(pl.program_id(1) == 0)
    def _():
        conv_s[0:CONV_PAD, :] = conv0_ref[...]
        sconv_s[0:SCONV_PAD, :] = sconv0_ref[...]
        state_s[...] = ssd0_ref[...].reshape(SSD_INNER, SSD_STATE).T

    h = (_rms(x_ref[...], gpre_ref[...]) * (1.0 + scale_ref[...]) + shift_ref[...]).astype(BF16)

    glu = _dot(h, wglu_ref[...])
    conv_s[CONV_PAD:CONV_PAD + bm, :] = glu[:, :CONV_CH] * _sigmoid(glu[:, CONV_CH:])
    sconv_s[SCONV_PAD:SCONV_PAD + bm, :] = _dot(h, wxbc_ref[...])
    dt_raw_t = _dot_nt(wdtt_ref[...], h)
    slices = {}

    def project(name, w_ref):
        def job(c0):
            slices[name].append(_dot(h, w_ref[:, c0:c0 + MIX_COLS]))
        slices[name] = []
        return [functools.partial(job, c0) for c0 in range(0, w_ref.shape[1], MIX_COLS)]

    projections = (project("dt", wdtr_ref) + project("z", wz_ref) + project("qkv", wqkv_ref)
                   + project("gate", wgate_ref))
    rc = min(CONV_ROWS, bm)
    conv_jobs, conv_rows = _dwconv_jobs(conv_s, CONV_PAD - (CONV_WIDTH - 1), cw_ref[...], cb_ref[...], bm, rc, shift_s)
    sconv_jobs, sconv_rows = _dwconv_jobs(sconv_s, SCONV_PAD - (SSD_CONV_WIDTH - 1), sw_ref[...], sb_ref[...], bm, rc)
    _interleave(projections, conv_jobs + sconv_jobs)
    dt_raw, z, qkv, gate_logits = (jnp.concatenate(slices[n], axis=1) for n in ("dt", "z", "qkv", "gate"))

    k = qkv[:, ATT_WIDTH:2 * ATT_WIDTH]
    v = qkv[:, 2 * ATT_WIDTH:]
    for hd in range(ATT_HEADS):
        k_ref[:, hd, :] = k[:, hd * ATT_V_DIM:(hd + 1) * ATT_V_DIM]
        v_ref[:, hd, :] = v[:, hd * ATT_V_DIM:(hd + 1) * ATT_V_DIM]
    qb_ref[...] = (qkv[:, :ATT_WIDTH] * (ATT_SCALE * LOG2E)).astype(BF16)
    kb_ref[...] = k.astype(BF16)
    vb_ref[...] = v.astype(BF16)

    a = jnp.concatenate(conv_rows, axis=0)
    mu = jnp.mean(a, axis=-1, keepdims=True)
    ac = a - mu
    a = ac * lax.rsqrt(jnp.mean(ac * ac, axis=-1, keepdims=True) + CONV_LN_EPS) * lng_ref[...] + lnb_ref[...]
    a_out = _dot(_silu(a).astype(BF16), wbrc_ref[...])
    convn_ref[...] = conv_s[CONV_PAD + bm - (CONV_WIDTH - 1):CONV_PAD + bm, :]
    conv_s[0:CONV_PAD, :] = conv_s[bm:bm + CONV_PAD, :]

    xbc = _silu(jnp.concatenate(sconv_rows, axis=0))
    sconvn_ref[...] = sconv_s[SCONV_PAD + bm - (SSD_CONV_WIDTH - 1):SCONV_PAD + bm, :]
    sconv_s[0:SCONV_PAD, :] = sconv_s[bm:bm + SCONV_PAD, :]

    dt_e = jax.nn.softplus(dt_raw + dtbr_ref[...])
    dt_t = jax.nn.softplus(dt_raw_t + dtbc_ref[...])
    a_rep = -jnp.exp(alr_ref[...])
    a_col = -jnp.exp(alc_ref[...])
    row = lax.broadcasted_iota(jnp.int32, (q, q), 0)
    col = lax.broadcasted_iota(jnp.int32, (q, q), 1)
    causal = row >= col
    tril = causal.astype(BF16)
    triu = (row <= col).astype(BF16)
    gw = SSD_INNER // SSD_GROUPS
    ys = []
    for c in range(bm // q):
        rows = slice(c * q, (c + 1) * q)
        dt_c = dt_e[rows]
        hi, mid, lo = _split3(dt_c * a_rep)
        acum = _dot(tril, hi) + _dot(tril, mid) + _dot(tril, lo)
        hi, mid, lo = _split3(dt_t[:, rows] * a_col)
        acum_t = _dot(hi, triu) + _dot(mid, triu) + _dot(lo, triu)
        xs_c = xbc[rows, :SSD_INNER]
        xdt = xs_c * dt_c
        a_last = acum[q - 1:q, :]
        x_end = (xdt * jnp.exp(a_last - acum)).astype(BF16)
        e_in = jnp.exp(acum)
        xdt_b = xdt.astype(BF16)
        state = state_s[...]
        state_b = state.astype(BF16)
        y_parts = []
        for g in range(SSD_GROUPS):
            bg = xbc[rows, SSD_INNER + g * SSD_STATE:SSD_INNER + (g + 1) * SSD_STATE]
            cg = xbc[rows, SSD_INNER + (SSD_GROUPS + g) * SSD_STATE:SSD_INNER + (SSD_GROUPS + g + 1) * SSD_STATE]
            bg_b, cg_b = bg.astype(BF16), cg.astype(BF16)
            cb = _dot_nt(cg_b, bg_b)
            y_off = _dot(cg_b, state_b[:, g * gw:(g + 1) * gw]) * e_in[:, g * gw:(g + 1) * gw]
            y_diag = []
            for r in range(SSD_HEADS // SSD_GROUPS):
                hd = g * (SSD_HEADS // SSD_GROUPS) + r
                lo_l = hd * SSD_HEAD_DIM
                seg = acum[:, lo_l:lo_l + q] - acum_t[hd:hd + 1, :]
                m = (cb * jnp.exp(jnp.where(causal, seg, NEG_BIG))).astype(BF16)
                y_diag.append(_dot(m, xdt_b[:, lo_l:lo_l + SSD_HEAD_DIM]))
            y_parts.append(jnp.concatenate(y_diag, axis=1) + y_off)
            state_s[:, g * gw:(g + 1) * gw] = (state[:, g * gw:(g + 1) * gw] * jnp.exp(a_last[:, g * gw:(g + 1) * gw])
                                               + _dot_tn(bg_b, x_end[:, g * gw:(g + 1) * gw]))
        ys.append(jnp.concatenate(y_parts, axis=1) + dr_ref[...] * xs_c)

    @pl.when(pl.program_id(1) == pl.num_programs(1) - 1)
    def _():
        ssdn_ref[...] = state_s[...].T.reshape(SSD_HEADS, SSD_HEAD_DIM, SSD_STATE)

    y = jnp.concatenate(ys, axis=0) if len(ys) > 1 else ys[0]
    yz = y * _silu(z)
    yz = jnp.concatenate([_rms(yz[:, g * gw:(g + 1) * gw], 1.0) for g in range(SSD_GROUPS)], axis=1) * ng_ref[...]
    b_out = _dot(yz.astype(BF16), wbrs_ref[...])

    gates = _sigmoid(gate_logits + bgate_ref[...])
    p_ref[...] = (gates[:, :D_MODEL] * a_out + gates[:, D_MODEL:2 * D_MODEL] * b_out).astype(BF16)
    g2_ref[...] = gates[:, 2 * D_MODEL:].astype(BF16)


def _mix_call(x, shift, scale, gpre, w, conv0, sconv0, ssd0, layer, depth, kv_prev):
    nseq, L, D = x.shape
    bm = min(MIX_ROWS, L)
    q = min(CHUNK, bm)
    kv_shape = jax.ShapeDtypeStruct((depth, nseq, L, ATT_HEADS, ATT_V_DIM), F32)
    kv_spec = pl.BlockSpec((None, None, bm, ATT_HEADS, ATT_V_DIM), lambda s, i: (layer, s, i, 0, 0))
    prev = list(kv_prev or ())
    rows = lambda n: pl.BlockSpec((None, bm, n), lambda s, i: (s, i, 0))
    per_seq = lambda a: pl.BlockSpec((None,) + a.shape[1:], lambda s, i: (s,) + (0,) * (len(a.shape) - 1))
    weights = [gpre, w["w_glu"], w["w_z"], w["w_xbc"], w["w_dt_rep"], w["w_dt_t"], w["w_qkv"], w["w_gate"], w["b_gate"],
               w["conv_w"], w["conv_b"], w["ln_g"], w["ln_b"], w["w_br_conv"],
               w["sconv_w"], w["sconv_b"], w["dtb_rep"], w["dtb_col"], w["alog_rep"], w["alog_col"], w["d_rep"],
               w["norm_g"], w["w_br_ssd"]]
    out_shape = [jax.ShapeDtypeStruct((nseq, L, D), BF16), jax.ShapeDtypeStruct((nseq, L, D), BF16), kv_shape, kv_shape,
                 jax.ShapeDtypeStruct((nseq, L, ATT_WIDTH), BF16), jax.ShapeDtypeStruct((nseq, L, ATT_WIDTH), BF16),
                 jax.ShapeDtypeStruct((nseq, L, ATT_WIDTH), BF16),
                 jax.ShapeDtypeStruct((nseq, CONV_WIDTH - 1, CONV_CH), F32),
                 jax.ShapeDtypeStruct((nseq, SSD_CONV_WIDTH - 1, SSD_CONV_CH), F32),
                 jax.ShapeDtypeStruct((nseq, SSD_HEADS, SSD_HEAD_DIM, SSD_STATE), F32)]
    out_specs = [rows(D), rows(D), kv_spec, kv_spec, rows(ATT_WIDTH), rows(ATT_WIDTH), rows(ATT_WIDTH),
                 per_seq(out_shape[7]), per_seq(out_shape[8]), per_seq(out_shape[9])]
    return pl.pallas_call(
        functools.partial(_mix_kernel, bm=bm, q=q, n_prev=len(prev)),
        grid=(nseq, L // bm),
        in_specs=[pl.BlockSpec(memory_space=pl.ANY)] * len(prev)
                 + [rows(D), per_seq(shift), per_seq(scale)] + [_resident(a.shape) for a in weights]
                 + [per_seq(conv0), per_seq(sconv0), per_seq(ssd0)],
        input_output_aliases={i: 2 + i for i in range(len(prev))},
        out_specs=out_specs,
        out_shape=out_shape,
        scratch_shapes=[pltpu.VMEM((CONV_PAD + bm, CONV_CH), F32), pltpu.VMEM((SCONV_PAD + bm, SSD_CONV_CH), F32),
                        pltpu.VMEM((SSD_STATE, SSD_INNER), F32),
                        pltpu.VMEM((SUBLANES - 1, CONV_PAD + bm - SUBLANES, CONV_CH), F32)],
        compiler_params=_params(("arbitrary", "arbitrary")),
        name="mix_front",
    )(*prev, x, shift, scale, *weights, conv0, sconv0, ssd0)


def _lambda(lq_ref, lk_ref, lam0):
    lq, lk = lq_ref[...], lk_ref[...]
    e0 = jnp.exp(jnp.sum(lq[0:1] * lk[0:1], axis=-1, keepdims=True))
    e1 = jnp.exp(jnp.sum(lq[1:2] * lk[1:2], axis=-1, keepdims=True))
    return e0 - e1 + lam0


def _attn_prompt_kernel(q_ref, k_ref, vt_ref, ext_ref, corr_ref, lq_ref, lk_ref, o_ref,
                        qa_s, acc_s, m_s, kn_s, sa_s, sb_s, *, lam0):
    blk, qw = ATT_BLOCK, ATT_QBLOCK
    per_q = qw // blk
    qi = pl.program_id(1)
    first = lax.broadcasted_iota(jnp.int32, (blk, LANES), 1) < ATT_QK_DIM
    lane_q = lax.broadcasted_iota(jnp.int32, (qw, LANES), 1)
    first_q = lane_q < ATT_QK_DIM
    ones_from = lambda c: jnp.where((lane_q >= c) & (lane_q < c + BIAS_TERMS), 1.0, 0.0).astype(BF16)
    head_cols = lambda h: slice(h * ATT_V_DIM, (h + 1) * ATT_V_DIM)
    qk_shift = ATT_QK_DIM.bit_length() - 1
    sel = (jnp.right_shift(lax.broadcasted_iota(jnp.int32, (ATT_WIDTH, LANES), 0), qk_shift)
           == lax.broadcasted_iota(jnp.int32, (ATT_WIDTH, LANES), 1)).astype(BF16)

    def max_sq_norm(x):
        xf = x.astype(F32)
        return jnp.max(_dot((xf * xf).astype(BF16), sel), axis=0, keepdims=True)

    @pl.when(qi == 0)
    def _():
        def body(t, mx):
            return jnp.maximum(mx, max_sq_norm(k_ref[pl.ds(pl.multiple_of(t * blk, blk), blk), :]))
        kn_s[0:1, :] = lax.fori_loop(0, k_ref.shape[0] // blk, body, jnp.zeros((1, LANES), F32))

    q = q_ref[...]
    for h in range(ATT_HEADS):
        qa_s[2 * h] = jnp.where(first_q, q[:, head_cols(h)], ones_from(ATT_QK_DIM))
        qa_s[2 * h + 1] = jnp.where(first_q, ones_from(0), q[:, head_cols(h)])
    acc_s[...] = jnp.zeros_like(acc_s)
    m_s[...] = jnp.full(m_s.shape, NEG_BIG, F32)
    ones_rows = jnp.ones((ATT_ONES_ROWS, blk), BF16)

    slot_of = lambda pos, h, t, c, nt: 2 * (h if nt == 1 else pos * nt + t) + c

    def score_jobs(kjs, heads, s_ref, q0=0):
        def job(pos, h, t, c):
            kh = k_ref[pl.ds(pl.multiple_of(kjs[t] * blk, blk), blk), head_cols(h)]
            ka = jnp.where(first, kh, ext_ref[h, 0]) if c == 0 else jnp.where(first, ext_ref[h, 1], kh)
            s_ref[slot_of(pos, h, t, c, len(kjs)), :, q0:] = _dot_nt(ka, qa_s[2 * h + c, q0:, :])
        return [functools.partial(job, pos, h, t, c)
                for pos, h in enumerate(heads) for c in range(2) for t in range(len(kjs))]

    def fold_jobs(kjs, heads, s_ref, diagonal=None):
        q0 = 0 if diagonal is None else diagonal * blk
        k0s = [pl.multiple_of(kj * blk, blk) for kj in kjs]
        nt = len(k0s)

        def job(pos, h, c):
            bases = [(ALIBI_SLOPES[h] * LOG2E) * k0.astype(F32) for k0 in k0s]
            vt = jnp.concatenate([vt_ref[head_cols(h), pl.ds(k0, blk)] for k0 in k0s], axis=1)
            vt = jnp.concatenate([vt, jnp.concatenate([ones_rows] * nt, axis=1)], axis=0)
            sc = [s_ref[slot_of(pos, h, t, c, nt), :, q0:] for t in range(nt)]
            if diagonal is not None:
                sc = [x + (-2.0 * ALIBI_SLOPES[h] * LOG2E) * corr_ref[diagonal, :, q0:] for x in sc]
            m_old = m_s[2 * h + c:2 * h + c + 1, q0:]
            m_new = m_old
            for t in range(nt):
                m_new = jnp.maximum(m_new, jnp.max(sc[t], axis=0, keepdims=True) + bases[t])
            m_s[2 * h + c:2 * h + c + 1, q0:] = m_new
            p = jnp.concatenate([jnp.exp2(sc[t] - (m_new - bases[t])).astype(BF16) for t in range(nt)], axis=0)
            acc_s[2 * h + c, :, q0:] = jnp.exp2(m_old - m_new) * acc_s[2 * h + c, :, q0:] + _dot(vt, p)
        return [functools.partial(job, pos, h, c) for pos, h in enumerate(heads) for c in range(2)]

    def run(jobs):
        for job in jobs:
            job()

    def interleave(ahead, folds):
        per = len(ahead) // len(folds)
        for f, job in enumerate(folds):
            run(ahead[f * per:(f + 1) * per])
            job()

    bufs = (sa_s, sb_s)
    all_heads = range(ATT_HEADS)
    run(score_jobs([per_q * qi], all_heads, bufs[0]))
    for d in range(per_q):
        if d + 1 < per_q:
            ahead = score_jobs([per_q * qi + d + 1], all_heads, bufs[(d + 1) % 2], q0=(d + 1) * blk)
        else:
            ahead = score_jobs([jnp.maximum(per_q * qi - 1, 0)], all_heads, bufs[(d + 1) % 2])
        interleave(ahead, fold_jobs([per_q * qi + d], all_heads, bufs[d % 2], diagonal=d))

    bound = jnp.sqrt(max_sq_norm(q)) * jnp.sqrt(kn_s[0:1, :]) * NORM_SLACK
    top = per_q * qi
    lo, limit = [], top
    for h in range(ATT_HEADS):
        gap = None
        for c in range(2):
            g = bound[:, 2 * h + c:2 * h + c + 1] - jnp.min(m_s[2 * h + c:2 * h + c + 1, :], axis=1, keepdims=True)
            gap = g if gap is None else jnp.maximum(gap, g)
        x = (-ATT_SKIP_LOG2 - gap) / (ALIBI_SLOPES[h] * LOG2E)
        need = jnp.ceil((x - (blk - 1)) / blk)
        need = jnp.clip(jnp.where(need == need, need, 0.0), 0.0, top.astype(F32))
        limit = jnp.minimum(limit, need.astype(jnp.int32)[0, 0])
        lo.append(limit)

    hi = top
    steps = per_q
    widths = [w for _, w in ATT_PHASES]
    for i, (heads, width) in enumerate(ATT_PHASES):
        align = max(width, widths[min(i + 1, len(widths) - 1)]).bit_length() - 1
        lo_i = jnp.left_shift(jnp.right_shift(lo[heads[0]], align), align)

        n_i = jnp.right_shift(hi - lo_i, width.bit_length() - 1)
        tiles = lambda j, hi=hi, width=width: [jnp.maximum(hi - (j + 1) * width + t, 0) for t in range(width)]

        if width == 1:
            first_buf = steps
            steps = steps + n_i
        else:
            first_buf = 0

            @pl.when(n_i > 0)
            def _(tiles=tiles, heads=heads):
                run(score_jobs(tiles(0), heads, bufs[0]))

        def body(j, carry, tiles=tiles, heads=heads, first_buf=first_buf):
            def trip(cur, nxt):
                interleave(score_jobs(tiles(j + 1), heads, nxt), fold_jobs(tiles(j), heads, cur))
            lax.cond(jnp.bitwise_and(j + first_buf, 1) == 0, lambda: trip(bufs[0], bufs[1]), lambda: trip(bufs[1], bufs[0]))
            return carry

        lax.fori_loop(0, n_i, body, 0)
        hi = lo_i

    lam = _lambda(lq_ref, lk_ref, lam0)
    for h in range(ATT_HEADS):
        a1, a2 = acc_s[2 * h], acc_s[2 * h + 1]
        o_t = (a1[:ATT_V_DIM] / a1[ATT_V_DIM:ATT_V_DIM + 1]
               - lam * (a2[:ATT_V_DIM] / a2[ATT_V_DIM:ATT_V_DIM + 1]))
        o_ref[:, head_cols(h)] = o_t.T


def _attn_tables():
    blk, qw = ATT_BLOCK, ATT_QBLOCK
    pos = np.arange(blk, dtype=np.float64)
    ext = np.zeros((ATT_HEADS, 2, blk, LANES), np.float32)
    to_bf16 = lambda x: x.astype(ml_dtypes.bfloat16).astype(np.float64)
    for h, slope in enumerate(ALIBI_SLOPES):
        rest = slope * LOG2E * pos
        for t in range(BIAS_TERMS):
            term = to_bf16(rest)
            ext[h, 0, :, ATT_QK_DIM + t] = term
            ext[h, 1, :, t] = term
            rest = rest - term
    ahead = np.zeros((qw // blk, blk, qw), np.float32)
    qpos = np.arange(qw, dtype=np.float64)[None, :]
    for d in range(qw // blk):
        kpos = (d * blk + pos)[:, None]
        ahead[d] = np.where((kpos // CHUNK) <= (qpos // CHUNK), np.maximum(kpos - qpos, 0.0), -NEG_BIG)
    return jnp.asarray(ext, BF16), jnp.asarray(ahead, F32)


def _attn_prompt_call(qb, kb, vb, lq, lk, lam0):
    b, S, W = qb.shape
    blk, qw = ATT_BLOCK, ATT_QBLOCK
    vt = jnp.swapaxes(vb, 1, 2)
    ext, ahead = _attn_tables()
    whole = lambda shape: pl.BlockSpec((None,) + shape, lambda s, i: (s, 0, 0), pipeline_mode=pl.Buffered(1))
    return pl.pallas_call(
        functools.partial(_attn_prompt_kernel, lam0=lam0),
        grid=(b, S // qw),
        in_specs=[pl.BlockSpec((None, qw, W), lambda s, i: (s, i, 0)), whole((S, W)), whole((W, S)),
                  _resident(ext.shape), _resident(ahead.shape), _resident(lq.shape), _resident(lk.shape)],
        out_specs=pl.BlockSpec((None, qw, W), lambda s, i: (s, i, 0)),
        out_shape=jax.ShapeDtypeStruct((b, S, W), F32),
        scratch_shapes=[pltpu.VMEM((2 * ATT_HEADS, qw, LANES), BF16),
                        pltpu.VMEM((2 * ATT_HEADS, ATT_V_DIM + ATT_ONES_ROWS, qw), F32),
                        pltpu.VMEM((2 * ATT_HEADS, qw), F32), pltpu.VMEM((SUBLANES, LANES), F32),
                        pltpu.VMEM((2 * ATT_HEADS, blk, qw), F32), pltpu.VMEM((2 * ATT_HEADS, blk, qw), F32)],
        compiler_params=_params(("arbitrary", "arbitrary")),
        name="attn_prompt",
    )(qb, kb, vt, ext, ahead, lq, lk)


def _attn_sample_kernel(q_ref, kn_ref, vn_ref, kc_ref, vc_ref, lq_ref, lk_ref, o_ref, *, lam0):
    ls, past = q_ref.shape[0], kc_ref.shape[0]
    lam = _lambda(lq_ref, lk_ref, lam0)
    lane = lax.broadcasted_iota(jnp.int32, (ls, LANES), 1)
    first = lane < ATT_QK_DIM

    def bias_mask(nk, k_first):
        qpos = past + lax.broadcasted_iota(jnp.int32, (ls, nk), 0)
        kpos = k_first + lax.broadcasted_iota(jnp.int32, (ls, nk), 1)
        dist = jnp.abs(qpos - kpos).astype(F32)
        visible = jnp.right_shift(kpos, CHUNK_SHIFT) <= jnp.right_shift(qpos, CHUNK_SHIFT)
        return dist, visible

    dist_c, vis_c = bias_mask(past, 0)
    dist_n, vis_n = bias_mask(ls, past)
    outs = []
    for h in range(ATT_HEADS):
        cols = slice(h * ATT_V_DIM, (h + 1) * ATT_V_DIM)
        qh = q_ref[:, cols]
        kc, kn = kc_ref[:, h, :].astype(BF16), kn_ref[:, cols]
        vc, vn = vc_ref[:, h, :].astype(BF16), vn_ref[:, cols]
        o = []
        for c in range(2):
            qm = jnp.where(first == (c == 0), qh, jnp.zeros_like(qh))
            sc = jnp.where(vis_c, _dot_nt(qm, kc) - (ALIBI_SLOPES[h] * LOG2E) * dist_c, NEG_BIG)
            sn = jnp.where(vis_n, _dot_nt(qm, kn) - (ALIBI_SLOPES[h] * LOG2E) * dist_n, NEG_BIG)
            m = jnp.maximum(jnp.max(sc, axis=-1, keepdims=True), jnp.max(sn, axis=-1, keepdims=True))
            pc, pn = jnp.exp2(sc - m), jnp.exp2(sn - m)
            l = jnp.sum(pc, axis=-1, keepdims=True) + jnp.sum(pn, axis=-1, keepdims=True)
            o.append((_dot(pc.astype(BF16), vc) + _dot(pn.astype(BF16), vn)) / l)
        outs.append(o[0] - lam * o[1])
    o_ref[...] = jnp.concatenate(outs, axis=1)


def _attn_sample_call(qb, kb, vb, cache_k, cache_v, layer, lq, lk, lam0):
    b, ls, W = qb.shape
    past = cache_k.shape[2]
    new = pl.BlockSpec((None, ls, W), lambda s: (s, 0, 0))
    old = pl.BlockSpec((None, None, past, ATT_HEADS, ATT_V_DIM), lambda s: (layer, s, 0, 0, 0))
    return pl.pallas_call(
        functools.partial(_attn_sample_kernel, lam0=lam0),
        grid=(b,),
        in_specs=[new, new, new, old, old, _resident(lq.shape), _resident(lk.shape)],
        out_specs=new,
        out_shape=jax.ShapeDtypeStruct((b, ls, W), F32),
        compiler_params=_params(("arbitrary",)),
        name="attn_sample",
    )(qb, kb, vb, cache_k, cache_v, lq, lk)


def _prep_layer(l, w_in, b_gate, conv_dw_w, conv_dw_b, conv_ln_g, conv_ln_b, w_br_conv, ssd_conv_w, ssd_conv_b,
                ssd_dt_bias, ssd_A_log, ssd_D, ssd_norm_g, w_br_ssd):
    splits = (2 * CONV_CH, SSD_INNER, SSD_CONV_CH, SSD_HEADS, 3 * ATT_WIDTH, N_BRANCH * D_MODEL)
    pts = np.cumsum(splits)[:-1].tolist()
    w_glu, w_z, w_xbc, w_dt, w_qkv, w_gate = (p.astype(BF16) for p in jnp.split(w_in[l], pts, axis=1))
    rep = lambda a: jnp.repeat(a, SSD_HEAD_DIM, axis=-1)
    return dict(
        w_glu=w_glu, w_z=w_z, w_xbc=w_xbc, w_dt_rep=rep(w_dt), w_dt_t=w_dt.T, w_qkv=w_qkv, w_gate=w_gate,
        b_gate=b_gate[l].reshape(1, N_BRANCH * D_MODEL),
        conv_w=conv_dw_w[l], conv_b=conv_dw_b[l][None], ln_g=conv_ln_g[l][None], ln_b=conv_ln_b[l][None],
        w_br_conv=w_br_conv[l].astype(BF16),
        sconv_w=ssd_conv_w[l], sconv_b=ssd_conv_b[l][None],
        dtb_rep=rep(ssd_dt_bias[l])[None], dtb_col=ssd_dt_bias[l][:, None],
        alog_rep=rep(ssd_A_log[l])[None], alog_col=ssd_A_log[l][:, None],
        d_rep=rep(ssd_D[l])[None], norm_g=ssd_norm_g[l][None], w_br_ssd=w_br_ssd[l].astype(BF16))


def _state_in(conv, sconv, ssd):
    conv = jnp.pad(conv, ((0, 0), (CONV_PAD - (CONV_WIDTH - 1), 0), (0, 0)))
    sconv = jnp.pad(sconv, ((0, 0), (SCONV_PAD - (SSD_CONV_WIDTH - 1), 0), (0, 0)))
    return conv, sconv, ssd


def kernel(x_prompt, x_sample, cache_attn_k, cache_attn_v, state_conv, state_ssd_conv, state_ssd, c_prompt, c_sample,
           w_ada, b_ada, norm_pre, norm_post, w_ffn_in, w_ffn_out, w_in, b_gate, conv_dw_w, conv_dw_b, conv_ln_g,
           conv_ln_b, w_br_conv, ssd_conv_w, ssd_conv_b, ssd_dt_bias, ssd_A_log, ssd_D, ssd_norm_g, w_br_ssd,
           lambda_q, lambda_k, attn_subln_g, w_br_attn, w_mix_out):
    depth = w_ada.shape[0]
    bp, S, D = x_prompt.shape
    bs, ls, _ = x_sample.shape

    mods = _ada_call(jnp.concatenate([c_prompt, c_sample], axis=0), w_ada, b_ada)
    mods = mods.reshape(depth, bp + bs, N_SUB, 3, D)

    zeros = _state_in(jnp.zeros((bp, CONV_WIDTH - 1, CONV_CH), F32), jnp.zeros((bp, SSD_CONV_WIDTH - 1, SSD_CONV_CH), F32),
                      jnp.zeros((bp, SSD_HEADS, SSD_HEAD_DIM, SSD_STATE), F32))
    xp = x_prompt
    xs = x_sample.reshape(1, bs * ls, D)
    st_p, st_s = [], []
    kv_p = kv_s = None
    for l in range(depth):
        lam0 = _lambda_init(l)
        w = _prep_layer(l, w_in, b_gate, conv_dw_w, conv_dw_b, conv_ln_g, conv_ln_b, w_br_conv, ssd_conv_w, ssd_conv_b,
                        ssd_dt_bias, ssd_A_log, ssd_D, ssd_norm_g, w_br_ssd)
        wf_in, wf_out = w_ffn_in[l].astype(BF16), w_ffn_out[l].astype(BF16)
        w_bra, w_mix = w_br_attn[l].astype(BF16), w_mix_out[l].astype(BF16)
        gpre = lambda s: norm_pre[l, s][None]
        gpost = lambda s: norm_post[l, s][None]
        subg = attn_subln_g[l][None]
        mp = lambda s, k: mods[l, :bp, s, k][:, None, :]
        ms_seq = lambda s, k: mods[l, bp:, s, k][:, None, :]
        ms_tok = lambda s, k: jnp.repeat(mods[l, bp:, s, k], ls, axis=0)[None]

        xp = _ffn_call(xp, (mp(0, 0), mp(0, 1), mp(0, 2)), gpre(0), gpost(0), wf_in[0], wf_out[0])
        p, g2, *kv_p, qb, kb, vb, convn, sconvn, ssdn = _mix_call(xp, mp(1, 0), mp(1, 1), gpre(1), w, *zeros,
                                                                  l, depth, kv_p)
        o = _attn_prompt_call(qb, kb, vb, lambda_q[l], lambda_k[l], lam0)
        mix = dict(p=p, g2=g2, o=o, subg=subg, w_br_attn=w_bra, w_mix_out=w_mix, gate=mp(1, 2), gpost=gpost(1))
        xp = _ffn_call(xp, (mp(2, 0), mp(2, 1), mp(2, 2)), gpre(2), gpost(2), wf_in[1], wf_out[1], mix=mix, lam0=lam0)
        st_p.append((convn, sconvn, ssdn))

        xs = _ffn_call(xs, (ms_tok(0, 0), ms_tok(0, 1), ms_tok(0, 2)), gpre(0), gpost(0), wf_in[0], wf_out[0])
        states = _state_in(state_conv[l], state_ssd_conv[l], state_ssd[l])
        p, g2, *kv_s, qb, kb, vb, convn, sconvn, ssdn = _mix_call(xs.reshape(bs, ls, D), ms_seq(1, 0), ms_seq(1, 1),
                                                                  gpre(1), w, *states, l, depth, kv_s)
        o = _attn_sample_call(qb, kb, vb, cache_attn_k, cache_attn_v, l, lambda_q[l], lambda_k[l], lam0)
        flat = lambda t: t.reshape(1, bs * ls, t.shape[-1])
        mix = dict(p=flat(p), g2=flat(g2), o=flat(o), subg=subg, w_br_attn=w_bra, w_mix_out=w_mix, gate=ms_tok(1, 2),
                   gpost=gpost(1))
        xs = _ffn_call(xs, (ms_tok(2, 0), ms_tok(2, 1), ms_tok(2, 2)), gpre(2), gpost(2), wf_in[1], wf_out[1], mix=mix,
                       lam0=lam0)
        st_s.append((convn, sconvn, ssdn))

    stack = lambda st, i: jnp.stack([s[i] for s in st])
    return (xp, xs.reshape(bs, ls, D),
            kv_p[0], kv_p[1], stack(st_p, 0), stack(st_p, 1), stack(st_p, 2),
            kv_s[0], kv_s[1], stack(st_s, 0), stack(st_s, 1), stack(st_s, 2))
```

```python
import functools
import math

import ml_dtypes
import numpy as np
import jax
import jax.numpy as jnp
from jax import lax
from jax.experimental import pallas as pl
from jax.experimental.pallas import tpu as pltpu

F32, BF16 = jnp.float32, jnp.bfloat16

D_MODEL = 1024
D_FF = 2816
CONV_CH = 512
CONV_WIDTH = 31
SSD_HEADS = 16
SSD_HEAD_DIM = 64
SSD_INNER = SSD_HEADS * SSD_HEAD_DIM
SSD_GROUPS = 2
SSD_STATE = 128
SSD_CONV_WIDTH = 4
SSD_CONV_CH = SSD_INNER + 2 * SSD_GROUPS * SSD_STATE
CHUNK = 64
CHUNK_SHIFT = 6
ATT_HEADS = 4
ATT_QK_DIM = 64
ATT_V_DIM = 2 * ATT_QK_DIM
ATT_WIDTH = ATT_HEADS * ATT_V_DIM
ATT_SCALE = ATT_QK_DIM ** -0.5
ALIBI_SLOPES = tuple(2.0 ** (-8.0 * (h + 1) / ATT_HEADS) for h in range(ATT_HEADS))
N_BRANCH = 3
N_SUB = 3
EPS = 1e-6
SUBLN_EPS = 1e-5
CONV_LN_EPS = 1e-5
NEG_BIG = -1e30
LOG2E = math.log2(math.e)
BIAS_TERMS = 3
ATT_ONES_ROWS = 16
ATT_SKIP_LOG2 = 152.0
NORM_SLACK = 1.0 + 2.0 ** -6
ATT_PHASES = (((0, 1, 2, 3), 1), ((1, 2, 3), 1), ((2, 3), 2), ((3,), 4))
assert all(len(heads) * width <= ATT_HEADS for heads, width in ATT_PHASES)

VMEM_LIMIT_BYTES = 58 * 1024 * 1024
LANES = 128
SUBLANES = 8

FFN_ROWS = 512
FFN_EDGES = (0, 768, 1792, D_FF)
MIX_COLS = 512
CONV_ROWS = 32
MIX_ROWS = 256
ATT_BLOCK = 256
ATT_QBLOCK = 512
ADA_COLS = 1152
CONV_PAD = 32
SCONV_PAD = 8


def _lambda_init(layer):
    return 0.8 - 0.6 * math.exp(-0.3 * layer)


def _params(sem):
    return pltpu.CompilerParams(dimension_semantics=sem, vmem_limit_bytes=VMEM_LIMIT_BYTES)


def _resident(shape):
    nd = len(shape)
    return pl.BlockSpec(shape, lambda *_: (0,) * nd, pipeline_mode=pl.Buffered(1))


def _dot(a, b):
    return jnp.dot(a, b, preferred_element_type=F32)


def _dot_nt(a, b):
    return lax.dot_general(a, b, (((1,), (1,)), ((), ())), preferred_element_type=F32)


def _dot_tn(a, b):
    return lax.dot_general(a, b, (((0,), (0,)), ((), ())), preferred_element_type=F32)


def _rms(x, g, eps=EPS):
    return x * lax.rsqrt(jnp.mean(x * x, axis=-1, keepdims=True) + eps) * g


def _sigmoid(x):
    return 0.5 * jnp.tanh(0.5 * x) + 0.5


def _silu(x):
    return x * _sigmoid(x)


def _split3(a):
    hi = a.astype(BF16)
    r = a - hi.astype(F32)
    mid = r.astype(BF16)
    lo = (r - mid.astype(F32)).astype(BF16)
    return hi, mid, lo


def _ada_kernel(c_ref, w_ref, b_ref, o_ref):
    sc = _silu(c_ref[...]).astype(BF16)
    o_ref[...] = _dot(sc, w_ref[...].astype(BF16)) + b_ref[...]


def _ada_call(c_all, w_ada, b_ada):
    depth, d, n = w_ada.shape
    rows = c_all.shape[0]
    return pl.pallas_call(
        _ada_kernel,
        grid=(depth, n // ADA_COLS),
        in_specs=[pl.BlockSpec((rows, d), lambda l, j: (0, 0)),
                  pl.BlockSpec((None, d, ADA_COLS), lambda l, j: (l, 0, j)),
                  pl.BlockSpec((None, 1, ADA_COLS), lambda l, j: (l, 0, j))],
        out_specs=pl.BlockSpec((None, rows, ADA_COLS), lambda l, j: (l, 0, j)),
        out_shape=jax.ShapeDtypeStruct((depth, rows, n), F32),
        compiler_params=_params(("arbitrary", "arbitrary")),
        name="adaln",
    )(c_all, w_ada, b_ada.reshape(depth, 1, n))


def _ffn_kernel(*refs, with_mix, lam0):
    if with_mix:
        (x_ref, p_ref, g2_ref, o_ref, subg_ref, wbr_ref, wmix_ref, gate1_ref, gpost1_ref, *rest) = refs
    else:
        x_ref, *rest = refs
    shift_ref, scale_ref, gate_ref, gpre_ref, gpost_ref, win_ref, wout_ref, out_ref = rest
    x = x_ref[...]
    if with_mix:
        o = o_ref[...]
        subg = subg_ref[...]
        heads = []
        for h in range(ATT_HEADS):
            oh = o[:, h * ATT_V_DIM:(h + 1) * ATT_V_DIM]
            heads.append(_rms(oh, subg, SUBLN_EPS) * (1.0 - lam0))
        c_out = _dot(jnp.concatenate(heads, axis=1).astype(BF16), wbr_ref[...])
        merged = (p_ref[...] + g2_ref[...] * c_out).astype(BF16)
        x = x + gate1_ref[...] * _rms(_dot(merged, wmix_ref[...]), gpost1_ref[...])
    h = (_rms(x, gpre_ref[...]) * (1.0 + scale_ref[...]) + shift_ref[...]).astype(BF16)
    edges = FFN_EDGES

    def project(c):
        return (_dot(h, win_ref[:, edges[c]:edges[c + 1]]), _dot(h, win_ref[:, D_FF + edges[c]:D_FF + edges[c + 1]]))

    f, cur = None, project(0)
    for c in range(len(edges) - 1):
        nxt = project(c + 1) if c + 2 < len(edges) else None
        up, gt = cur
        part = _dot((_silu(gt) * up).astype(BF16), wout_ref[edges[c]:edges[c + 1], :])
        f = part if f is None else f + part
        cur = nxt
    out_ref[...] = x + 0.5 * gate_ref[...] * _rms(f, gpost_ref[...])


def _ffn_call(x, mod, gpre, gpost, w_in, w_out, mix=None, lam0=0.0):
    nseq, L, D = x.shape
    bm = min(FFN_ROWS, L)
    row_spec = pl.BlockSpec((None, bm, D), lambda s, i: (s, i, 0))

    def mod_spec(m):
        if m.shape[1] == 1:
            return pl.BlockSpec((None, 1, D), lambda s, i: (s, 0, 0))
        return row_spec

    args, specs = [x], [row_spec]
    if mix is not None:
        args += [mix["p"], mix["g2"], mix["o"], mix["subg"], mix["w_br_attn"], mix["w_mix_out"], mix["gate"], mix["gpost"]]
        specs += [row_spec, row_spec, pl.BlockSpec((None, bm, ATT_WIDTH), lambda s, i: (s, i, 0)),
                  _resident((1, ATT_V_DIM)), _resident((ATT_WIDTH, D)), _resident((D, D)),
                  mod_spec(mix["gate"]), _resident((1, D))]
    args += [mod[0], mod[1], mod[2], gpre, gpost, w_in, w_out]
    specs += [mod_spec(mod[0]), mod_spec(mod[1]), mod_spec(mod[2]), _resident((1, D)), _resident((1, D)),
              _resident(w_in.shape), _resident(w_out.shape)]
    return pl.pallas_call(
        functools.partial(_ffn_kernel, with_mix=mix is not None, lam0=lam0),
        grid=(nseq, L // bm),
        in_specs=specs,
        out_specs=row_spec,
        out_shape=jax.ShapeDtypeStruct(x.shape, F32),
        compiler_params=_params(("arbitrary", "arbitrary")),
        name="merge_ffn" if mix is not None else "ffn",
    )(*args)


def _interleave(ahead, jobs):
    done = 0
    for i, job in enumerate(jobs):
        upto = -(-len(ahead) * (i + 1) // len(jobs))
        for a in ahead[done:upto]:
            a()
        done = upto
        job()


def _dwconv_jobs(src_ref, first, w, b, rows, rc, shift_ref=None):
    width, ch = w.shape
    if shift_ref is not None:
        span = shift_ref.shape[1]
        for s in sorted({(first + j) % SUBLANES for j in range(width)} - {0}):
            shift_ref[s - 1] = src_ref[s:s + span, :]

    def tap(j, r0):
        a, s = divmod(first + j, SUBLANES)
        if shift_ref is None or s == 0:
            return src_ref[first + j + r0:first + j + r0 + rc, :]
        return shift_ref[s - 1, a * SUBLANES + r0:a * SUBLANES + r0 + rc, :]

    outs = []

    def job(r0):
        acc = jnp.broadcast_to(b, (rc, ch))
        for j in range(width):
            acc = acc + w[j:j + 1, :] * tap(j, r0)
        outs.append(acc)

    return [functools.partial(job, r0) for r0 in range(0, rows, rc)], outs


def _mix_kernel(*refs, bm, q, n_prev):
    (x_ref, shift_ref, scale_ref, gpre_ref,
     wglu_ref, wz_ref, wxbc_ref, wdtr_ref, wdtt_ref, wqkv_ref, wgate_ref, bgate_ref,
     cw_ref, cb_ref, lng_ref, lnb_ref, wbrc_ref,
     sw_ref, sb_ref, dtbr_ref, dtbc_ref, alr_ref, alc_ref, dr_ref, ng_ref, wbrs_ref,
     conv0_ref, sconv0_ref, ssd0_ref,
     p_ref, g2_ref, k_ref, v_ref, qb_ref, kb_ref, vb_ref, convn_ref, sconvn_ref, ssdn_ref,
     conv_s, sconv_s, state_s, shift_s) = refs[n_prev:]

    @pl.when(pl.program_id(1) == 0)
    def _():
        conv_s[0:CONV_PAD, :] = conv0_ref[...]
        sconv_s[0:SCONV_PAD, :] = sconv0_ref[...]
        state_s[...] = ssd0_ref[...].reshape(SSD_INNER, SSD_STATE).T

    h = (_rms(x_ref[...], gpre_ref[...]) * (1.0 + scale_ref[...]) + shift_ref[...]).astype(BF16)

    glu = _dot(h, wglu_ref[...])
    conv_s[CONV_PAD:CONV_PAD + bm, :] = glu[:, :CONV_CH] * _sigmoid(glu[:, CONV_CH:])
    sconv_s[SCONV_PAD:SCONV_PAD + bm, :] = _dot(h, wxbc_ref[...])
    dt_raw_t = _dot_nt(wdtt_ref[...], h)
    slices = {}

    def project(name, w_ref):
        def job(c0):
            slices[name].append(_dot(h, w_ref[:, c0:c0 + MIX_COLS]))
        slices[name] = []
        return [functools.partial(job, c0) for c0 in range(0, w_ref.shape[1], MIX_COLS)]

    projections = (project("dt", wdtr_ref) + project("z", wz_ref) + project("qkv", wqkv_ref)
                   + project("gate", wgate_ref))
    rc = min(CONV_ROWS, bm)
    conv_jobs, conv_rows = _dwconv_jobs(conv_s, CONV_PAD - (CONV_WIDTH - 1), cw_ref[...], cb_ref[...], bm, rc, shift_s)
    sconv_jobs, sconv_rows = _dwconv_jobs(sconv_s, SCONV_PAD - (SSD_CONV_WIDTH - 1), sw_ref[...], sb_ref[...], bm, rc)
    _interleave(projections, conv_jobs + sconv_jobs)
    dt_raw, z, qkv, gate_logits = (jnp.concatenate(slices[n], axis=1) for n in ("dt", "z", "qkv", "gate"))

    k = qkv[:, ATT_WIDTH:2 * ATT_WIDTH]
    v = qkv[:, 2 * ATT_WIDTH:]
    per_head = lambda t: jnp.stack([t[:, hd * ATT_V_DIM:(hd + 1) * ATT_V_DIM] for hd in range(ATT_HEADS)], axis=1)
    k_ref[...] = per_head(k)
    v_ref[...] = per_head(v)
    qb_ref[...] = (qkv[:, :ATT_WIDTH] * (ATT_SCALE * LOG2E)).astype(BF16)
    kb_ref[...] = k.astype(BF16)
    vb_ref[...] = v.astype(BF16)

    a = jnp.concatenate(conv_rows, axis=0)
    mu = jnp.mean(a, axis=-1, keepdims=True)
    ac = a - mu
    a = ac * lax.rsqrt(jnp.mean(ac * ac, axis=-1, keepdims=True) + CONV_LN_EPS) * lng_ref[...] + lnb_ref[...]
    a_out = _dot(_silu(a).astype(BF16), wbrc_ref[...])
    convn_ref[...] = conv_s[CONV_PAD + bm - (CONV_WIDTH - 1):CONV_PAD + bm, :]
    conv_s[0:CONV_PAD, :] = conv_s[bm:bm + CONV_PAD, :]

    xbc = _silu(jnp.concatenate(sconv_rows, axis=0))
    sconvn_ref[...] = sconv_s[SCONV_PAD + bm - (SSD_CONV_WIDTH - 1):SCONV_PAD + bm, :]
    sconv_s[0:SCONV_PAD, :] = sconv_s[bm:bm + SCONV_PAD, :]

    dt_e = jax.nn.softplus(dt_raw + dtbr_ref[...])
    dt_t = jax.nn.softplus(dt_raw_t + dtbc_ref[...])
    a_rep = -jnp.exp(alr_ref[...])
    a_col = -jnp.exp(alc_ref[...])
    row = lax.broadcasted_iota(jnp.int32, (q, q), 0)
    col = lax.broadcasted_iota(jnp.int32, (q, q), 1)
    causal = row >= col
    tril = causal.astype(BF16)
    triu = (row <= col).astype(BF16)
    gw = SSD_INNER // SSD_GROUPS
    ys = []
    for c in range(bm // q):
        rows = slice(c * q, (c + 1) * q)
        dt_c = dt_e[rows]
        hi, mid, lo = _split3(dt_c * a_rep)
        acum = _dot(tril, hi) + _dot(tril, mid) + _dot(tril, lo)
        hi, mid, lo = _split3(dt_t[:, rows] * a_col)
        acum_t = _dot(hi, triu) + _dot(mid, triu) + _dot(lo, triu)
        xs_c = xbc[rows, :SSD_INNER]
        xdt = xs_c * dt_c
        a_last = acum[q - 1:q, :]
        x_end = (xdt * jnp.exp(a_last - acum)).astype(BF16)
        e_in = jnp.exp(acum)
        xdt_b = xdt.astype(BF16)
        state = state_s[...]
        state_b = state.astype(BF16)
        y_parts = []
        for g in range(SSD_GROUPS):
            bg = xbc[rows, SSD_INNER + g * SSD_STATE:SSD_INNER + (g + 1) * SSD_STATE]
            cg = xbc[rows, SSD_INNER + (SSD_GROUPS + g) * SSD_STATE:SSD_INNER + (SSD_GROUPS + g + 1) * SSD_STATE]
            bg_b, cg_b = bg.astype(BF16), cg.astype(BF16)
            cb = _dot_nt(cg_b, bg_b)
            y_off = _dot(cg_b, state_b[:, g * gw:(g + 1) * gw]) * e_in[:, g * gw:(g + 1) * gw]
            y_diag = []
            for r in range(SSD_HEADS // SSD_GROUPS):
                hd = g * (SSD_HEADS // SSD_GROUPS) + r
                lo_l = hd * SSD_HEAD_DIM
                seg = acum[:, lo_l:lo_l + q] - acum_t[hd:hd + 1, :]
                m = (cb * jnp.exp(jnp.where(causal, seg, NEG_BIG))).astype(BF16)
                y_diag.append(_dot(m, xdt_b[:, lo_l:lo_l + SSD_HEAD_DIM]))
            y_parts.append(jnp.concatenate(y_diag, axis=1) + y_off)
            state_s[:, g * gw:(g + 1) * gw] = (state[:, g * gw:(g + 1) * gw] * jnp.exp(a_last[:, g * gw:(g + 1) * gw])
                                               + _dot_tn(bg_b, x_end[:, g * gw:(g + 1) * gw]))
        ys.append(jnp.concatenate(y_parts, axis=1) + dr_ref[...] * xs_c)

    @pl.when(pl.program_id(1) == pl.num_programs(1) - 1)
    def _():
        ssdn_ref[...] = state_s[...].T.reshape(SSD_HEADS, SSD_HEAD_DIM, SSD_STATE)

    y = jnp.concatenate(ys, axis=0) if len(ys) > 1 else ys[0]
    yz = y * _silu(z)
    yz = jnp.concatenate([_rms(yz[:, g * gw:(g + 1) * gw], 1.0) for g in range(SSD_GROUPS)], axis=1) * ng_ref[...]
    b_out = _dot(yz.astype(BF16), wbrs_ref[...])

    gates = _sigmoid(gate_logits + bgate_ref[...])
    p_ref[...] = (gates[:, :D_MODEL] * a_out + gates[:, D_MODEL:2 * D_MODEL] * b_out).astype(BF16)
    g2_ref[...] = gates[:, 2 * D_MODEL:].astype(BF16)


def _mix_call(x, shift, scale, gpre, w, conv0, sconv0, ssd0, layer, depth, kv_prev):
    nseq, L, D = x.shape
    bm = min(MIX_ROWS, L)
    q = min(CHUNK, bm)
    kv_shape = jax.ShapeDtypeStruct((depth, nseq, L, ATT_HEADS, ATT_V_DIM), F32)
    kv_spec = pl.BlockSpec((None, None, bm, ATT_HEADS, ATT_V_DIM), lambda s, i: (layer, s, i, 0, 0))
    prev = list(kv_prev or ())
    rows = lambda n: pl.BlockSpec((None, bm, n), lambda s, i: (s, i, 0))
    per_seq = lambda a: pl.BlockSpec((None,) + a.shape[1:], lambda s, i: (s,) + (0,) * (len(a.shape) - 1))
    weights = [gpre, w["w_glu"], w["w_z"], w["w_xbc"], w["w_dt_rep"], w["w_dt_t"], w["w_qkv"], w["w_gate"], w["b_gate"],
               w["conv_w"], w["conv_b"], w["ln_g"], w["ln_b"], w["w_br_conv"],
               w["sconv_w"], w["sconv_b"], w["dtb_rep"], w["dtb_col"], w["alog_rep"], w["alog_col"], w["d_rep"],
               w["norm_g"], w["w_br_ssd"]]
    out_shape = [jax.ShapeDtypeStruct((nseq, L, D), BF16), jax.ShapeDtypeStruct((nseq, L, D), BF16), kv_shape, kv_shape,
                 jax.ShapeDtypeStruct((nseq, L, ATT_WIDTH), BF16), jax.ShapeDtypeStruct((nseq, L, ATT_WIDTH), BF16),
                 jax.ShapeDtypeStruct((nseq, L, ATT_WIDTH), BF16),
                 jax.ShapeDtypeStruct((nseq, CONV_WIDTH - 1, CONV_CH), F32),
                 jax.ShapeDtypeStruct((nseq, SSD_CONV_WIDTH - 1, SSD_CONV_CH), F32),
                 jax.ShapeDtypeStruct((nseq, SSD_HEADS, SSD_HEAD_DIM, SSD_STATE), F32)]
    out_specs = [rows(D), rows(D), kv_spec, kv_spec, rows(ATT_WIDTH), rows(ATT_WIDTH), rows(ATT_WIDTH),
                 per_seq(out_shape[7]), per_seq(out_shape[8]), per_seq(out_shape[9])]
    return pl.pallas_call(
        functools.partial(_mix_kernel, bm=bm, q=q, n_prev=len(prev)),
        grid=(nseq, L // bm),
        in_specs=[pl.BlockSpec(memory_space=pl.ANY)] * len(prev)
                 + [rows(D), per_seq(shift), per_seq(scale)] + [_resident(a.shape) for a in weights]
                 + [per_seq(conv0), per_seq(sconv0), per_seq(ssd0)],
        input_output_aliases={i: 2 + i for i in range(len(prev))},
        out_specs=out_specs,
        out_shape=out_shape,
        scratch_shapes=[pltpu.VMEM((CONV_PAD + bm, CONV_CH), F32), pltpu.VMEM((SCONV_PAD + bm, SSD_CONV_CH), F32),
                        pltpu.VMEM((SSD_STATE, SSD_INNER), F32),
                        pltpu.VMEM((SUBLANES - 1, CONV_PAD + bm - SUBLANES, CONV_CH), F32)],
        compiler_params=_params(("arbitrary", "arbitrary")),
        name="mix_front",
    )(*prev, x, shift, scale, *weights, conv0, sconv0, ssd0)


def _lambda(lq_ref, lk_ref, lam0):
    lq, lk = lq_ref[...], lk_ref[...]
    e0 = jnp.exp(jnp.sum(lq[0:1] * lk[0:1], axis=-1, keepdims=True))
    e1 = jnp.exp(jnp.sum(lq[1:2] * lk[1:2], axis=-1, keepdims=True))
    return e0 - e1 + lam0


def _attn_prompt_kernel(q_ref, k_ref, vt_ref, ext_ref, corr_ref, lq_ref, lk_ref, o_ref,
                        qa_s, acc_s, m_s, kn_s, sa_s, sb_s, *, lam0):
    blk, qw = ATT_BLOCK, ATT_QBLOCK
    per_q = qw // blk
    qi = pl.program_id(1)
    first = lax.broadcasted_iota(jnp.int32, (blk, LANES), 1) < ATT_QK_DIM
    lane_q = lax.broadcasted_iota(jnp.int32, (qw, LANES), 1)
    first_q = lane_q < ATT_QK_DIM
    ones_from = lambda c: jnp.where((lane_q >= c) & (lane_q < c + BIAS_TERMS), 1.0, 0.0).astype(BF16)
    head_cols = lambda h: slice(h * ATT_V_DIM, (h + 1) * ATT_V_DIM)
    qk_shift = ATT_QK_DIM.bit_length() - 1
    sel = (jnp.right_shift(lax.broadcasted_iota(jnp.int32, (ATT_WIDTH, LANES), 0), qk_shift)
           == lax.broadcasted_iota(jnp.int32, (ATT_WIDTH, LANES), 1)).astype(BF16)

    def max_sq_norm(x):
        xf = x.astype(F32)
        return jnp.max(_dot((xf * xf).astype(BF16), sel), axis=0, keepdims=True)

    @pl.when(qi == 0)
    def _():
        def body(t, mx):
            return jnp.maximum(mx, max_sq_norm(k_ref[pl.ds(pl.multiple_of(t * blk, blk), blk), :]))
        kn_s[0:1, :] = lax.fori_loop(0, k_ref.shape[0] // blk, body, jnp.zeros((1, LANES), F32))

    q = q_ref[...]
    for h in range(ATT_HEADS):
        qa_s[2 * h] = jnp.where(first_q, q[:, head_cols(h)], ones_from(ATT_QK_DIM))
        qa_s[2 * h + 1] = jnp.where(first_q, ones_from(0), q[:, head_cols(h)])
    acc_s[...] = jnp.zeros_like(acc_s)
    m_s[...] = jnp.full(m_s.shape, NEG_BIG, F32)
    ones_rows = jnp.ones((ATT_ONES_ROWS, blk), BF16)

    slot_of = lambda pos, h, t, c, nt: 2 * (h if nt == 1 else pos * nt + t) + c

    def score_jobs(kjs, heads, s_ref, q0=0):
        def job(pos, h, t, c):
            kh = k_ref[pl.ds(pl.multiple_of(kjs[t] * blk, blk), blk), head_cols(h)]
            ka = jnp.where(first, kh, ext_ref[h, 0]) if c == 0 else jnp.where(first, ext_ref[h, 1], kh)
            s_ref[slot_of(pos, h, t, c, len(kjs)), :, q0:] = _dot_nt(ka, qa_s[2 * h + c, q0:, :])
        return [functools.partial(job, pos, h, t, c)
                for pos, h in enumerate(heads) for c in range(2) for t in range(len(kjs))]

    def fold_jobs(kjs, heads, s_ref, diagonal=None):
        q0 = 0 if diagonal is None else diagonal * blk
        k0s = [pl.multiple_of(kj * blk, blk) for kj in kjs]
        nt = len(k0s)

        def job(pos, h, c):
            bases = [(ALIBI_SLOPES[h] * LOG2E) * k0.astype(F32) for k0 in k0s]
            vt = jnp.concatenate([vt_ref[head_cols(h), pl.ds(k0, blk)] for k0 in k0s], axis=1)
            vt = jnp.concatenate([vt, jnp.concatenate([ones_rows] * nt, axis=1)], axis=0)
            sc = [s_ref[slot_of(pos, h, t, c, nt), :, q0:] for t in range(nt)]
            if diagonal is not None:
                sc = [x + (-2.0 * ALIBI_SLOPES[h] * LOG2E) * corr_ref[diagonal, :, q0:] for x in sc]
            m_old = m_s[2 * h + c:2 * h + c + 1, q0:]
            m_new = m_old
            for t in range(nt):
                m_new = jnp.maximum(m_new, jnp.max(sc[t], axis=0, keepdims=True) + bases[t])
            m_s[2 * h + c:2 * h + c + 1, q0:] = m_new
            p = jnp.concatenate([jnp.exp2(sc[t] - (m_new - bases[t])).astype(BF16) for t in range(nt)], axis=0)
            acc_s[2 * h + c, :, q0:] = jnp.exp2(m_old - m_new) * acc_s[2 * h + c, :, q0:] + _dot(vt, p)
        return [functools.partial(job, pos, h, c) for pos, h in enumerate(heads) for c in range(2)]

    def run(jobs):
        for job in jobs:
            job()

    def interleave(ahead, folds):
        per = len(ahead) // len(folds)
        for f, job in enumerate(folds):
            run(ahead[f * per:(f + 1) * per])
            job()

    bufs = (sa_s, sb_s)
    all_heads = range(ATT_HEADS)
    run(score_jobs([per_q * qi], all_heads, bufs[0]))
    for d in range(per_q):
        if d + 1 < per_q:
            ahead = score_jobs([per_q * qi + d + 1], all_heads, bufs[(d + 1) % 2], q0=(d + 1) * blk)
        else:
            ahead = score_jobs([jnp.maximum(per_q * qi - 1, 0)], all_heads, bufs[(d + 1) % 2])
        interleave(ahead, fold_jobs([per_q * qi + d], all_heads, bufs[d % 2], diagonal=d))

    bound = jnp.sqrt(max_sq_norm(q)) * jnp.sqrt(kn_s[0:1, :]) * NORM_SLACK
    top = per_q * qi
    lo, limit = [], top
    for h in range(ATT_HEADS):
        gap = None
        for c in range(2):
            g = bound[:, 2 * h + c:2 * h + c + 1] - jnp.min(m_s[2 * h + c:2 * h + c + 1, :], axis=1, keepdims=True)
            gap = g if gap is None else jnp.maximum(gap, g)
        x = (-ATT_SKIP_LOG2 - gap) / (ALIBI_SLOPES[h] * LOG2E)
        need = jnp.ceil((x - (blk - 1)) / blk)
        need = jnp.clip(jnp.where(need == need, need, 0.0), 0.0, top.astype(F32))
        limit = jnp.minimum(limit, need.astype(jnp.int32)[0, 0])
        lo.append(limit)

    hi = top
    steps = per_q
    widths = [w for _, w in ATT_PHASES]
    for i, (heads, width) in enumerate(ATT_PHASES):
        align = max(width, widths[min(i + 1, len(widths) - 1)]).bit_length() - 1
        lo_i = jnp.left_shift(jnp.right_shift(lo[heads[0]], align), align)

        n_i = jnp.right_shift(hi - lo_i, width.bit_length() - 1)
        tiles = lambda j, hi=hi, width=width: [jnp.maximum(hi - (j + 1) * width + t, 0) for t in range(width)]

        if width == 1:
            first_buf = steps
            steps = steps + n_i
        else:
            first_buf = 0

            @pl.when(n_i > 0)
            def _(tiles=tiles, heads=heads):
                run(score_jobs(tiles(0), heads, bufs[0]))

        def body(j, carry, tiles=tiles, heads=heads, first_buf=first_buf):
            def trip(cur, nxt):
                interleave(score_jobs(tiles(j + 1), heads, nxt), fold_jobs(tiles(j), heads, cur))
            lax.cond(jnp.bitwise_and(j + first_buf, 1) == 0, lambda: trip(bufs[0], bufs[1]), lambda: trip(bufs[1], bufs[0]))
            return carry

        lax.fori_loop(0, n_i, body, 0)
        hi = lo_i

    lam = _lambda(lq_ref, lk_ref, lam0)
    for h in range(ATT_HEADS):
        a1, a2 = acc_s[2 * h], acc_s[2 * h + 1]
        o_t = (a1[:ATT_V_DIM] / a1[ATT_V_DIM:ATT_V_DIM + 1]
               - lam * (a2[:ATT_V_DIM] / a2[ATT_V_DIM:ATT_V_DIM + 1]))
        o_ref[:, head_cols(h)] = o_t.T


def _attn_tables():
    blk, qw = ATT_BLOCK, ATT_QBLOCK
    pos = np.arange(blk, dtype=np.float64)
    ext = np.zeros((ATT_HEADS, 2, blk, LANES), np.float32)
    to_bf16 = lambda x: x.astype(ml_dtypes.bfloat16).astype(np.float64)
    for h, slope in enumerate(ALIBI_SLOPES):
        rest = slope * LOG2E * pos
        for t in range(BIAS_TERMS):
            term = to_bf16(rest)
            ext[h, 0, :, ATT_QK_DIM + t] = term
            ext[h, 1, :, t] = term
            rest = rest - term
    ahead = np.zeros((qw // blk, blk, qw), np.float32)
    qpos = np.arange(qw, dtype=np.float64)[None, :]
    for d in range(qw // blk):
        kpos = (d * blk + pos)[:, None]
        ahead[d] = np.where((kpos // CHUNK) <= (qpos // CHUNK), np.maximum(kpos - qpos, 0.0), -NEG_BIG)
    return jnp.asarray(ext, BF16), jnp.asarray(ahead, F32)


def _attn_prompt_call(qb, kb, vb, lq, lk, lam0):
    b, S, W = qb.shape
    blk, qw = ATT_BLOCK, ATT_QBLOCK
    vt = jnp.swapaxes(vb, 1, 2)
    ext, ahead = _attn_tables()
    whole = lambda shape: pl.BlockSpec((None,) + shape, lambda s, i: (s, 0, 0), pipeline_mode=pl.Buffered(1))
    return pl.pallas_call(
        functools.partial(_attn_prompt_kernel, lam0=lam0),
        grid=(b, S // qw),
        in_specs=[pl.BlockSpec((None, qw, W), lambda s, i: (s, i, 0)), whole((S, W)), whole((W, S)),
                  _resident(ext.shape), _resident(ahead.shape), _resident(lq.shape), _resident(lk.shape)],
        out_specs=pl.BlockSpec((None, qw, W), lambda s, i: (s, i, 0)),
        out_shape=jax.ShapeDtypeStruct((b, S, W), F32),
        scratch_shapes=[pltpu.VMEM((2 * ATT_HEADS, qw, LANES), BF16),
                        pltpu.VMEM((2 * ATT_HEADS, ATT_V_DIM + ATT_ONES_ROWS, qw), F32),
                        pltpu.VMEM((2 * ATT_HEADS, qw), F32), pltpu.VMEM((SUBLANES, LANES), F32),
                        pltpu.VMEM((2 * ATT_HEADS, blk, qw), F32), pltpu.VMEM((2 * ATT_HEADS, blk, qw), F32)],
        compiler_params=_params(("arbitrary", "arbitrary")),
        name="attn_prompt",
    )(qb, kb, vt, ext, ahead, lq, lk)


def _attn_sample_kernel(q_ref, kn_ref, vn_ref, kc_ref, vc_ref, lq_ref, lk_ref, o_ref, *, lam0):
    ls, past = q_ref.shape[0], kc_ref.shape[0]
    lam = _lambda(lq_ref, lk_ref, lam0)
    lane = lax.broadcasted_iota(jnp.int32, (ls, LANES), 1)
    first = lane < ATT_QK_DIM

    def bias_mask(nk, k_first):
        qpos = past + lax.broadcasted_iota(jnp.int32, (ls, nk), 0)
        kpos = k_first + lax.broadcasted_iota(jnp.int32, (ls, nk), 1)
        dist = jnp.abs(qpos - kpos).astype(F32)
        visible = jnp.right_shift(kpos, CHUNK_SHIFT) <= jnp.right_shift(qpos, CHUNK_SHIFT)
        return dist, visible

    dist_c, vis_c = bias_mask(past, 0)
    dist_n, vis_n = bias_mask(ls, past)
    outs = []
    for h in range(ATT_HEADS):
        cols = slice(h * ATT_V_DIM, (h + 1) * ATT_V_DIM)
        qh = q_ref[:, cols]
        kc, kn = kc_ref[:, h, :].astype(BF16), kn_ref[:, cols]
        vc, vn = vc_ref[:, h, :].astype(BF16), vn_ref[:, cols]
        o = []
        for c in range(2):
            qm = jnp.where(first == (c == 0), qh, jnp.zeros_like(qh))
            sc = jnp.where(vis_c, _dot_nt(qm, kc) - (ALIBI_SLOPES[h] * LOG2E) * dist_c, NEG_BIG)
            sn = jnp.where(vis_n, _dot_nt(qm, kn) - (ALIBI_SLOPES[h] * LOG2E) * dist_n, NEG_BIG)
            m = jnp.maximum(jnp.max(sc, axis=-1, keepdims=True), jnp.max(sn, axis=-1, keepdims=True))
            pc, pn = jnp.exp2(sc - m), jnp.exp2(sn - m)
            l = jnp.sum(pc, axis=-1, keepdims=True) + jnp.sum(pn, axis=-1, keepdims=True)
            o.append((_dot(pc.astype(BF16), vc) + _dot(pn.astype(BF16), vn)) / l)
        outs.append(o[0] - lam * o[1])
    o_ref[...] = jnp.concatenate(outs, axis=1)


def _attn_sample_call(qb, kb, vb, cache_k, cache_v, layer, lq, lk, lam0):
    b, ls, W = qb.shape
    past = cache_k.shape[2]
    new = pl.BlockSpec((None, ls, W), lambda s: (s, 0, 0))
    old = pl.BlockSpec((None, None, past, ATT_HEADS, ATT_V_DIM), lambda s: (layer, s, 0, 0, 0))
    return pl.pallas_call(
        functools.partial(_attn_sample_kernel, lam0=lam0),
        grid=(b,),
        in_specs=[new, new, new, old, old, _resident(lq.shape), _resident(lk.shape)],
        out_specs=new,
        out_shape=jax.ShapeDtypeStruct((b, ls, W), F32),
        compiler_params=_params(("arbitrary",)),
        name="attn_sample",
    )(qb, kb, vb, cache_k, cache_v, lq, lk)


def _prep_layer(l, w_in, b_gate, conv_dw_w, conv_dw_b, conv_ln_g, conv_ln_b, w_br_conv, ssd_conv_w, ssd_conv_b,
                ssd_dt_bias, ssd_A_log, ssd_D, ssd_norm_g, w_br_ssd):
    splits = (2 * CONV_CH, SSD_INNER, SSD_CONV_CH, SSD_HEADS, 3 * ATT_WIDTH, N_BRANCH * D_MODEL)
    pts = np.cumsum(splits)[:-1].tolist()
    w_glu, w_z, w_xbc, w_dt, w_qkv, w_gate = (p.astype(BF16) for p in jnp.split(w_in[l], pts, axis=1))
    rep = lambda a: jnp.repeat(a, SSD_HEAD_DIM, axis=-1)
    return dict(
        w_glu=w_glu, w_z=w_z, w_xbc=w_xbc, w_dt_rep=rep(w_dt), w_dt_t=w_dt.T, w_qkv=w_qkv, w_gate=w_gate,
        b_gate=b_gate[l].reshape(1, N_BRANCH * D_MODEL),
        conv_w=conv_dw_w[l], conv_b=conv_dw_b[l][None], ln_g=conv_ln_g[l][None], ln_b=conv_ln_b[l][None],
        w_br_conv=w_br_conv[l].astype(BF16),
        sconv_w=ssd_conv_w[l], sconv_b=ssd_conv_b[l][None],
        dtb_rep=rep(ssd_dt_bias[l])[None], dtb_col=ssd_dt_bias[l][:, None],
        alog_rep=rep(ssd_A_log[l])[None], alog_col=ssd_A_log[l][:, None],
        d_rep=rep(ssd_D[l])[None], norm_g=ssd_norm_g[l][None], w_br_ssd=w_br_ssd[l].astype(BF16))


def _state_in(conv, sconv, ssd):
    conv = jnp.pad(conv, ((0, 0), (CONV_PAD - (CONV_WIDTH - 1), 0), (0, 0)))
    sconv = jnp.pad(sconv, ((0, 0), (SCONV_PAD - (SSD_CONV_WIDTH - 1), 0), (0, 0)))
    return conv, sconv, ssd


def kernel(x_prompt, x_sample, cache_attn_k, cache_attn_v, state_conv, state_ssd_conv, state_ssd, c_prompt, c_sample,
           w_ada, b_ada, norm_pre, norm_post, w_ffn_in, w_ffn_out, w_in, b_gate, conv_dw_w, conv_dw_b, conv_ln_g,
           conv_ln_b, w_br_conv, ssd_conv_w, ssd_conv_b, ssd_dt_bias, ssd_A_log, ssd_D, ssd_norm_g, w_br_ssd,
           lambda_q, lambda_k, attn_subln_g, w_br_attn, w_mix_out):
    depth = w_ada.shape[0]
    bp, S, D = x_prompt.shape
    bs, ls, _ = x_sample.shape

    mods = _ada_call(jnp.concatenate([c_prompt, c_sample], axis=0), w_ada, b_ada)
    mods = mods.reshape(depth, bp + bs, N_SUB, 3, D)

    zeros = _state_in(jnp.zeros((bp, CONV_WIDTH - 1, CONV_CH), F32), jnp.zeros((bp, SSD_CONV_WIDTH - 1, SSD_CONV_CH), F32),
                      jnp.zeros((bp, SSD_HEADS, SSD_HEAD_DIM, SSD_STATE), F32))
    xp = x_prompt
    xs = x_sample.reshape(1, bs * ls, D)
    st_p, st_s = [], []
    kv_p = kv_s = None
    for l in range(depth):
        lam0 = _lambda_init(l)
        w = _prep_layer(l, w_in, b_gate, conv_dw_w, conv_dw_b, conv_ln_g, conv_ln_b, w_br_conv, ssd_conv_w, ssd_conv_b,
                        ssd_dt_bias, ssd_A_log, ssd_D, ssd_norm_g, w_br_ssd)
        wf_in, wf_out = w_ffn_in[l].astype(BF16), w_ffn_out[l].astype(BF16)
        w_bra, w_mix = w_br_attn[l].astype(BF16), w_mix_out[l].astype(BF16)
        gpre = lambda s: norm_pre[l, s][None]
        gpost = lambda s: norm_post[l, s][None]
        subg = attn_subln_g[l][None]
        mp = lambda s, k: mods[l, :bp, s, k][:, None, :]
        ms_seq = lambda s, k: mods[l, bp:, s, k][:, None, :]
        ms_tok = lambda s, k: jnp.repeat(mods[l, bp:, s, k], ls, axis=0)[None]

        xp = _ffn_call(xp, (mp(0, 0), mp(0, 1), mp(0, 2)), gpre(0), gpost(0), wf_in[0], wf_out[0])
        p, g2, *kv_p, qb, kb, vb, convn, sconvn, ssdn = _mix_call(xp, mp(1, 0), mp(1, 1), gpre(1), w, *zeros,
                                                                  l, depth, kv_p)
        o = _attn_prompt_call(qb, kb, vb, lambda_q[l], lambda_k[l], lam0)
        mix = dict(p=p, g2=g2, o=o, subg=subg, w_br_attn=w_bra, w_mix_out=w_mix, gate=mp(1, 2), gpost=gpost(1))
        xp = _ffn_call(xp, (mp(2, 0), mp(2, 1), mp(2, 2)), gpre(2), gpost(2), wf_in[1], wf_out[1], mix=mix, lam0=lam0)
        st_p.append((convn, sconvn, ssdn))

        xs = _ffn_call(xs, (ms_tok(0, 0), ms_tok(0, 1), ms_tok(0, 2)), gpre(0), gpost(0), wf_in[0], wf_out[0])
        states = _state_in(state_conv[l], state_ssd_conv[l], state_ssd[l])
        p, g2, *kv_s, qb, kb, vb, convn, sconvn, ssdn = _mix_call(xs.reshape(bs, ls, D), ms_seq(1, 0), ms_seq(1, 1),
                                                                  gpre(1), w, *states, l, depth, kv_s)
        o = _attn_sample_call(qb, kb, vb, cache_attn_k, cache_attn_v, l, lambda_q[l], lambda_k[l], lam0)
        flat = lambda t: t.reshape(1, bs * ls, t.shape[-1])
        mix = dict(p=flat(p), g2=flat(g2), o=flat(o), subg=subg, w_br_attn=w_bra, w_mix_out=w_mix, gate=ms_tok(1, 2),
                   gpost=gpost(1))
        xs = _ffn_call(xs, (ms_tok(2, 0), ms_tok(2, 1), ms_tok(2, 2)), gpre(2), gpost(2), wf_in[1], wf_out[1], mix=mix,
                       lam0=lam0)
        st_s.append((convn, sconvn, ssdn))

    stack = lambda st, i: jnp.stack([s[i] for s in st])
    return (xp, xs.reshape(bs, ls, D),
            kv_p[0], kv_p[1], stack(st_p, 0), stack(st_p, 1), stack(st_p, 2),
            kv_s[0], kv_s[1], stack(st_s, 0), stack(st_s, 1), stack(st_s, 2))
```

```python
import functools
import math

import ml_dtypes
import numpy as np
import jax
import jax.numpy as jnp
from jax import lax
from jax.experimental import pallas as pl
from jax.experimental.pallas import tpu as pltpu

F32, BF16 = jnp.float32, jnp.bfloat16

D_MODEL = 1024
D_FF = 2816
CONV_CH = 512
CONV_WIDTH = 31
SSD_HEADS = 16
SSD_HEAD_DIM = 64
SSD_INNER = SSD_HEADS * SSD_HEAD_DIM
SSD_GROUPS = 2
SSD_STATE = 128
SSD_CONV_WIDTH = 4
SSD_CONV_CH = SSD_INNER + 2 * SSD_GROUPS * SSD_STATE
CHUNK = 64
CHUNK_SHIFT = 6
ATT_HEADS = 4
ATT_QK_DIM = 64
ATT_V_DIM = 2 * ATT_QK_DIM
ATT_WIDTH = ATT_HEADS * ATT_V_DIM
ATT_SCALE = ATT_QK_DIM ** -0.5
ALIBI_SLOPES = tuple(2.0 ** (-8.0 * (h + 1) / ATT_HEADS) for h in range(ATT_HEADS))
N_BRANCH = 3
N_SUB = 3
EPS = 1e-6
SUBLN_EPS = 1e-5
CONV_LN_EPS = 1e-5
NEG_BIG = -1e30
LOG2E = math.log2(math.e)
BIAS_TERMS = 3
ATT_ONES_ROWS = 16
ATT_SKIP_LOG2 = 152.0
NORM_SLACK = 1.0 + 2.0 ** -6
ATT_PHASES = (((0, 1, 2, 3), 1), ((1, 2, 3), 1), ((2, 3), 2), ((3,), 4))
assert all(len(heads) * width <= ATT_HEADS for heads, width in ATT_PHASES)

VMEM_LIMIT_BYTES = 58 * 1024 * 1024
LANES = 128
SUBLANES = 8

FFN_ROWS = 512
FFN_EDGES = (0, 768, 1792, D_FF)
MIX_COLS = 512
CONV_ROWS = 32
MIX_ROWS = 256
ATT_BLOCK = 256
ATT_QBLOCK = 512
ADA_COLS = 1152
CONV_PAD = 32
SCONV_PAD = 8


def _lambda_init(layer):
    return 0.8 - 0.6 * math.exp(-0.3 * layer)


def _params(sem):
    return pltpu.CompilerParams(dimension_semantics=sem, vmem_limit_bytes=VMEM_LIMIT_BYTES)


def _resident(shape):
    nd = len(shape)
    return pl.BlockSpec(shape, lambda *_: (0,) * nd, pipeline_mode=pl.Buffered(1))


def _dot(a, b):
    return jnp.dot(a, b, preferred_element_type=F32)


def _dot_nt(a, b):
    return lax.dot_general(a, b, (((1,), (1,)), ((), ())), preferred_element_type=F32)


def _dot_tn(a, b):
    return lax.dot_general(a, b, (((0,), (0,)), ((), ())), preferred_element_type=F32)


def _rms(x, g, eps=EPS):
    return x * lax.rsqrt(jnp.mean(x * x, axis=-1, keepdims=True) + eps) * g


def _sigmoid(x):
    return 0.5 * jnp.tanh(0.5 * x) + 0.5


def _silu(x):
    return x * _sigmoid(x)


def _split3(a):
    hi = a.astype(BF16)
    r = a - hi.astype(F32)
    mid = r.astype(BF16)
    lo = (r - mid.astype(F32)).astype(BF16)
    return hi, mid, lo


def _ada_kernel(c_ref, w_ref, b_ref, o_ref):
    sc = _silu(c_ref[...]).astype(BF16)
    o_ref[...] = _dot(sc, w_ref[...].astype(BF16)) + b_ref[...]


def _ada_call(c_all, w_ada, b_ada):
    depth, d, n = w_ada.shape
    rows = c_all.shape[0]
    return pl.pallas_call(
        _ada_kernel,
        grid=(depth, n // ADA_COLS),
        in_specs=[pl.BlockSpec((rows, d), lambda l, j: (0, 0)),
                  pl.BlockSpec((None, d, ADA_COLS), lambda l, j: (l, 0, j)),
                  pl.BlockSpec((None, 1, ADA_COLS), lambda l, j: (l, 0, j))],
        out_specs=pl.BlockSpec((None, rows, ADA_COLS), lambda l, j: (l, 0, j)),
        out_shape=jax.ShapeDtypeStruct((depth, rows, n), F32),
        compiler_params=_params(("arbitrary", "arbitrary")),
        name="adaln",
    )(c_all, w_ada, b_ada.reshape(depth, 1, n))


def _ffn_kernel(*refs, with_mix, lam0):
    if with_mix:
        (x_ref, p_ref, g2_ref, o_ref, subg_ref, wbr_ref, wmix_ref, gate1_ref, gpost1_ref, *rest) = refs
    else:
        x_ref, *rest = refs
    shift_ref, scale_ref, gate_ref, gpre_ref, gpost_ref, win_ref, wout_ref, out_ref = rest
    x = x_ref[...]
    if with_mix:
        o = o_ref[...]
        subg = subg_ref[...]
        heads = []
        for h in range(ATT_HEADS):
            oh = o[:, h * ATT_V_DIM:(h + 1) * ATT_V_DIM]
            heads.append(_rms(oh, subg, SUBLN_EPS) * (1.0 - lam0))
        c_out = _dot(jnp.concatenate(heads, axis=1).astype(BF16), wbr_ref[...])
        merged = (p_ref[...] + g2_ref[...] * c_out).astype(BF16)
        x = x + gate1_ref[...] * _rms(_dot(merged, wmix_ref[...]), gpost1_ref[...])
    h = (_rms(x, gpre_ref[...]) * (1.0 + scale_ref[...]) + shift_ref[...]).astype(BF16)
    edges = FFN_EDGES

    def project(c):
        return (_dot(h, win_ref[:, edges[c]:edges[c + 1]]), _dot(h, win_ref[:, D_FF + edges[c]:D_FF + edges[c + 1]]))

    f, cur = None, project(0)
    for c in range(len(edges) - 1):
        nxt = project(c + 1) if c + 2 < len(edges) else None
        up, gt = cur
        part = _dot((_silu(gt) * up).astype(BF16), wout_ref[edges[c]:edges[c + 1], :])
        f = part if f is None else f + part
        cur = nxt
    out_ref[...] = x + 0.5 * gate_ref[...] * _rms(f, gpost_ref[...])


def _ffn_call(x, mod, gpre, gpost, w_in, w_out, mix=None, lam0=0.0):
    nseq, L, D = x.shape
    bm = min(FFN_ROWS, L)
    row_spec = pl.BlockSpec((None, bm, D), lambda s, i: (s, i, 0))

    def mod_spec(m):
        if m.shape[1] == 1:
            return pl.BlockSpec((None, 1, D), lambda s, i: (s, 0, 0))
        return row_spec

    args, specs = [x], [row_spec]
    if mix is not None:
        args += [mix["p"], mix["g2"], mix["o"], mix["subg"], mix["w_br_attn"], mix["w_mix_out"], mix["gate"], mix["gpost"]]
        specs += [row_spec, row_spec, pl.BlockSpec((None, bm, ATT_WIDTH), lambda s, i: (s, i, 0)),
                  _resident((1, ATT_V_DIM)), _resident((ATT_WIDTH, D)), _resident((D, D)),
                  mod_spec(mix["gate"]), _resident((1, D))]
    args += [mod[0], mod[1], mod[2], gpre, gpost, w_in, w_out]
    specs += [mod_spec(mod[0]), mod_spec(mod[1]), mod_spec(mod[2]), _resident((1, D)), _resident((1, D)),
              _resident(w_in.shape), _resident(w_out.shape)]
    return pl.pallas_call(
        functools.partial(_ffn_kernel, with_mix=mix is not None, lam0=lam0),
        grid=(nseq, L // bm),
        in_specs=specs,
        out_specs=row_spec,
        out_shape=jax.ShapeDtypeStruct(x.shape, F32),
        compiler_params=_params(("arbitrary", "arbitrary")),
        name="merge_ffn" if mix is not None else "ffn",
    )(*args)


def _interleave(ahead, jobs):
    done = 0
    for i, job in enumerate(jobs):
        upto = -(-len(ahead) * (i + 1) // len(jobs))
        for a in ahead[done:upto]:
            a()
        done = upto
        job()


def _dwconv_jobs(src_ref, first, w, b, rows, rc, shift_ref=None):
    width, ch = w.shape
    if shift_ref is not None:
        span = shift_ref.shape[1]
        for s in sorted({(first + j) % SUBLANES for j in range(width)} - {0}):
            shift_ref[s - 1] = src_ref[s:s + span, :]

    def tap(j, r0):
        a, s = divmod(first + j, SUBLANES)
        if shift_ref is None or s == 0:
            return src_ref[first + j + r0:first + j + r0 + rc, :]
        return shift_ref[s - 1, a * SUBLANES + r0:a * SUBLANES + r0 + rc, :]

    outs = []

    def job(r0):
        acc = jnp.broadcast_to(b, (rc, ch))
        for j in range(width):
            acc = acc + w[j:j + 1, :] * tap(j, r0)
        outs.append(acc)

    return [functools.partial(job, r0) for r0 in range(0, rows, rc)], outs


def _mix_kernel(*refs, bm, q, n_prev, layer):
    (x_ref, shift_ref, scale_ref, gpre_ref,
     wglu_ref, wz_ref, wxbc_ref, wdtr_ref, wdtt_ref, wqkv_ref, wgate_ref, bgate_ref,
     cw_ref, cb_ref, lng_ref, lnb_ref, wbrc_ref,
     sw_ref, sb_ref, dtbr_ref, dtbc_ref, alr_ref, alc_ref, dr_ref, ng_ref, wbrs_ref,
     conv0_ref, sconv0_ref, ssd0_ref,
     p_ref, g2_ref, k_ref, v_ref, qb_ref, kb_ref, vb_ref, convn_ref, sconvn_ref, ssdn_ref,
     conv_s, sconv_s, state_s, shift_s, k_s, v_s, kv_sem) = refs[n_prev:]

    def kv_copies(seq, block):
        return [pltpu.make_async_copy(src.at[:, pl.ds(hd * ATT_V_DIM, ATT_V_DIM)],
                                      dst.at[layer, seq, pl.ds(block * bm, bm), hd, :], kv_sem.at[t * ATT_HEADS + hd])
                for t, (src, dst) in enumerate(((k_s, k_ref), (v_s, v_ref))) for hd in range(ATT_HEADS)]

    seq, block = pl.program_id(0), pl.program_id(1)
    is_first = jnp.logical_and(seq == 0, block == 0)
    is_last = jnp.logical_and(seq == pl.num_programs(0) - 1, block == pl.num_programs(1) - 1)

    @pl.when(pl.program_id(1) == 0)
    def _():
        conv_s[0:CONV_PAD, :] = conv0_ref[...]
        sconv_s[0:SCONV_PAD, :] = sconv0_ref[...]
        state_s[...] = ssd0_ref[...].reshape(SSD_INNER, SSD_STATE).T

    h = (_rms(x_ref[...], gpre_ref[...]) * (1.0 + scale_ref[...]) + shift_ref[...]).astype(BF16)

    glu = _dot(h, wglu_ref[...])
    conv_s[CONV_PAD:CONV_PAD + bm, :] = glu[:, :CONV_CH] * _sigmoid(glu[:, CONV_CH:])
    sconv_s[SCONV_PAD:SCONV_PAD + bm, :] = _dot(h, wxbc_ref[...])
    dt_raw_t = _dot_nt(wdtt_ref[...], h)
    slices = {}

    def project(name, w_ref):
        def job(c0):
            slices[name].append(_dot(h, w_ref[:, c0:c0 + MIX_COLS]))
        slices[name] = []
        return [functools.partial(job, c0) for c0 in range(0, w_ref.shape[1], MIX_COLS)]

    projections = (project("dt", wdtr_ref) + project("z", wz_ref) + project("qkv", wqkv_ref)
                   + project("gate", wgate_ref))
    rc = min(CONV_ROWS, bm)
    conv_jobs, conv_rows = _dwconv_jobs(conv_s, CONV_PAD - (CONV_WIDTH - 1), cw_ref[...], cb_ref[...], bm, rc, shift_s)
    sconv_jobs, sconv_rows = _dwconv_jobs(sconv_s, SCONV_PAD - (SSD_CONV_WIDTH - 1), sw_ref[...], sb_ref[...], bm, rc)
    _interleave(projections, conv_jobs + sconv_jobs)
    dt_raw, z, qkv, gate_logits = (jnp.concatenate(slices[n], axis=1) for n in ("dt", "z", "qkv", "gate"))

    k = qkv[:, ATT_WIDTH:2 * ATT_WIDTH]
    v = qkv[:, 2 * ATT_WIDTH:]
    @pl.when(jnp.logical_not(is_first))
    def _():
        for cp in kv_copies(0, 0):
            cp.wait()

    k_s[...] = k
    v_s[...] = v
    for cp in kv_copies(seq, block):
        cp.start()
    qb_ref[...] = (qkv[:, :ATT_WIDTH] * (ATT_SCALE * LOG2E)).astype(BF16)
    kb_ref[...] = k.astype(BF16)
    vb_ref[...] = v.astype(BF16)

    a = jnp.concatenate(conv_rows, axis=0)
    mu = jnp.mean(a, axis=-1, keepdims=True)
    ac = a - mu
    a = ac * lax.rsqrt(jnp.mean(ac * ac, axis=-1, keepdims=True) + CONV_LN_EPS) * lng_ref[...] + lnb_ref[...]
    a_out = _dot(_silu(a).astype(BF16), wbrc_ref[...])
    convn_ref[...] = conv_s[CONV_PAD + bm - (CONV_WIDTH - 1):CONV_PAD + bm, :]
    conv_s[0:CONV_PAD, :] = conv_s[bm:bm + CONV_PAD, :]

    xbc = _silu(jnp.concatenate(sconv_rows, axis=0))
    sconvn_ref[...] = sconv_s[SCONV_PAD + bm - (SSD_CONV_WIDTH - 1):SCONV_PAD + bm, :]
    sconv_s[0:SCONV_PAD, :] = sconv_s[bm:bm + SCONV_PAD, :]

    dt_e = jax.nn.softplus(dt_raw + dtbr_ref[...])
    dt_t = jax.nn.softplus(dt_raw_t + dtbc_ref[...])
    a_rep = -jnp.exp(alr_ref[...])
    a_col = -jnp.exp(alc_ref[...])
    row = lax.broadcasted_iota(jnp.int32, (q, q), 0)
    col = lax.broadcasted_iota(jnp.int32, (q, q), 1)
    causal = row >= col
    tril = causal.astype(BF16)
    triu = (row <= col).astype(BF16)
    gw = SSD_INNER // SSD_GROUPS
    ys = []
    for c in range(bm // q):
        rows = slice(c * q, (c + 1) * q)
        dt_c = dt_e[rows]
        hi, mid, lo = _split3(dt_c * a_rep)
        acum = _dot(tril, hi) + _dot(tril, mid) + _dot(tril, lo)
        hi, mid, lo = _split3(dt_t[:, rows] * a_col)
        acum_t = _dot(hi, triu) + _dot(mid, triu) + _dot(lo, triu)
        xs_c = xbc[rows, :SSD_INNER]
        xdt = xs_c * dt_c
        a_last = acum[q - 1:q, :]
        x_end = (xdt * jnp.exp(a_last - acum)).astype(BF16)
        e_in = jnp.exp(acum)
        xdt_b = xdt.astype(BF16)
        state = state_s[...]
        state_b = state.astype(BF16)
        y_parts = []
        for g in range(SSD_GROUPS):
            bg = xbc[rows, SSD_INNER + g * SSD_STATE:SSD_INNER + (g + 1) * SSD_STATE]
            cg = xbc[rows, SSD_INNER + (SSD_GROUPS + g) * SSD_STATE:SSD_INNER + (SSD_GROUPS + g + 1) * SSD_STATE]
            bg_b, cg_b = bg.astype(BF16), cg.astype(BF16)
            cb = _dot_nt(cg_b, bg_b)
            y_off = _dot(cg_b, state_b[:, g * gw:(g + 1) * gw]) * e_in[:, g * gw:(g + 1) * gw]
            y_diag = []
            for r in range(SSD_HEADS // SSD_GROUPS):
                hd = g * (SSD_HEADS // SSD_GROUPS) + r
                lo_l = hd * SSD_HEAD_DIM
                seg = acum[:, lo_l:lo_l + q] - acum_t[hd:hd + 1, :]
                m = (cb * jnp.exp(jnp.where(causal, seg, NEG_BIG))).astype(BF16)
                y_diag.append(_dot(m, xdt_b[:, lo_l:lo_l + SSD_HEAD_DIM]))
            y_parts.append(jnp.concatenate(y_diag, axis=1) + y_off)
            state_s[:, g * gw:(g + 1) * gw] = (state[:, g * gw:(g + 1) * gw] * jnp.exp(a_last[:, g * gw:(g + 1) * gw])
                                               + _dot_tn(bg_b, x_end[:, g * gw:(g + 1) * gw]))
        ys.append(jnp.concatenate(y_parts, axis=1) + dr_ref[...] * xs_c)

    @pl.when(pl.program_id(1) == pl.num_programs(1) - 1)
    def _():
        ssdn_ref[...] = state_s[...].T.reshape(SSD_HEADS, SSD_HEAD_DIM, SSD_STATE)

    @pl.when(is_last)
    def _():
        for cp in kv_copies(seq, block):
            cp.wait()

    y = jnp.concatenate(ys, axis=0) if len(ys) > 1 else ys[0]
    yz = y * _silu(z)
    yz = jnp.concatenate([_rms(yz[:, g * gw:(g + 1) * gw], 1.0) for g in range(SSD_GROUPS)], axis=1) * ng_ref[...]
    b_out = _dot(yz.astype(BF16), wbrs_ref[...])

    gates = _sigmoid(gate_logits + bgate_ref[...])
    p_ref[...] = (gates[:, :D_MODEL] * a_out + gates[:, D_MODEL:2 * D_MODEL] * b_out).astype(BF16)
    g2_ref[...] = gates[:, 2 * D_MODEL:].astype(BF16)


def _mix_call(x, shift, scale, gpre, w, conv0, sconv0, ssd0, layer, depth, kv_prev):
    nseq, L, D = x.shape
    bm = min(MIX_ROWS, L)
    q = min(CHUNK, bm)
    kv_shape = jax.ShapeDtypeStruct((depth, nseq, L, ATT_HEADS, ATT_V_DIM), F32)
    kv_spec = pl.BlockSpec(memory_space=pl.ANY)
    prev = list(kv_prev or ())
    rows = lambda n: pl.BlockSpec((None, bm, n), lambda s, i: (s, i, 0))
    per_seq = lambda a: pl.BlockSpec((None,) + a.shape[1:], lambda s, i: (s,) + (0,) * (len(a.shape) - 1))
    weights = [gpre, w["w_glu"], w["w_z"], w["w_xbc"], w["w_dt_rep"], w["w_dt_t"], w["w_qkv"], w["w_gate"], w["b_gate"],
               w["conv_w"], w["conv_b"], w["ln_g"], w["ln_b"], w["w_br_conv"],
               w["sconv_w"], w["sconv_b"], w["dtb_rep"], w["dtb_col"], w["alog_rep"], w["alog_col"], w["d_rep"],
               w["norm_g"], w["w_br_ssd"]]
    out_shape = [jax.ShapeDtypeStruct((nseq, L, D), BF16), jax.ShapeDtypeStruct((nseq, L, D), BF16), kv_shape, kv_shape,
                 jax.ShapeDtypeStruct((nseq, L, ATT_WIDTH), BF16), jax.ShapeDtypeStruct((nseq, L, ATT_WIDTH), BF16),
                 jax.ShapeDtypeStruct((nseq, L, ATT_WIDTH), BF16),
                 jax.ShapeDtypeStruct((nseq, CONV_WIDTH - 1, CONV_CH), F32),
                 jax.ShapeDtypeStruct((nseq, SSD_CONV_WIDTH - 1, SSD_CONV_CH), F32),
                 jax.ShapeDtypeStruct((nseq, SSD_HEADS, SSD_HEAD_DIM, SSD_STATE), F32)]
    out_specs = [rows(D), rows(D), kv_spec, kv_spec, rows(ATT_WIDTH), rows(ATT_WIDTH), rows(ATT_WIDTH),
                 per_seq(out_shape[7]), per_seq(out_shape[8]), per_seq(out_shape[9])]
    return pl.pallas_call(
        functools.partial(_mix_kernel, bm=bm, q=q, n_prev=len(prev), layer=layer),
        grid=(nseq, L // bm),
        in_specs=[pl.BlockSpec(memory_space=pl.ANY)] * len(prev)
                 + [rows(D), per_seq(shift), per_seq(scale)] + [_resident(a.shape) for a in weights]
                 + [per_seq(conv0), per_seq(sconv0), per_seq(ssd0)],
        input_output_aliases={i: 2 + i for i in range(len(prev))},
        out_specs=out_specs,
        out_shape=out_shape,
        scratch_shapes=[pltpu.VMEM((CONV_PAD + bm, CONV_CH), F32), pltpu.VMEM((SCONV_PAD + bm, SSD_CONV_CH), F32),
                        pltpu.VMEM((SSD_STATE, SSD_INNER), F32),
                        pltpu.VMEM((SUBLANES - 1, CONV_PAD + bm - SUBLANES, CONV_CH), F32),
                        pltpu.VMEM((bm, ATT_WIDTH), F32), pltpu.VMEM((bm, ATT_WIDTH), F32),
                        pltpu.SemaphoreType.DMA((2 * ATT_HEADS,))],
        compiler_params=_params(("arbitrary", "arbitrary")),
        name="mix_front",
    )(*prev, x, shift, scale, *weights, conv0, sconv0, ssd0)


def _lambda(lq_ref, lk_ref, lam0):
    lq, lk = lq_ref[...], lk_ref[...]
    e0 = jnp.exp(jnp.sum(lq[0:1] * lk[0:1], axis=-1, keepdims=True))
    e1 = jnp.exp(jnp.sum(lq[1:2] * lk[1:2], axis=-1, keepdims=True))
    return e0 - e1 + lam0


def _attn_prompt_kernel(q_ref, k_ref, vt_ref, ext_ref, corr_ref, lq_ref, lk_ref, o_ref,
                        qa_s, acc_s, m_s, kn_s, sa_s, sb_s, *, lam0):
    blk, qw = ATT_BLOCK, ATT_QBLOCK
    per_q = qw // blk
    qi = pl.program_id(1)
    first = lax.broadcasted_iota(jnp.int32, (blk, LANES), 1) < ATT_QK_DIM
    lane_q = lax.broadcasted_iota(jnp.int32, (qw, LANES), 1)
    first_q = lane_q < ATT_QK_DIM
    ones_from = lambda c: jnp.where((lane_q >= c) & (lane_q < c + BIAS_TERMS), 1.0, 0.0).astype(BF16)
    head_cols = lambda h: slice(h * ATT_V_DIM, (h + 1) * ATT_V_DIM)
    qk_shift = ATT_QK_DIM.bit_length() - 1
    sel = (jnp.right_shift(lax.broadcasted_iota(jnp.int32, (ATT_WIDTH, LANES), 0), qk_shift)
           == lax.broadcasted_iota(jnp.int32, (ATT_WIDTH, LANES), 1)).astype(BF16)

    def max_sq_norm(x):
        xf = x.astype(F32)
        return jnp.max(_dot((xf * xf).astype(BF16), sel), axis=0, keepdims=True)

    @pl.when(qi == 0)
    def _():
        def body(t, mx):
            return jnp.maximum(mx, max_sq_norm(k_ref[pl.ds(pl.multiple_of(t * blk, blk), blk), :]))
        kn_s[0:1, :] = lax.fori_loop(0, k_ref.shape[0] // blk, body, jnp.zeros((1, LANES), F32))

    q = q_ref[...]
    for h in range(ATT_HEADS):
        qa_s[2 * h] = jnp.where(first_q, q[:, head_cols(h)], ones_from(ATT_QK_DIM))
        qa_s[2 * h + 1] = jnp.where(first_q, ones_from(0), q[:, head_cols(h)])
    acc_s[...] = jnp.zeros_like(acc_s)
    m_s[...] = jnp.full(m_s.shape, NEG_BIG, F32)
    ones_rows = jnp.ones((ATT_ONES_ROWS, blk), BF16)

    slot_of = lambda pos, h, t, c, nt: 2 * (h if nt == 1 else pos * nt + t) + c

    def score_jobs(kjs, heads, s_ref, q0=0):
        def job(pos, h, t, c):
            kh = k_ref[pl.ds(pl.multiple_of(kjs[t] * blk, blk), blk), head_cols(h)]
            ka = jnp.where(first, kh, ext_ref[h, 0]) if c == 0 else jnp.where(first, ext_ref[h, 1], kh)
            s_ref[slot_of(pos, h, t, c, len(kjs)), :, q0:] = _dot_nt(ka, qa_s[2 * h + c, q0:, :])
        return [functools.partial(job, pos, h, t, c)
                for pos, h in enumerate(heads) for c in range(2) for t in range(len(kjs))]

    def fold_jobs(kjs, heads, s_ref, diagonal=None):
        q0 = 0 if diagonal is None else diagonal * blk
        k0s = [pl.multiple_of(kj * blk, blk) for kj in kjs]
        nt = len(k0s)

        def job(pos, h, c):
            bases = [(ALIBI_SLOPES[h] * LOG2E) * k0.astype(F32) for k0 in k0s]
            vt = jnp.concatenate([vt_ref[head_cols(h), pl.ds(k0, blk)] for k0 in k0s], axis=1)
            vt = jnp.concatenate([vt, jnp.concatenate([ones_rows] * nt, axis=1)], axis=0)
            sc = [s_ref[slot_of(pos, h, t, c, nt), :, q0:] for t in range(nt)]
            if diagonal is not None:
                sc = [x + (-2.0 * ALIBI_SLOPES[h] * LOG2E) * corr_ref[diagonal, :, q0:] for x in sc]
            m_old = m_s[2 * h + c:2 * h + c + 1, q0:]
            m_new = m_old
            for t in range(nt):
                m_new = jnp.maximum(m_new, jnp.max(sc[t], axis=0, keepdims=True) + bases[t])
            m_s[2 * h + c:2 * h + c + 1, q0:] = m_new
            p = jnp.concatenate([jnp.exp2(sc[t] - (m_new - bases[t])).astype(BF16) for t in range(nt)], axis=0)
            acc_s[2 * h + c, :, q0:] = jnp.exp2(m_old - m_new) * acc_s[2 * h + c, :, q0:] + _dot(vt, p)
        return [functools.partial(job, pos, h, c) for pos, h in enumerate(heads) for c in range(2)]

    def run(jobs):
        for job in jobs:
            job()

    def interleave(ahead, folds):
        per = len(ahead) // len(folds)
        for f, job in enumerate(folds):
            run(ahead[f * per:(f + 1) * per])
            job()

    bufs = (sa_s, sb_s)
    all_heads = range(ATT_HEADS)
    run(score_jobs([per_q * qi], all_heads, bufs[0]))
    for d in range(per_q):
        if d + 1 < per_q:
            ahead = score_jobs([per_q * qi + d + 1], all_heads, bufs[(d + 1) % 2], q0=(d + 1) * blk)
        else:
            ahead = score_jobs([jnp.maximum(per_q * qi - 1, 0)], all_heads, bufs[(d + 1) % 2])
        interleave(ahead, fold_jobs([per_q * qi + d], all_heads, bufs[d % 2], diagonal=d))

    bound = jnp.sqrt(max_sq_norm(q)) * jnp.sqrt(kn_s[0:1, :]) * NORM_SLACK
    top = per_q * qi
    lo, limit = [], top
    for h in range(ATT_HEADS):
        gap = None
        for c in range(2):
            g = bound[:, 2 * h + c:2 * h + c + 1] - jnp.min(m_s[2 * h + c:2 * h + c + 1, :], axis=1, keepdims=True)
            gap = g if gap is None else jnp.maximum(gap, g)
        x = (-ATT_SKIP_LOG2 - gap) / (ALIBI_SLOPES[h] * LOG2E)
        need = jnp.ceil((x - (blk - 1)) / blk)
        need = jnp.clip(jnp.where(need == need, need, 0.0), 0.0, top.astype(F32))
        limit = jnp.minimum(limit, need.astype(jnp.int32)[0, 0])
        lo.append(limit)

    hi = top
    steps = per_q
    widths = [w for _, w in ATT_PHASES]
    for i, (heads, width) in enumerate(ATT_PHASES):
        align = max(width, widths[min(i + 1, len(widths) - 1)]).bit_length() - 1
        lo_i = jnp.left_shift(jnp.right_shift(lo[heads[0]], align), align)

        n_i = jnp.right_shift(hi - lo_i, width.bit_length() - 1)
        tiles = lambda j, hi=hi, width=width: [jnp.maximum(hi - (j + 1) * width + t, 0) for t in range(width)]

        if width == 1:
            first_buf = steps
            steps = steps + n_i
        else:
            first_buf = 0

            @pl.when(n_i > 0)
            def _(tiles=tiles, heads=heads):
                run(score_jobs(tiles(0), heads, bufs[0]))

        def body(j, carry, tiles=tiles, heads=heads, first_buf=first_buf):
            def trip(cur, nxt):
                interleave(score_jobs(tiles(j + 1), heads, nxt), fold_jobs(tiles(j), heads, cur))
            lax.cond(jnp.bitwise_and(j + first_buf, 1) == 0, lambda: trip(bufs[0], bufs[1]), lambda: trip(bufs[1], bufs[0]))
            return carry

        lax.fori_loop(0, n_i, body, 0)
        hi = lo_i

    lam = _lambda(lq_ref, lk_ref, lam0)
    for h in range(ATT_HEADS):
        a1, a2 = acc_s[2 * h], acc_s[2 * h + 1]
        o_t = (a1[:ATT_V_DIM] / a1[ATT_V_DIM:ATT_V_DIM + 1]
               - lam * (a2[:ATT_V_DIM] / a2[ATT_V_DIM:ATT_V_DIM + 1]))
        o_ref[:, head_cols(h)] = o_t.T


def _attn_tables():
    blk, qw = ATT_BLOCK, ATT_QBLOCK
    pos = np.arange(blk, dtype=np.float64)
    ext = np.zeros((ATT_HEADS, 2, blk, LANES), np.float32)
    to_bf16 = lambda x: x.astype(ml_dtypes.bfloat16).astype(np.float64)
    for h, slope in enumerate(ALIBI_SLOPES):
        rest = slope * LOG2E * pos
        for t in range(BIAS_TERMS):
            term = to_bf16(rest)
            ext[h, 0, :, ATT_QK_DIM + t] = term
            ext[h, 1, :, t] = term
            rest = rest - term
    ahead = np.zeros((qw // blk, blk, qw), np.float32)
    qpos = np.arange(qw, dtype=np.float64)[None, :]
    for d in range(qw // blk):
        kpos = (d * blk + pos)[:, None]
        ahead[d] = np.where((kpos // CHUNK) <= (qpos // CHUNK), np.maximum(kpos - qpos, 0.0), -NEG_BIG)
    return jnp.asarray(ext, BF16), jnp.asarray(ahead, F32)


def _attn_prompt_call(qb, kb, vb, lq, lk, lam0):
    b, S, W = qb.shape
    blk, qw = ATT_BLOCK, ATT_QBLOCK
    vt = jnp.swapaxes(vb, 1, 2)
    ext, ahead = _attn_tables()
    whole = lambda shape: pl.BlockSpec((None,) + shape, lambda s, i: (s, 0, 0), pipeline_mode=pl.Buffered(1))
    return pl.pallas_call(
        functools.partial(_attn_prompt_kernel, lam0=lam0),
        grid=(b, S // qw),
        in_specs=[pl.BlockSpec((None, qw, W), lambda s, i: (s, i, 0)), whole((S, W)), whole((W, S)),
                  _resident(ext.shape), _resident(ahead.shape), _resident(lq.shape), _resident(lk.shape)],
        out_specs=pl.BlockSpec((None, qw, W), lambda s, i: (s, i, 0)),
        out_shape=jax.ShapeDtypeStruct((b, S, W), F32),
        scratch_shapes=[pltpu.VMEM((2 * ATT_HEADS, qw, LANES), BF16),
                        pltpu.VMEM((2 * ATT_HEADS, ATT_V_DIM + ATT_ONES_ROWS, qw), F32),
                        pltpu.VMEM((2 * ATT_HEADS, qw), F32), pltpu.VMEM((SUBLANES, LANES), F32),
                        pltpu.VMEM((2 * ATT_HEADS, blk, qw), F32), pltpu.VMEM((2 * ATT_HEADS, blk, qw), F32)],
        compiler_params=_params(("arbitrary", "arbitrary")),
        name="attn_prompt",
    )(qb, kb, vt, ext, ahead, lq, lk)


def _attn_sample_kernel(q_ref, kn_ref, vn_ref, kc_ref, vc_ref, lq_ref, lk_ref, o_ref, *, lam0):
    ls, past = q_ref.shape[0], kc_ref.shape[0]
    lam = _lambda(lq_ref, lk_ref, lam0)
    lane = lax.broadcasted_iota(jnp.int32, (ls, LANES), 1)
    first = lane < ATT_QK_DIM

    def bias_mask(nk, k_first):
        qpos = past + lax.broadcasted_iota(jnp.int32, (ls, nk), 0)
        kpos = k_first + lax.broadcasted_iota(jnp.int32, (ls, nk), 1)
        dist = jnp.abs(qpos - kpos).astype(F32)
        visible = jnp.right_shift(kpos, CHUNK_SHIFT) <= jnp.right_shift(qpos, CHUNK_SHIFT)
        return dist, visible

    dist_c, vis_c = bias_mask(past, 0)
    dist_n, vis_n = bias_mask(ls, past)
    outs = []
    for h in range(ATT_HEADS):
        cols = slice(h * ATT_V_DIM, (h + 1) * ATT_V_DIM)
        qh = q_ref[:, cols]
        kc, kn = kc_ref[:, h, :].astype(BF16), kn_ref[:, cols]
        vc, vn = vc_ref[:, h, :].astype(BF16), vn_ref[:, cols]
        o = []
        for c in range(2):
            qm = jnp.where(first == (c == 0), qh, jnp.zeros_like(qh))
            sc = jnp.where(vis_c, _dot_nt(qm, kc) - (ALIBI_SLOPES[h] * LOG2E) * dist_c, NEG_BIG)
            sn = jnp.where(vis_n, _dot_nt(qm, kn) - (ALIBI_SLOPES[h] * LOG2E) * dist_n, NEG_BIG)
            m = jnp.maximum(jnp.max(sc, axis=-1, keepdims=True), jnp.max(sn, axis=-1, keepdims=True))
            pc, pn = jnp.exp2(sc - m), jnp.exp2(sn - m)
            l = jnp.sum(pc, axis=-1, keepdims=True) + jnp.sum(pn, axis=-1, keepdims=True)
            o.append((_dot(pc.astype(BF16), vc) + _dot(pn.astype(BF16), vn)) / l)
        outs.append(o[0] - lam * o[1])
    o_ref[...] = jnp.concatenate(outs, axis=1)


def _attn_sample_call(qb, kb, vb, cache_k, cache_v, layer, lq, lk, lam0):
    b, ls, W = qb.shape
    past = cache_k.shape[2]
    new = pl.BlockSpec((None, ls, W), lambda s: (s, 0, 0))
    old = pl.BlockSpec((None, None, past, ATT_HEADS, ATT_V_DIM), lambda s: (layer, s, 0, 0, 0))
    return pl.pallas_call(
        functools.partial(_attn_sample_kernel, lam0=lam0),
        grid=(b,),
        in_specs=[new, new, new, old, old, _resident(lq.shape), _resident(lk.shape)],
        out_specs=new,
        out_shape=jax.ShapeDtypeStruct((b, ls, W), F32),
        compiler_params=_params(("arbitrary",)),
        name="attn_sample",
    )(qb, kb, vb, cache_k, cache_v, lq, lk)


def _prep_layer(l, w_in, b_gate, conv_dw_w, conv_dw_b, conv_ln_g, conv_ln_b, w_br_conv, ssd_conv_w, ssd_conv_b,
                ssd_dt_bias, ssd_A_log, ssd_D, ssd_norm_g, w_br_ssd):
    splits = (2 * CONV_CH, SSD_INNER, SSD_CONV_CH, SSD_HEADS, 3 * ATT_WIDTH, N_BRANCH * D_MODEL)
    pts = np.cumsum(splits)[:-1].tolist()
    w_glu, w_z, w_xbc, w_dt, w_qkv, w_gate = (p.astype(BF16) for p in jnp.split(w_in[l], pts, axis=1))
    rep = lambda a: jnp.repeat(a, SSD_HEAD_DIM, axis=-1)
    return dict(
        w_glu=w_glu, w_z=w_z, w_xbc=w_xbc, w_dt_rep=rep(w_dt), w_dt_t=w_dt.T, w_qkv=w_qkv, w_gate=w_gate,
        b_gate=b_gate[l].reshape(1, N_BRANCH * D_MODEL),
        conv_w=conv_dw_w[l], conv_b=conv_dw_b[l][None], ln_g=conv_ln_g[l][None], ln_b=conv_ln_b[l][None],
        w_br_conv=w_br_conv[l].astype(BF16),
        sconv_w=ssd_conv_w[l], sconv_b=ssd_conv_b[l][None],
        dtb_rep=rep(ssd_dt_bias[l])[None], dtb_col=ssd_dt_bias[l][:, None],
        alog_rep=rep(ssd_A_log[l])[None], alog_col=ssd_A_log[l][:, None],
        d_rep=rep(ssd_D[l])[None], norm_g=ssd_norm_g[l][None], w_br_ssd=w_br_ssd[l].astype(BF16))


def _state_in(conv, sconv, ssd):
    conv = jnp.pad(conv, ((0, 0), (CONV_PAD - (CONV_WIDTH - 1), 0), (0, 0)))
    sconv = jnp.pad(sconv, ((0, 0), (SCONV_PAD - (SSD_CONV_WIDTH - 1), 0), (0, 0)))
    return conv, sconv, ssd


def kernel(x_prompt, x_sample, cache_attn_k, cache_attn_v, state_conv, state_ssd_conv, state_ssd, c_prompt, c_sample,
           w_ada, b_ada, norm_pre, norm_post, w_ffn_in, w_ffn_out, w_in, b_gate, conv_dw_w, conv_dw_b, conv_ln_g,
           conv_ln_b, w_br_conv, ssd_conv_w, ssd_conv_b, ssd_dt_bias, ssd_A_log, ssd_D, ssd_norm_g, w_br_ssd,
           lambda_q, lambda_k, attn_subln_g, w_br_attn, w_mix_out):
    depth = w_ada.shape[0]
    bp, S, D = x_prompt.shape
    bs, ls, _ = x_sample.shape

    mods = _ada_call(jnp.concatenate([c_prompt, c_sample], axis=0), w_ada, b_ada)
    mods = mods.reshape(depth, bp + bs, N_SUB, 3, D)

    zeros = _state_in(jnp.zeros((bp, CONV_WIDTH - 1, CONV_CH), F32), jnp.zeros((bp, SSD_CONV_WIDTH - 1, SSD_CONV_CH), F32),
                      jnp.zeros((bp, SSD_HEADS, SSD_HEAD_DIM, SSD_STATE), F32))
    xp = x_prompt
    xs = x_sample.reshape(1, bs * ls, D)
    st_p, st_s = [], []
    kv_p = kv_s = None
    for l in range(depth):
        lam0 = _lambda_init(l)
        w = _prep_layer(l, w_in, b_gate, conv_dw_w, conv_dw_b, conv_ln_g, conv_ln_b, w_br_conv, ssd_conv_w, ssd_conv_b,
                        ssd_dt_bias, ssd_A_log, ssd_D, ssd_norm_g, w_br_ssd)
        wf_in, wf_out = w_ffn_in[l].astype(BF16), w_ffn_out[l].astype(BF16)
        w_bra, w_mix = w_br_attn[l].astype(BF16), w_mix_out[l].astype(BF16)
        gpre = lambda s: norm_pre[l, s][None]
        gpost = lambda s: norm_post[l, s][None]
        subg = attn_subln_g[l][None]
        mp = lambda s, k: mods[l, :bp, s, k][:, None, :]
        ms_seq = lambda s, k: mods[l, bp:, s, k][:, None, :]
        ms_tok = lambda s, k: jnp.repeat(mods[l, bp:, s, k], ls, axis=0)[None]

        xp = _ffn_call(xp, (mp(0, 0), mp(0, 1), mp(0, 2)), gpre(0), gpost(0), wf_in[0], wf_out[0])
        p, g2, *kv_p, qb, kb, vb, convn, sconvn, ssdn = _mix_call(xp, mp(1, 0), mp(1, 1), gpre(1), w, *zeros,
                                                                  l, depth, kv_p)
        o = _attn_prompt_call(qb, kb, vb, lambda_q[l], lambda_k[l], lam0)
        mix = dict(p=p, g2=g2, o=o, subg=subg, w_br_attn=w_bra, w_mix_out=w_mix, gate=mp(1, 2), gpost=gpost(1))
        xp = _ffn_call(xp, (mp(2, 0), mp(2, 1), mp(2, 2)), gpre(2), gpost(2), wf_in[1], wf_out[1], mix=mix, lam0=lam0)
        st_p.append((convn, sconvn, ssdn))

        xs = _ffn_call(xs, (ms_tok(0, 0), ms_tok(0, 1), ms_tok(0, 2)), gpre(0), gpost(0), wf_in[0], wf_out[0])
        states = _state_in(state_conv[l], state_ssd_conv[l], state_ssd[l])
        p, g2, *kv_s, qb, kb, vb, convn, sconvn, ssdn = _mix_call(xs.reshape(bs, ls, D), ms_seq(1, 0), ms_seq(1, 1),
                                                                  gpre(1), w, *states, l, depth, kv_s)
        o = _attn_sample_call(qb, kb, vb, cache_attn_k, cache_attn_v, l, lambda_q[l], lambda_k[l], lam0)
        flat = lambda t: t.reshape(1, bs * ls, t.shape[-1])
        mix = dict(p=flat(p), g2=flat(g2), o=flat(o), subg=subg, w_br_attn=w_bra, w_mix_out=w_mix, gate=ms_tok(1, 2),
                   gpost=gpost(1))
        xs = _ffn_call(xs, (ms_tok(2, 0), ms_tok(2, 1), ms_tok(2, 2)), gpre(2), gpost(2), wf_in[1], wf_out[1], mix=mix,
                       lam0=lam0)
        st_s.append((convn, sconvn, ssdn))

    stack = lambda st, i: jnp.stack([s[i] for s in st])
    return (xp, xs.reshape(bs, ls, D),
            kv_p[0], kv_p[1], stack(st_p, 0), stack(st_p, 1), stack(st_p, 2),
            kv_s[0], kv_s[1], stack(st_s, 0), stack(st_s, 1), stack(st_s, 2))
```
